```python
import math
import jax, jax.numpy as jnp
from jax import lax
import numpy as np


D_MODEL = 1024
BATCH = 8
SEQ = 4096
DEPTH = 1
DEC_BATCH = 32
DEC_SEQ = 2048
PAST_LEN = 128

HEAD_DIM = 64
A_CONFIGS = ((128, 1), (512, 4), (2048, 16))
A_GROUPS = 3
A_HEADS_PER_GROUP = 8
A_HEADS = A_GROUPS * A_HEADS_PER_GROUP
A_WIDTH = A_HEADS * HEAD_DIM
A_OUT = A_HEADS_PER_GROUP * HEAD_DIM
B_HEADS = 8
B_KV_HEADS = 2
B_HALF_WINDOW = 128
B_Q = B_HEADS * HEAD_DIM
B_KV = B_KV_HEADS * HEAD_DIM
B_OUT = B_Q
IN_COLS = 3 * A_WIDTH + B_Q + 2 * B_KV
MEM_TOKENS = 256
X_HEADS = 4
X_HEAD_DIM = 128
X_WIDTH = X_HEADS * X_HEAD_DIM
N_GROUPS = 4
EXPERTS_PER_GROUP = 8
N_EXPERTS = N_GROUPS * EXPERTS_PER_GROUP
TOP_K = 2
EXPERT_FF = 512
MOE_BLOCK = 256
EPS = 1e-6
NEG = -1e30

kernel_name = 'hybrid_dilated_window_moe_encoder'


def rms_norm(x, g):
    xf = x.astype(jnp.float32)
    y = xf * lax.rsqrt(jnp.mean(xf * xf, axis=-1, keepdims=True) + EPS)
    return (y * g.astype(jnp.float32)).astype(x.dtype)


def alibi_slopes(n):
    return jnp.asarray(2.0 ** (-8.0 * np.arange(1, n + 1, dtype=np.float32) / n), dtype=jnp.float32)


def banded_attention(q, k, v, half_w, slopes, sink=None):
    B, L, Hq, Dh = q.shape
    G = k.shape[2]
    R = Hq // G
    W = half_w
    nb = -(-L // W)
    Lp = nb * W
    pad = Lp - L
    qp = jnp.pad(q, ((0, 0), (0, pad), (0, 0), (0, 0)))
    kp = jnp.pad(k, ((0, 0), (W, pad + W), (0, 0), (0, 0)))
    vp = jnp.pad(v, ((0, 0), (W, pad + W), (0, 0), (0, 0)))
    qb = qp.reshape(B, nb, W, G, R, Dh)
    kb = kp.reshape(B, nb + 2, W, G, Dh)
    vb = vp.reshape(B, nb + 2, W, G, Dh)
    kwin = jnp.concatenate([kb[:, :-2], kb[:, 1:-1], kb[:, 2:]], axis=2)
    vwin = jnp.concatenate([vb[:, :-2], vb[:, 1:-1], vb[:, 2:]], axis=2)
    q_pos = jnp.arange(nb)[:, None] * W + jnp.arange(W)[None, :]
    k_pos = jnp.arange(nb)[:, None] * W - W + jnp.arange(3 * W)[None, :]
    dist = jnp.abs(q_pos[:, :, None] - k_pos[:, None, :])
    valid = (dist <= W) & (k_pos[:, None, :] >= 0) & (k_pos[:, None, :] < L)
    scores = jnp.einsum('bnqgrd,bnkgd->bngrqk', qb, kwin,
                        preferred_element_type=jnp.float32) * (Dh ** -0.5)
    bias = -slopes.reshape(G, R)[None, None, :, :, None, None] * dist.astype(jnp.float32)[None, :, None, None, :, :]
    scores = jnp.where(valid[None, :, None, None, :, :], scores + bias, NEG)
    m = jnp.max(scores, axis=-1)
    if sink is not None:
        sk = sink.astype(jnp.float32).reshape(G, R)[None, None, :, :, None]
        m = jnp.maximum(m, sk)
    e = jnp.exp(scores - m[..., None])
    denom = jnp.sum(e, axis=-1)
    if sink is not None:
        denom = denom + jnp.exp(sk - m)
    o = jnp.einsum('bngrqk,bnkgd->bnqgrd', e.astype(v.dtype), vwin,
                   preferred_element_type=jnp.float32)
    denom_t = jnp.moveaxis(denom, -1, 2)
    o = o / denom_t[..., None]
    lse = jnp.moveaxis(m, -1, 2) + jnp.log(denom_t)
    o = o.reshape(B, Lp, Hq, Dh)[:, :L].astype(q.dtype)
    lse = lse.reshape(B, Lp, Hq)[:, :L]
    return o, lse


def dilated_attention(qa, ka, va):
    B, S, _, Dh = qa.shape
    Hg = A_HEADS_PER_GROUP
    slopes = alibi_slopes(A_HEADS).reshape(A_GROUPS, Hg)
    outs, lses = [], []
    for g, (window, r) in enumerate(A_CONFIGS):
        M = S // r
        def to_sub(t):
            t = t[:, :, g * Hg:(g + 1) * Hg]
            return t.reshape(B, M, r, Hg, Dh).transpose(0, 2, 1, 3, 4).reshape(B * r, M, Hg, Dh)
        o, lse = banded_attention(to_sub(qa), to_sub(ka), to_sub(va), window // (2 * r), slopes[g] * r)
        outs.append(o.reshape(B, r, M, Hg, Dh).transpose(0, 2, 1, 3, 4).reshape(B, S, Hg, Dh))
        lses.append(lse.reshape(B, r, M, Hg).transpose(0, 2, 1, 3).reshape(B, S, Hg))
    wts = jax.nn.softmax(jnp.stack(lses, axis=0), axis=0)
    o = jnp.sum(wts[..., None] * jnp.stack(outs, axis=0).astype(jnp.float32), axis=0)
    return o.reshape(B, S, Hg * Dh).astype(qa.dtype)


def token_mixers(h, w_in, sink_b, w_gate, b_gate, w_branch, w_out):
    B, S, D = h.shape
    proj = h @ w_in
    cuts = [A_WIDTH, 2 * A_WIDTH, 3 * A_WIDTH, 3 * A_WIDTH + B_Q, 3 * A_WIDTH + B_Q + B_KV]
    qa, ka, va, qb, kb, vb = jnp.split(proj, cuts, axis=-1)
    hd = lambda t, n: t.reshape(B, S, n, HEAD_DIM)
    o_a = dilated_attention(hd(qa, A_HEADS), hd(ka, A_HEADS), hd(va, A_HEADS))
    o_b, _ = banded_attention(hd(qb, B_HEADS), hd(kb, B_KV_HEADS), hd(vb, B_KV_HEADS),
                              B_HALF_WINDOW, alibi_slopes(B_HEADS), sink_b)
    o_b = o_b.reshape(B, S, B_OUT)
    gates = jax.nn.sigmoid((h @ w_gate + b_gate).astype(jnp.float32))
    br_a = (o_a @ w_branch[:A_OUT]).astype(jnp.float32)
    br_b = (o_b @ w_branch[A_OUT:]).astype(jnp.float32)
    merged = gates[..., :D] * br_a + gates[..., D:] * br_b
    return merged.astype(h.dtype) @ w_out


def memory_cross_attention(h, mem, g_mem, w_cq, w_ckv, w_co):
    B, S, _ = h.shape
    Mt = mem.shape[1]
    q = (h @ w_cq).reshape(B, S, X_HEADS, X_HEAD_DIM)
    kv = (rms_norm(mem, g_mem) @ w_ckv).reshape(B, Mt, 2, X_HEADS, X_HEAD_DIM)
    k, v = kv[:, :, 0], kv[:, :, 1]
    s = jnp.einsum('bshd,bmhd->bhsm', q, k, preferred_element_type=jnp.float32) * (X_HEAD_DIM ** -0.5)
    p = jax.nn.softmax(s, axis=-1)
    o = jnp.einsum('bhsm,bmhd->bshd', p.astype(v.dtype), v).reshape(B, S, X_WIDTH)
    return o @ w_co


def hierarchical_moe(h, w_rg, b_rg, w_re, b_re, w_gu, w_down):
    T, D = h.shape
    pg = jax.nn.softmax((h @ w_rg + b_rg).astype(jnp.float32), axis=-1)
    gidx = jnp.argmax(pg, axis=-1).astype(jnp.int32)
    pg_sel = jnp.take_along_axis(pg, gidx[:, None], axis=1)[:, 0]
    el = (h @ w_re + b_re).astype(jnp.float32).reshape(T, N_GROUPS, EXPERTS_PER_GROUP)
    el = jnp.take_along_axis(el, gidx[:, None, None], axis=1)[:, 0]
    top_v, top_i = lax.top_k(jax.nn.softmax(el, axis=-1), TOP_K)
    gate = top_v / jnp.sum(top_v, axis=-1, keepdims=True) * pg_sel[:, None]
    expert = gidx[:, None] * EXPERTS_PER_GROUP + top_i.astype(jnp.int32)
    n_slots = T * TOP_K
    flat_e = expert.reshape(-1)
    flat_w = gate.reshape(-1)
    order = jnp.argsort(flat_e).astype(jnp.int32)
    sorted_e = flat_e[order]
    counts = jnp.zeros((N_EXPERTS,), jnp.int32).at[flat_e].add(1)
    starts = jnp.cumsum(counts) - counts
    padded = (counts + MOE_BLOCK - 1) // MOE_BLOCK * MOE_BLOCK
    pends = jnp.cumsum(padded)
    pstarts = pends - padded
    dest = pstarts[sorted_e] + (jnp.arange(n_slots, dtype=jnp.int32) - starts[sorted_e])
    n_blocks = -(-n_slots // MOE_BLOCK) + N_EXPERTS
    P = n_blocks * MOE_BLOCK
    row_tok = jnp.full((P,), T, jnp.int32).at[dest].set(order // TOP_K)
    row_w = jnp.zeros((P,), jnp.float32).at[dest].set(flat_w[order])
    block_e = jnp.minimum(jnp.searchsorted(pends, jnp.arange(n_blocks, dtype=jnp.int32) * MOE_BLOCK,
                                           side='right'), N_EXPERTS - 1).astype(jnp.int32)
    h_pad = jnp.concatenate([h, jnp.zeros((1, D), h.dtype)], axis=0)

    def run_block(args):
        tok, e = args
        xb = h_pad[tok]
        gu = xb @ w_gu[e]
        return (jax.nn.silu(gu[:, :EXPERT_FF]) * gu[:, EXPERT_FF:]) @ w_down[e]

    ys = lax.map(run_block, (row_tok.reshape(n_blocks, MOE_BLOCK), block_e))
    out = jnp.zeros((T + 1, D), jnp.float32).at[row_tok].add(
        ys.reshape(P, D).astype(jnp.float32) * row_w[:, None])
    return out[:T].astype(h.dtype)


def encoder_layer(x, mem, g_mix, w_in, sink_b, w_gate, b_gate, w_branch, w_out,
                  g_xattn, g_mem, w_cq, w_ckv, w_co,
                  g_ffn, w_rg, b_rg, w_re, b_re, w_gu, w_down):
    B, S, D = x.shape
    x = x + token_mixers(rms_norm(x, g_mix), w_in, sink_b, w_gate, b_gate, w_branch, w_out)
    x = x + memory_cross_attention(rms_norm(x, g_xattn), mem, g_mem, w_cq, w_ckv, w_co)
    h = rms_norm(x, g_ffn).reshape(B * S, D)
    x = x + hierarchical_moe(h, w_rg, b_rg, w_re, b_re, w_gu, w_down).reshape(B, S, D)
    return x


def setup_inputs(seed: int = 0) -> dict:
    key = jax.random.key(seed)
    ks = iter(jax.random.split(key, 32))
    f32 = jnp.float32
    D = D_MODEL
    nrm = lambda shape, scale: jax.random.normal(next(ks), shape, f32) * scale
    gain = lambda shape: 1.0 + 0.02 * jax.random.normal(next(ks), shape, f32)
    return {
        'x_prompt': nrm((BATCH, SEQ, D), 1.0),
        'x_sample': nrm((DEC_BATCH, DEC_SEQ, D), 1.0),
        'mem_prompt': nrm((BATCH, MEM_TOKENS, D), 1.0),
        'mem_sample': nrm((DEC_BATCH, MEM_TOKENS, D), 1.0),
        'g_mix': gain((DEPTH, D)),
        'w_in': nrm((DEPTH, D, IN_COLS), D ** -0.5),
        'sink_b': nrm((DEPTH, B_HEADS), 0.5),
        'w_gate': nrm((DEPTH, D, 2 * D), D ** -0.5),
        'b_gate': nrm((DEPTH, 2 * D), 0.1),
        'w_branch': nrm((DEPTH, A_OUT + B_OUT, D), A_OUT ** -0.5),
        'w_out': nrm((DEPTH, D, D), D ** -0.5),
        'g_xattn': gain((DEPTH, D)),
        'g_mem': gain((DEPTH, D)),
        'w_cq': nrm((DEPTH, D, X_WIDTH), D ** -0.5),
        'w_ckv': nrm((DEPTH, D, 2 * X_WIDTH), D ** -0.5),
        'w_co': nrm((DEPTH, X_WIDTH, D), X_WIDTH ** -0.5),
        'g_ffn': gain((DEPTH, D)),
        'w_rg': nrm((DEPTH, D, N_GROUPS), D ** -0.5),
        'b_rg': nrm((DEPTH, N_GROUPS), 0.01),
        'w_re': nrm((DEPTH, D, N_EXPERTS), D ** -0.5),
        'b_re': nrm((DEPTH, N_EXPERTS), 0.01),
        'w_gu': nrm((DEPTH, N_EXPERTS, D, 2 * EXPERT_FF), D ** -0.5),
        'w_down': nrm((DEPTH, N_EXPERTS, EXPERT_FF, D), EXPERT_FF ** -0.5),
        'g_final': gain((D,)),
    }


def reference(x_prompt, x_sample, mem_prompt, mem_sample, g_mix, w_in, sink_b, w_gate, b_gate,
              w_branch, w_out, g_xattn, g_mem, w_cq, w_ckv, w_co, g_ffn, w_rg, b_rg, w_re, b_re,
              w_gu, w_down, g_final):
    xp, xs = x_prompt, x_sample
    for l in range(DEPTH):
        args = (g_mix[l], w_in[l], sink_b[l], w_gate[l], b_gate[l], w_branch[l], w_out[l],
                g_xattn[l], g_mem[l], w_cq[l], w_ckv[l], w_co[l],
                g_ffn[l], w_rg[l], b_rg[l], w_re[l], b_re[l], w_gu[l], w_down[l])
        xp = encoder_layer(xp, mem_prompt, *args)
        xs = encoder_layer(xs, mem_sample, *args)
    y_prompt = rms_norm(xp, g_final)
    y_sample = rms_norm(xs, g_final)
    return (y_prompt, y_sample)
```

```python
import functools

import numpy as np
import jax
import jax.numpy as jnp
from jax import lax
from jax.experimental import pallas as pl
from jax.experimental.pallas import tpu as pltpu

F32 = jnp.float32
BF16 = jnp.bfloat16

D_MODEL = 1024
HEAD_DIM = 64
A_CONFIGS = ((128, 1), (512, 4), (2048, 16))
A_GROUPS = 3
A_HEADS_PER_GROUP = 8
A_HEADS = A_GROUPS * A_HEADS_PER_GROUP
A_WIDTH = A_HEADS * HEAD_DIM
B_HEADS = 8
B_KV_HEADS = 2
B_HALF_WINDOW = 128
B_Q = B_HEADS * HEAD_DIM
B_KV = B_KV_HEADS * HEAD_DIM
X_HEADS = 4
X_HEAD_DIM = 128
X_WIDTH = X_HEADS * X_HEAD_DIM
N_GROUPS = 4
EXPERTS_PER_GROUP = 8
N_EXPERTS = N_GROUPS * EXPERTS_PER_GROUP
EXPERT_FF = 512
MOE_BLOCK = 256
EPS = 1e-6
NEG = -1e30

LANES = 128
GROUP_W = A_HEADS_PER_GROUP * HEAD_DIM
PAIR_W = 2 * HEAD_DIM
N_PAIRS = GROUP_W // PAIR_W
PROJ_BLOCKS = 11
PROJ_W = PROJ_BLOCKS * GROUP_W
KV2_W = 2 * B_KV
ROUTE_W = LANES
EXPERT_LANE0 = 32
VMEM_LIMIT = 52 * 1024 * 1024

ROW_TILE = 256
Q_TILE = 128


def _rms(x, g):
    ms = jnp.mean(x * x, axis=-1, keepdims=True)
    return x * lax.rsqrt(ms + EPS) * g


def _alibi_slopes(n):
    return 2.0 ** (-8.0 * np.arange(1, n + 1, dtype=np.float32) / n)


def _const_spec(shape):
    return pl.BlockSpec(shape, lambda *_: (0,) * len(shape), pipeline_mode=pl.Buffered(1))


def _proj_kernel(x_ref, g_ref, w_ref, o_ref):
    h = _rms(x_ref[...], g_ref[...]).astype(BF16)
    for j in range(PROJ_BLOCKS):
        cols = slice(j * GROUP_W, (j + 1) * GROUP_W)
        o_ref[:, cols] = jnp.dot(h, w_ref[:, cols], preferred_element_type=F32).astype(BF16)


def _proj(x2d, g, w):
    T = x2d.shape[0]
    return pl.pallas_call(
        _proj_kernel,
        grid=(T // ROW_TILE,),
        in_specs=[pl.BlockSpec((ROW_TILE, D_MODEL), lambda i: (i, 0)),
                  _const_spec((1, D_MODEL)),
                  _const_spec((D_MODEL, PROJ_W))],
        out_specs=pl.BlockSpec((ROW_TILE, PROJ_W), lambda i: (i, 0)),
        out_shape=jax.ShapeDtypeStruct((T, PROJ_W), BF16),
        compiler_params=pltpu.CompilerParams(dimension_semantics=("arbitrary",),
                                             vmem_limit_bytes=VMEM_LIMIT),
        name="proj",
    )(x2d, g, w)


def _attn_kernel(*refs, qb, kb, m_len, half_w, slopes, kv_shared, has_sink, want_lse):
    refs = list(refs)
    sink_ref = refs.pop(0) if has_sink else None
    q_ref, k_ref, v_ref, o_ref = refs[:4]
    lse_ref = refs[4] if want_lse else None

    i = pl.program_id(2)
    ks = jnp.clip(i * qb - half_w, 0, m_len - kb)
    ks = pl.multiple_of(ks, 16)
    kw = k_ref[pl.ds(ks, kb), :]
    vw = v_ref[pl.ds(ks, kb), :]
    qpos = i * qb + lax.broadcasted_iota(jnp.int32, (qb, kb), 0)
    kpos = ks + lax.broadcasted_iota(jnp.int32, (qb, kb), 1)
    dist = jnp.abs(qpos - kpos)
    valid = dist <= half_w
    distf = dist.astype(F32)
    lo_kv = lax.broadcasted_iota(jnp.int32, (kb, PAIR_W), 1) < HEAD_DIM
    lo_q = lax.broadcasted_iota(jnp.int32, (qb, PAIR_W), 1) < HEAD_DIM
    zeros_kv = jnp.zeros((kb, PAIR_W), BF16)

    for j in range(N_PAIRS):
        qp = q_ref[:, j * PAIR_W:(j + 1) * PAIR_W]
        jc = (j // 2) if kv_shared else j
        kp = kw[:, jc * PAIR_W:(jc + 1) * PAIR_W]
        vp = vw[:, jc * PAIR_W:(jc + 1) * PAIR_W]
        k_st = jnp.concatenate([jnp.where(lo_kv, kp, zeros_kv), jnp.where(lo_kv, zeros_kv, kp)], axis=0)
        v_st = jnp.concatenate([jnp.where(lo_kv, vp, zeros_kv), jnp.where(lo_kv, zeros_kv, vp)], axis=0)
        s = lax.dot_general(qp, k_st, (((1,), (1,)), ((), ())), preferred_element_type=F32)
        es, ms, dens = [], [], []
        for hh in range(2):
            head = 2 * j + hh
            sh = jnp.where(valid, s[:, hh * kb:(hh + 1) * kb] - slopes[head] * distf, NEG)
            m = jnp.max(sh, axis=-1, keepdims=True)
            if has_sink:
                sk = sink_ref[head]
                m = jnp.maximum(m, sk)
            e = jnp.exp(sh - m)
            den = jnp.sum(e, axis=-1, keepdims=True)
            if has_sink:
                den = den + jnp.exp(sk - m)
            es.append(e.astype(BF16))
            ms.append(m)
            dens.append(den)
        p = jnp.concatenate(es, axis=1)
        o = jnp.dot(p, v_st, preferred_element_type=F32)
        den2 = jnp.where(lo_q, dens[0], dens[1])
        o_ref[:, j * PAIR_W:(j + 1) * PAIR_W] = (o / den2).astype(BF16)
        if want_lse:
            m2 = jnp.where(lo_q, ms[0], ms[1])
            lse_ref[:, j * PAIR_W:(j + 1) * PAIR_W] = m2 + jnp.log(den2)


def _banded_attention(proj, batch, seq, *, dilation, half_w, q_block, k_block, v_block, kv_width,
                      slopes, sink=None, want_lse=True):
    r = dilation
    m_len = seq // r
    qb = min(Q_TILE, m_len)
    kb = min(qb + 2 * half_w, m_len)
    nq = m_len // qb
    proj3 = proj.reshape(batch, m_len, r * PROJ_W)
    kv_per_proj = PROJ_W // kv_width
    kv_shared = kv_width != GROUP_W
    has_sink = sink is not None

    in_specs = []
    args = []
    if has_sink:
        in_specs.append(pl.BlockSpec(memory_space=pltpu.SMEM))
        args.append(sink)
    in_specs += [
        pl.BlockSpec((None, qb, GROUP_W), lambda b, c, i: (b, i, c * PROJ_BLOCKS + q_block)),
        pl.BlockSpec((None, m_len, kv_width), lambda b, c, i: (b, 0, c * kv_per_proj + k_block)),
        pl.BlockSpec((None, m_len, kv_width), lambda b, c, i: (b, 0, c * kv_per_proj + v_block)),
    ]
    args += [proj3, proj3, proj3]
    out_spec = pl.BlockSpec((None, qb, GROUP_W), lambda b, c, i: (b, i, c))
    out_specs = [out_spec]
    out_shape = [jax.ShapeDtypeStruct((batch, m_len, r * GROUP_W), BF16)]
    if want_lse:
        out_specs.append(out_spec)
        out_shape.append(jax.ShapeDtypeStruct((batch, m_len, r * GROUP_W), F32))
    kern = functools.partial(_attn_kernel, qb=qb, kb=kb, m_len=m_len, half_w=half_w,
                             slopes=tuple(float(s) for s in slopes), kv_shared=kv_shared,
                             has_sink=has_sink, want_lse=want_lse)
    outs = pl.pallas_call(
        kern,
        grid=(batch, r, nq),
        in_specs=in_specs,
        out_specs=out_specs,
        out_shape=out_shape,
        compiler_params=pltpu.CompilerParams(dimension_semantics=("arbitrary",) * 3,
                                             vmem_limit_bytes=VMEM_LIMIT),
        name=f"attn_r{r}_w{half_w}",
    )(*args)
    return [o.reshape(batch * seq, GROUP_W) for o in outs]


def _memkv_kernel(m_ref, g_ref, w_ref, o_ref):
    h = _rms(m_ref[...], g_ref[...]).astype(BF16)
    o_ref[...] = jnp.dot(h, w_ref[...], preferred_element_type=F32).astype(BF16)


def _memkv(mem2d, g, w):
    R = mem2d.shape[0]
    return pl.pallas_call(
        _memkv_kernel,
        grid=(R // ROW_TILE,),
        in_specs=[pl.BlockSpec((ROW_TILE, D_MODEL), lambda i: (i, 0)),
                  _const_spec((1, D_MODEL)),
                  _const_spec((D_MODEL, 2 * X_WIDTH))],
        out_specs=pl.BlockSpec((ROW_TILE, 2 * X_WIDTH), lambda i: (i, 0)),
        out_shape=jax.ShapeDtypeStruct((R, 2 * X_WIDTH), BF16),
        compiler_params=pltpu.CompilerParams(dimension_semantics=("arbitrary",)),
        name="memkv",
    )(mem2d, g, w)


def _mix_kernel(x_ref, o0_ref, o1_ref, o2_ref, l0_ref, l1_ref, l2_ref, ob_ref, kv_ref,
                g_mix_ref, w_gate_ref, b_gate_ref, w_br_ref, w_out_ref,
                g_x_ref, w_cq_ref, w_co_ref, g_ffn_ref, w_rt_ref, b_rt_ref,
                x2_ref, route_ref, counts_ref, tri_ref, carry_ref):
    tm = x_ref.shape[0]

    @pl.when(pl.program_id(0) == 0)
    def _():
        row = lax.broadcasted_iota(jnp.int32, (tm, tm), 0)
        col = lax.broadcasted_iota(jnp.int32, (tm, tm), 1)
        tri_ref[...] = jnp.where(row > col, 1.0, 0.0).astype(BF16)
        carry_ref[...] = jnp.zeros_like(carry_ref)

    x = x_ref[...]
    h1 = _rms(x, g_mix_ref[...]).astype(BF16)

    l0, l1, l2 = l0_ref[...], l1_ref[...], l2_ref[...]
    lm = jnp.maximum(jnp.maximum(l0, l1), l2)
    e0, e1, e2 = jnp.exp(l0 - lm), jnp.exp(l1 - lm), jnp.exp(l2 - lm)
    den = e0 + e1 + e2
    oa = (e0 / den) * o0_ref[...].astype(F32) + (e1 / den) * o1_ref[...].astype(F32) \
        + (e2 / den) * o2_ref[...].astype(F32)
    br_a = jnp.dot(oa.astype(BF16), w_br_ref[:GROUP_W, :], preferred_element_type=F32)
    br_b = jnp.dot(ob_ref[...], w_br_ref[GROUP_W:, :], preferred_element_type=F32)
    ga = jax.nn.sigmoid(jnp.dot(h1, w_gate_ref[:, :D_MODEL], preferred_element_type=F32)
                        + b_gate_ref[:, :D_MODEL])
    merged = ga * br_a
    gb = jax.nn.sigmoid(jnp.dot(h1, w_gate_ref[:, D_MODEL:], preferred_element_type=F32)
                        + b_gate_ref[:, D_MODEL:])
    merged = merged + gb * br_b
    x1 = x + jnp.dot(merged.astype(BF16), w_out_ref[...], preferred_element_type=F32)

    h2 = _rms(x1, g_x_ref[...]).astype(BF16)
    q = jnp.dot(h2, w_cq_ref[...], preferred_element_type=F32).astype(BF16)
    heads = []
    for h in range(X_HEADS):
        cols = slice(h * X_HEAD_DIM, (h + 1) * X_HEAD_DIM)
        kh = kv_ref[:, cols]
        vh = kv_ref[:, X_WIDTH + h * X_HEAD_DIM:X_WIDTH + (h + 1) * X_HEAD_DIM]
        s = lax.dot_general(q[:, cols], kh, (((1,), (1,)), ((), ())),
                            preferred_element_type=F32) * (X_HEAD_DIM ** -0.5)
        m = jnp.max(s, axis=-1, keepdims=True)
        e = jnp.exp(s - m)
        p = e / jnp.sum(e, axis=-1, keepdims=True)
        heads.append(jnp.dot(p.astype(BF16), vh, preferred_element_type=F32))
    o = jnp.concatenate(heads, axis=1).astype(BF16)
    x2 = x1 + jnp.dot(o, w_co_ref[...], preferred_element_type=F32)
    x2_ref[...] = x2

    h3 = _rms(x2, g_ffn_ref[...])
    logits = jnp.dot(h3, w_rt_ref[...], preferred_element_type=F32,
                     precision=lax.Precision.HIGHEST) + b_rt_ref[...]
    lane = lax.broadcasted_iota(jnp.int32, (tm, ROUTE_W), 1)
    gmask = lane < N_GROUPS
    gl = jnp.where(gmask, logits, -jnp.inf)
    gmax = jnp.max(gl, axis=-1, keepdims=True)
    gidx = jnp.min(jnp.where(gl == gmax, lane, ROUTE_W), axis=-1, keepdims=True)
    pg_sel = 1.0 / jnp.sum(jnp.where(gmask, jnp.exp(logits - gmax), 0.0), axis=-1, keepdims=True)
    e_lo = EXPERT_LANE0 + gidx * EXPERTS_PER_GROUP
    emask = (lane >= e_lo) & (lane < e_lo + EXPERTS_PER_GROUP)
    el = jnp.where(emask, logits, -jnp.inf)
    emax1 = jnp.max(el, axis=-1, keepdims=True)
    i1 = jnp.min(jnp.where(el == emax1, lane, ROUTE_W), axis=-1, keepdims=True)
    el2 = jnp.where(lane == i1, -jnp.inf, el)
    emax2 = jnp.max(el2, axis=-1, keepdims=True)
    i2 = jnp.min(jnp.where(el2 == emax2, lane, ROUTE_W), axis=-1, keepdims=True)
    t2 = jnp.exp(emax2 - emax1)
    w1 = pg_sel / (1.0 + t2)
    w2 = pg_sel * t2 / (1.0 + t2)

    oh1 = lane == i1
    oh2 = lane == i2
    ohs = jnp.where(oh1 | oh2, 1.0, 0.0)
    before = jnp.dot(tri_ref[...], ohs.astype(BF16), preferred_element_type=F32) + carry_ref[...]
    rank1 = jnp.sum(jnp.where(oh1, before, 0.0), axis=-1, keepdims=True)
    rank2 = jnp.sum(jnp.where(oh2, before, 0.0), axis=-1, keepdims=True)
    carry_ref[...] = carry_ref[...] + jnp.sum(ohs, axis=0, keepdims=True)
    counts_ref[...] = jnp.broadcast_to(carry_ref[...], counts_ref.shape)

    rec = jnp.where(lane == 0, (i1 - EXPERT_LANE0).astype(F32), 0.0)
    rec = jnp.where(lane == 1, (i2 - EXPERT_LANE0).astype(F32), rec)
    rec = jnp.where(lane == 2, w1, rec)
    rec = jnp.where(lane == 3, w2, rec)
    rec = jnp.where(lane == 4, rank1, rec)
    rec = jnp.where(lane == 5, rank2, rec)
    route_ref[...] = rec


def _mix(x2d, o_groups, lse_groups, o_b, kvm, seq, wts):
    T = x2d.shape[0]
    tm = ROW_TILE
    steps_per_seq = seq // tm
    mem_tokens = kvm.shape[0] // (T // seq)
    row = lambda w: pl.BlockSpec((tm, w), lambda i: (i, 0))
    in_specs = ([row(D_MODEL)] + [row(GROUP_W)] * 7
                + [pl.BlockSpec((mem_tokens, 2 * X_WIDTH), lambda i: (i // steps_per_seq, 0))]
                + [_const_spec(w.shape) for w in wts])
    return pl.pallas_call(
        _mix_kernel,
        grid=(T // tm,),
        in_specs=in_specs,
        out_specs=[row(D_MODEL), row(ROUTE_W), pl.BlockSpec((8, ROUTE_W), lambda i: (0, 0))],
        out_shape=[jax.ShapeDtypeStruct((T, D_MODEL), F32),
                   jax.ShapeDtypeStruct((T, ROUTE_W), F32),
                   jax.ShapeDtypeStruct((8, ROUTE_W), F32)],
        scratch_shapes=[pltpu.VMEM((tm, tm), BF16), pltpu.VMEM((1, ROUTE_W), F32)],
        compiler_params=pltpu.CompilerParams(dimension_semantics=("arbitrary",),
                                             vmem_limit_bytes=VMEM_LIMIT),
        name="mix",
    )(x2d, *o_groups, *lse_groups, o_b, kvm, *wts)


def _dispatch_kernel(dest_ref, pad_ref, x2_ref, g_ref, xs_hbm, hbuf, zbuf, sems):
    tm = x2_ref.shape[0]
    n_pad = pad_ref.shape[-1]
    hbuf[...] = _rms(x2_ref[...], g_ref[...])
    zbuf[...] = jnp.zeros_like(zbuf)

    def row_copy(t, k):
        return pltpu.make_async_copy(hbuf.at[pl.ds(t, 1)], xs_hbm.at[pl.ds(dest_ref[0, k, t], 1)],
                                     sems.at[k])

    def pad_copy(t):
        return pltpu.make_async_copy(zbuf.at[pl.ds(0, 1)], xs_hbm.at[pl.ds(pad_ref[0, 0, t], 1)],
                                     sems.at[2])

    def issue(t, c):
        row_copy(t, 0).start()
        row_copy(t, 1).start()
        return c

    lax.fori_loop(0, tm, issue, 0)

    def issue_pad(t, c):
        pad_copy(t).start()
        return c

    lax.fori_loop(0, n_pad, issue_pad, 0)
    for k in range(2):
        pltpu.make_async_copy(hbuf, xs_hbm.at[pl.ds(0, tm)], sems.at[k]).wait()
    pltpu.make_async_copy(hbuf.at[pl.ds(0, n_pad)], xs_hbm.at[pl.ds(0, n_pad)], sems.at[2]).wait()


def _dispatch(x2, g_ffn, dest, pad_rows, n_rows):
    T = x2.shape[0]
    tm = ROW_TILE
    n_steps = T // tm
    n_pad = pad_rows.shape[0] // n_steps
    dest3 = dest.reshape(2, n_steps, tm).transpose(1, 0, 2)
    pad3 = pad_rows.reshape(n_steps, 1, n_pad)
    return pl.pallas_call(
        _dispatch_kernel,
        grid=(n_steps,),
        in_specs=[pl.BlockSpec((1, 2, tm), lambda i: (i, 0, 0), memory_space=pltpu.SMEM),
                  pl.BlockSpec((1, 1, n_pad), lambda i: (i, 0, 0), memory_space=pltpu.SMEM),
                  pl.BlockSpec((tm, D_MODEL), lambda i: (i, 0)),
                  _const_spec((1, D_MODEL))],
        out_specs=pl.BlockSpec(memory_space=pl.ANY),
        out_shape=jax.ShapeDtypeStruct((n_rows, D_MODEL), F32),
        scratch_shapes=[pltpu.VMEM((tm, D_MODEL), F32), pltpu.VMEM((8, D_MODEL), F32),
                        pltpu.SemaphoreType.DMA((3,))],
        compiler_params=pltpu.CompilerParams(dimension_semantics=("arbitrary",),
                                             has_side_effects=True),
        name="dispatch",
    )(dest3, pad3, x2, g_ffn)


def _expert_kernel(be_ref, xs_ref, wgu_ref, wdn_ref, ys_ref):
    xb = xs_ref[...].astype(BF16)
    gu = jnp.dot(xb, wgu_ref[...], preferred_element_type=F32)
    gate, up = gu[:, :EXPERT_FF], gu[:, EXPERT_FF:]
    act = (gate * jax.nn.sigmoid(gate) * up).astype(BF16)
    ys_ref[...] = jnp.dot(act, wdn_ref[...], preferred_element_type=F32)


def _experts(xs, block_e, w_gu, w_down):
    P = xs.shape[0]
    n_blocks = P // MOE_BLOCK
    grid_spec = pltpu.PrefetchScalarGridSpec(
        num_scalar_prefetch=1,
        grid=(n_blocks,),
        in_specs=[pl.BlockSpec((MOE_BLOCK, D_MODEL), lambda b, be: (b, 0)),
                  pl.BlockSpec((None, D_MODEL, 2 * EXPERT_FF), lambda b, be: (be[b], 0, 0)),
                  pl.BlockSpec((None, EXPERT_FF, D_MODEL), lambda b, be: (be[b], 0, 0))],
        out_specs=pl.BlockSpec((MOE_BLOCK, D_MODEL), lambda b, be: (b, 0)),
    )
    return pl.pallas_call(
        _expert_kernel,
        grid_spec=grid_spec,
        out_shape=jax.ShapeDtypeStruct((P, D_MODEL), F32),
        compiler_params=pltpu.CompilerParams(dimension_semantics=("arbitrary",),
                                             vmem_limit_bytes=VMEM_LIMIT),
        name="experts",
    )(block_e, xs, w_gu, w_down)


def _combine_kernel(dest_ref, x2_ref, route_ref, g_ref, ys_hbm, out_ref, buf, sems):
    tm = x2_ref.shape[0]

    def row_copy(t, k):
        return pltpu.make_async_copy(ys_hbm.at[pl.ds(dest_ref[0, k, t], 1)],
                                     buf.at[k, pl.ds(t, 1)], sems.at[k])

    def issue(t, c):
        row_copy(t, 0).start()
        row_copy(t, 1).start()
        return c

    lax.fori_loop(0, tm, issue, 0)
    for k in range(2):
        pltpu.make_async_copy(ys_hbm.at[pl.ds(0, tm)], buf.at[k], sems.at[k]).wait()
    route = route_ref[...]
    moe = buf[0] * route[:, 2:3] + buf[1] * route[:, 3:4]
    out_ref[...] = _rms(x2_ref[...] + moe, g_ref[...])


def _combine(x2, route, g_final, dest, ys):
    T = x2.shape[0]
    tm = ROW_TILE
    n_steps = T // tm
    dest3 = dest.reshape(2, n_steps, tm).transpose(1, 0, 2)
    return pl.pallas_call(
        _combine_kernel,
        grid=(n_steps,),
        in_specs=[pl.BlockSpec((1, 2, tm), lambda i: (i, 0, 0), memory_space=pltpu.SMEM),
                  pl.BlockSpec((tm, D_MODEL), lambda i: (i, 0)),
                  pl.BlockSpec((tm, ROUTE_W), lambda i: (i, 0)),
                  _const_spec((1, D_MODEL)),
                  pl.BlockSpec(memory_space=pl.ANY)],
        out_specs=pl.BlockSpec((tm, D_MODEL), lambda i: (i, 0)),
        out_shape=jax.ShapeDtypeStruct((T, D_MODEL), F32),
        scratch_shapes=[pltpu.VMEM((2, tm, D_MODEL), F32), pltpu.SemaphoreType.DMA((2,))],
        compiler_params=pltpu.CompilerParams(dimension_semantics=("arbitrary",)),
        name="combine",
    )(dest3, x2, route, g_final, ys)


def _routing_tables(route, counts_rec, n_tokens):
    n_slots = 2 * n_tokens
    n_blocks = n_slots // MOE_BLOCK + N_EXPERTS
    n_rows = n_blocks * MOE_BLOCK
    n_pad = n_rows - n_slots
    counts = counts_rec[0, EXPERT_LANE0:EXPERT_LANE0 + N_EXPERTS].astype(jnp.int32)
    padded = (counts + MOE_BLOCK - 1) // MOE_BLOCK * MOE_BLOCK
    pends = jnp.cumsum(padded)
    pstarts = pends - padded
    expert = route[:, 0:2].astype(jnp.int32)
    rank = route[:, 4:6].astype(jnp.int32)
    dest = (pstarts[expert] + rank).T
    block_e = jnp.minimum(jnp.searchsorted(pends, jnp.arange(n_blocks, dtype=jnp.int32) * MOE_BLOCK,
                                           side="right"), N_EXPERTS - 1).astype(jnp.int32)
    npad_e = padded - counts
    cum = jnp.cumsum(npad_e)
    idx = jnp.arange(n_pad, dtype=jnp.int32)
    e_of = jnp.searchsorted(cum, idx, side="right").astype(jnp.int32)
    e_c = jnp.minimum(e_of, N_EXPERTS - 1)
    in_expert = pstarts[e_c] + counts[e_c] + (idx - (cum[e_c] - npad_e[e_c]))
    pad_rows = jnp.where(e_of < N_EXPERTS, in_expert, pends[-1] + (idx - cum[-1])).astype(jnp.int32)
    return dest.astype(jnp.int32), block_e, pad_rows, n_rows


def _encoder_group(x, mem, w):
    batch, seq, _ = x.shape
    T = batch * seq
    x2d = x.reshape(T, D_MODEL)
    proj = _proj(x2d, w["g_mix"], w["w_in"])

    slopes_a = _alibi_slopes(A_HEADS).reshape(A_GROUPS, A_HEADS_PER_GROUP)
    o_groups, lse_groups = [], []
    for g, (window, r) in enumerate(A_CONFIGS):
        o, lse = _banded_attention(proj, batch, seq, dilation=r, half_w=window // (2 * r),
                                   q_block=g, k_block=A_GROUPS + g, v_block=2 * A_GROUPS + g,
                                   kv_width=GROUP_W, slopes=slopes_a[g] * np.float32(r))
        o_groups.append(o)
        lse_groups.append(lse)
    kv2_block0 = (3 * A_GROUPS + 1) * (GROUP_W // KV2_W)
    (o_b,) = _banded_attention(proj, batch, seq, dilation=1, half_w=B_HALF_WINDOW,
                               q_block=3 * A_GROUPS, k_block=kv2_block0, v_block=kv2_block0 + 1,
                               kv_width=KV2_W, slopes=_alibi_slopes(B_HEADS), sink=w["sink_b"],
                               want_lse=False)

    kvm = _memkv(mem.reshape(-1, D_MODEL), w["g_mem"], w["w_ckv"])
    mix_w = [w[k] for k in ("g_mix", "w_gate", "b_gate", "w_branch", "w_out", "g_xattn", "w_cq",
                            "w_co", "g_ffn", "w_route", "b_route")]
    x2, route, counts_rec = _mix(x2d, o_groups, lse_groups, o_b, kvm, seq, mix_w)

    dest, block_e, pad_rows, n_rows = _routing_tables(route, counts_rec, T)
    xs = _dispatch(x2, w["g_ffn"], dest, pad_rows, n_rows)
    ys = _experts(xs, block_e, w["w_gu"], w["w_down"])
    y = _combine(x2, route, w["g_final"], dest, ys)
    return y.reshape(batch, seq, D_MODEL)


def _prep_weights(g_mix, w_in, sink_b, w_gate, b_gate, w_branch, w_out, g_xattn, g_mem, w_cq, w_ckv,
                  w_co, g_ffn, w_rg, b_rg, w_re, b_re, w_gu, w_down, g_final):
    scale = HEAD_DIM ** -0.5
    aw = A_WIDTH
    qa, ka, va = w_in[:, :aw] * scale, w_in[:, aw:2 * aw], w_in[:, 2 * aw:3 * aw]
    qb = w_in[:, 3 * aw:3 * aw + B_Q] * scale
    kb = w_in[:, 3 * aw + B_Q:3 * aw + B_Q + B_KV]
    vb = w_in[:, 3 * aw + B_Q + B_KV:]
    twice = lambda t: jnp.repeat(t.reshape(D_MODEL, B_KV_HEADS, 1, HEAD_DIM), 2, axis=2).reshape(D_MODEL, KV2_W)
    w_in_x = jnp.concatenate([qa, ka, va, qb, twice(kb), twice(vb)], axis=1).astype(BF16)
    w_route = jnp.zeros((D_MODEL, ROUTE_W), F32)
    w_route = w_route.at[:, :N_GROUPS].set(w_rg).at[:, EXPERT_LANE0:EXPERT_LANE0 + N_EXPERTS].set(w_re)
    b_route = jnp.zeros((1, ROUTE_W), F32)
    b_route = b_route.at[0, :N_GROUPS].set(b_rg).at[0, EXPERT_LANE0:EXPERT_LANE0 + N_EXPERTS].set(b_re)
    vec = lambda v: v.reshape(1, -1).astype(F32)
    return dict(
        g_mix=vec(g_mix), w_in=w_in_x, sink_b=sink_b.astype(F32),
        w_gate=w_gate.astype(BF16), b_gate=vec(b_gate), w_branch=w_branch.astype(BF16),
        w_out=w_out.astype(BF16), g_xattn=vec(g_xattn), g_mem=vec(g_mem), w_cq=w_cq.astype(BF16),
        w_ckv=w_ckv.astype(BF16), w_co=w_co.astype(BF16), g_ffn=vec(g_ffn),
        w_route=w_route, b_route=b_route, w_gu=w_gu.astype(BF16), w_down=w_down.astype(BF16),
        g_final=vec(g_final))


def kernel(x_prompt, x_sample, mem_prompt, mem_sample, g_mix, w_in, sink_b, w_gate, b_gate, w_branch,
           w_out, g_xattn, g_mem, w_cq, w_ckv, w_co, g_ffn, w_rg, b_rg, w_re, b_re, w_gu, w_down,
           g_final):
    assert g_mix.shape[0] == 1, "single-layer encoder"
    w = _prep_weights(g_mix[0], w_in[0], sink_b[0], w_gate[0], b_gate[0], w_branch[0], w_out[0],
                      g_xattn[0], g_mem[0], w_cq[0], w_ckv[0], w_co[0], g_ffn[0], w_rg[0], b_rg[0],
                      w_re[0], b_re[0], w_gu[0], w_down[0], g_final)
    return (_encoder_group(x_prompt, mem_prompt, w), _encoder_group(x_sample, mem_sample, w))
```

```python
import functools

import numpy as np
import jax
import jax.numpy as jnp
from jax import lax
from jax.experimental import pallas as pl
from jax.experimental.pallas import tpu as pltpu

F32 = jnp.float32
BF16 = jnp.bfloat16

D_MODEL = 1024
HEAD_DIM = 64
A_CONFIGS = ((128, 1), (512, 4), (2048, 16))
A_GROUPS = 3
A_HEADS_PER_GROUP = 8
A_HEADS = A_GROUPS * A_HEADS_PER_GROUP
A_WIDTH = A_HEADS * HEAD_DIM
B_HEADS = 8
B_KV_HEADS = 2
B_HALF_WINDOW = 128
B_Q = B_HEADS * HEAD_DIM
B_KV = B_KV_HEADS * HEAD_DIM
X_HEADS = 4
X_HEAD_DIM = 128
X_WIDTH = X_HEADS * X_HEAD_DIM
N_GROUPS = 4
EXPERTS_PER_GROUP = 8
N_EXPERTS = N_GROUPS * EXPERTS_PER_GROUP
EXPERT_FF = 512
MOE_BLOCK = 256
EPS = 1e-6
NEG = -1e30

LANES = 128
GROUP_W = A_HEADS_PER_GROUP * HEAD_DIM
PAIR_W = 2 * HEAD_DIM
N_PAIRS = GROUP_W // PAIR_W
QKV_W = 3 * GROUP_W
KV2_W = 2 * B_KV
WIN_W = B_Q + 2 * KV2_W
PROJ_W = A_GROUPS * QKV_W + WIN_W
ROUTE_W = LANES
EXPERT_LANE0 = 32
VMEM_LIMIT = 52 * 1024 * 1024

ROW_TILE = 256
PROJ_TILE = 512
Q_TILE = 128


def _rms(x, g):
    ms = jnp.mean(x * x, axis=-1, keepdims=True)
    return x * lax.rsqrt(ms + EPS) * g


def _alibi_slopes(n):
    return 2.0 ** (-8.0 * np.arange(1, n + 1, dtype=np.float32) / n)


def _const_spec(shape):
    return pl.BlockSpec(shape, lambda *_: (0,) * len(shape), pipeline_mode=pl.Buffered(1))


def _proj_kernel(x_ref, g_ref, w_ref, *refs):
    o_refs, win_ref, h_ref = refs[:A_GROUPS], refs[A_GROUPS], refs[A_GROUPS + 1]
    tm = x_ref.shape[0]
    h32 = _rms(x_ref[...], g_ref[...])
    h_nat = h32.astype(BF16)
    n_slabs = h_ref.shape[0]
    for s in range(n_slabs):
        h_ref[s] = h32[:, s * LANES:(s + 1) * LANES]
    for g, (_, r) in enumerate(A_CONFIGS):
        n = tm // r
        if r == 1:
            h = h_nat
        else:
            h = jnp.concatenate(
                [jnp.concatenate([h_ref[s, pl.ds(c, n, stride=r), :] for c in range(r)], axis=0)
                 for s in range(n_slabs)], axis=1).astype(BF16)
        for j in range(3):
            cols = slice(g * QKV_W + j * GROUP_W, g * QKV_W + (j + 1) * GROUP_W)
            res = jnp.dot(h, w_ref[:, cols], preferred_element_type=F32).astype(BF16)
            for c in range(r):
                o_refs[g][c, :, j * GROUP_W:(j + 1) * GROUP_W] = res[c * n:(c + 1) * n]
    for j in range(WIN_W // GROUP_W):
        cols = slice(A_GROUPS * QKV_W + j * GROUP_W, A_GROUPS * QKV_W + (j + 1) * GROUP_W)
        win_ref[:, j * GROUP_W:(j + 1) * GROUP_W] = jnp.dot(
            h_nat, w_ref[:, cols], preferred_element_type=F32).astype(BF16)


def _proj(x, g, w):
    batch, seq, _ = x.shape
    tm = PROJ_TILE
    out_specs = [pl.BlockSpec((None, r, tm // r, QKV_W), lambda b, i: (b, 0, i, 0)) for _, r in A_CONFIGS]
    out_shape = [jax.ShapeDtypeStruct((batch, r, seq // r, QKV_W), BF16) for _, r in A_CONFIGS]
    out_specs.append(pl.BlockSpec((None, tm, WIN_W), lambda b, i: (b, i, 0)))
    out_shape.append(jax.ShapeDtypeStruct((batch, seq, WIN_W), BF16))
    return pl.pallas_call(
        _proj_kernel,
        grid=(batch, seq // tm),
        in_specs=[pl.BlockSpec((None, tm, D_MODEL), lambda b, i: (b, i, 0)),
                  _const_spec((1, D_MODEL)),
                  _const_spec((D_MODEL, PROJ_W))],
        out_specs=out_specs,
        out_shape=out_shape,
        scratch_shapes=[pltpu.VMEM((D_MODEL // LANES, tm, LANES), F32)],
        compiler_params=pltpu.CompilerParams(dimension_semantics=("arbitrary",) * 2,
                                             vmem_limit_bytes=VMEM_LIMIT),
        name="proj",
    )(x, g, w)


def _attn_kernel(*refs, qb, kb, m_len, half_w, slopes, kv_shared, has_sink, want_lse):
    refs = list(refs)
    sink_ref = refs.pop(0) if has_sink else None
    q_ref, k_ref, v_ref, o_ref = refs[:4]
    lse_ref = refs[4] if want_lse else None

    i = pl.program_id(2)
    ks = jnp.clip(i * qb - half_w, 0, m_len - kb)
    ks = pl.multiple_of(ks, 16)
    kw = k_ref[pl.ds(ks, kb), :]
    vw = v_ref[pl.ds(ks, kb), :]
    qpos = i * qb + lax.broadcasted_iota(jnp.int32, (qb, kb), 0)
    kpos = ks + lax.broadcasted_iota(jnp.int32, (qb, kb), 1)
    dist = jnp.abs(qpos - kpos)
    valid = dist <= half_w
    distf = dist.astype(F32)
    lo_kv = lax.broadcasted_iota(jnp.int32, (kb, PAIR_W), 1) < HEAD_DIM
    lo_q = lax.broadcasted_iota(jnp.int32, (qb, PAIR_W), 1) < HEAD_DIM
    zeros_kv = jnp.zeros((kb, PAIR_W), BF16)

    for j in range(N_PAIRS):
        qp = q_ref[:, j * PAIR_W:(j + 1) * PAIR_W]
        jc = (j // 2) if kv_shared else j
        kp = kw[:, jc * PAIR_W:(jc + 1) * PAIR_W]
        vp = vw[:, jc * PAIR_W:(jc + 1) * PAIR_W]
        k_st = jnp.concatenate([jnp.where(lo_kv, kp, zeros_kv), jnp.where(lo_kv, zeros_kv, kp)], axis=0)
        v_st = jnp.concatenate([jnp.where(lo_kv, vp, zeros_kv), jnp.where(lo_kv, zeros_kv, vp)], axis=0)
        s = lax.dot_general(qp, k_st, (((1,), (1,)), ((), ())), preferred_element_type=F32)
        es, ms, dens = [], [], []
        for hh in range(2):
            head = 2 * j + hh
            sh = jnp.where(valid, s[:, hh * kb:(hh + 1) * kb] - slopes[head] * distf, NEG)
            m = jnp.max(sh, axis=-1, keepdims=True)
            if has_sink:
                sk = sink_ref[head]
                m = jnp.maximum(m, sk)
            e = jnp.exp(sh - m)
            den = jnp.sum(e, axis=-1, keepdims=True)
            if has_sink:
                den = den + jnp.exp(sk - m)
            es.append(e.astype(BF16))
            ms.append(m)
            dens.append(den)
        p = jnp.concatenate(es, axis=1)
        o = jnp.dot(p, v_st, preferred_element_type=F32)
        den2 = jnp.where(lo_q, dens[0], dens[1])
        o_ref[:, j * PAIR_W:(j + 1) * PAIR_W] = (o / den2).astype(BF16)
        if want_lse:
            m2 = jnp.where(lo_q, ms[0], ms[1])
            lse_ref[:, j * PAIR_W:(j + 1) * PAIR_W] = m2 + jnp.log(den2)


def _banded_attention(qkv, *, half_w, kv_width, slopes, sink=None, want_lse=True):
    batch, r, m_len, _ = qkv.shape
    qb = min(Q_TILE, m_len)
    kb = min(qb + 2 * half_w, m_len)
    nq = m_len // qb
    kv_shared = kv_width != GROUP_W
    has_sink = sink is not None
    k_block = GROUP_W // kv_width

    in_specs = []
    args = []
    if has_sink:
        in_specs.append(pl.BlockSpec(memory_space=pltpu.SMEM))
        args.append(sink)
    in_specs += [
        pl.BlockSpec((None, None, qb, GROUP_W), lambda b, c, i: (b, c, i, 0)),
        pl.BlockSpec((None, None, m_len, kv_width), lambda b, c, i: (b, c, 0, k_block)),
        pl.BlockSpec((None, None, m_len, kv_width), lambda b, c, i: (b, c, 0, k_block + 1)),
    ]
    args += [qkv, qkv, qkv]
    out_spec = pl.BlockSpec((None, None, qb, GROUP_W), lambda b, c, i: (b, c, i, 0))
    out_specs = [out_spec]
    out_shape = [jax.ShapeDtypeStruct((batch, r, m_len, GROUP_W), BF16)]
    if want_lse:
        out_specs.append(out_spec)
        out_shape.append(jax.ShapeDtypeStruct((batch, r, m_len, GROUP_W), F32))
    kern = functools.partial(_attn_kernel, qb=qb, kb=kb, m_len=m_len, half_w=half_w,
                             slopes=tuple(float(s) for s in slopes), kv_shared=kv_shared,
                             has_sink=has_sink, want_lse=want_lse)
    return pl.pallas_call(
        kern,
        grid=(batch, r, nq),
        in_specs=in_specs,
        out_specs=out_specs,
        out_shape=out_shape,
        compiler_params=pltpu.CompilerParams(dimension_semantics=("arbitrary",) * 3,
                                             vmem_limit_bytes=VMEM_LIMIT),
        name=f"attn_r{r}_w{half_w}",
    )(*args)


def _memkv_kernel(m_ref, g_ref, w_ref, o_ref):
    h = _rms(m_ref[...], g_ref[...]).astype(BF16)
    o_ref[...] = jnp.dot(h, w_ref[...], preferred_element_type=F32).astype(BF16)


def _memkv(mem2d, g, w):
    R = mem2d.shape[0]
    return pl.pallas_call(
        _memkv_kernel,
        grid=(R // ROW_TILE,),
        in_specs=[pl.BlockSpec((ROW_TILE, D_MODEL), lambda i: (i, 0)),
                  _const_spec((1, D_MODEL)),
                  _const_spec((D_MODEL, 2 * X_WIDTH))],
        out_specs=pl.BlockSpec((ROW_TILE, 2 * X_WIDTH), lambda i: (i, 0)),
        out_shape=jax.ShapeDtypeStruct((R, 2 * X_WIDTH), BF16),
        compiler_params=pltpu.CompilerParams(dimension_semantics=("arbitrary",)),
        name="memkv",
    )(mem2d, g, w)


def _token_order(src_ref, dst_ref):
    r, n, _ = src_ref.shape
    if r == 1:
        return src_ref[0].astype(F32)
    n_slabs = dst_ref.shape[0]
    for c in range(r):
        rows = src_ref[c].astype(F32)
        for s in range(n_slabs):
            dst_ref[s, pl.ds(c, n, stride=r), :] = rows[:, s * LANES:(s + 1) * LANES]
    return jnp.concatenate([dst_ref[s] for s in range(n_slabs)], axis=1)


def _mix_kernel(x_ref, o0_ref, o1_ref, o2_ref, l0_ref, l1_ref, l2_ref, ob_ref, kv_ref,
                g_mix_ref, w_gate_ref, b_gate_ref, w_br_ref, w_out_ref,
                g_x_ref, w_cq_ref, w_co_ref, g_ffn_ref, w_rt_ref, b_rt_ref,
                x2_ref, route_ref, counts_ref, tri_ref, carry_ref, *order_refs):
    tm = x_ref.shape[0]

    @pl.when((pl.program_id(0) == 0) & (pl.program_id(1) == 0))
    def _():
        row = lax.broadcasted_iota(jnp.int32, (tm, tm), 0)
        col = lax.broadcasted_iota(jnp.int32, (tm, tm), 1)
        tri_ref[...] = jnp.where(row > col, 1.0, 0.0).astype(BF16)
        carry_ref[...] = jnp.zeros_like(carry_ref)

    x = x_ref[...]
    h1 = _rms(x, g_mix_ref[...]).astype(BF16)

    l0, l1, l2 = (_token_order(l, s) for l, s in zip((l0_ref, l1_ref, l2_ref), order_refs[:3]))
    lm = jnp.maximum(jnp.maximum(l0, l1), l2)
    e0, e1, e2 = jnp.exp(l0 - lm), jnp.exp(l1 - lm), jnp.exp(l2 - lm)
    den = e0 + e1 + e2
    o0, o1, o2 = (_token_order(o, s) for o, s in zip((o0_ref, o1_ref, o2_ref), order_refs[3:]))
    oa = (e0 / den) * o0 + (e1 / den) * o1 + (e2 / den) * o2
    br_a = jnp.dot(oa.astype(BF16), w_br_ref[:GROUP_W, :], preferred_element_type=F32)
    br_b = jnp.dot(ob_ref[...], w_br_ref[GROUP_W:, :], preferred_element_type=F32)
    ga = jax.nn.sigmoid(jnp.dot(h1, w_gate_ref[:, :D_MODEL], preferred_element_type=F32)
                        + b_gate_ref[:, :D_MODEL])
    merged = ga * br_a
    gb = jax.nn.sigmoid(jnp.dot(h1, w_gate_ref[:, D_MODEL:], preferred_element_type=F32)
                        + b_gate_ref[:, D_MODEL:])
    merged = merged + gb * br_b
    x1 = x + jnp.dot(merged.astype(BF16), w_out_ref[...], preferred_element_type=F32)

    h2 = _rms(x1, g_x_ref[...]).astype(BF16)
    q = jnp.dot(h2, w_cq_ref[...], preferred_element_type=F32).astype(BF16)
    heads = []
    for h in range(X_HEADS):
        cols = slice(h * X_HEAD_DIM, (h + 1) * X_HEAD_DIM)
        kh = kv_ref[:, cols]
        vh = kv_ref[:, X_WIDTH + h * X_HEAD_DIM:X_WIDTH + (h + 1) * X_HEAD_DIM]
        s = lax.dot_general(q[:, cols], kh, (((1,), (1,)), ((), ())),
                            preferred_element_type=F32) * (X_HEAD_DIM ** -0.5)
        m = jnp.max(s, axis=-1, keepdims=True)
        e = jnp.exp(s - m)
        p = e / jnp.sum(e, axis=-1, keepdims=True)
        heads.append(jnp.dot(p.astype(BF16), vh, preferred_element_type=F32))
    o = jnp.concatenate(heads, axis=1).astype(BF16)
    x2 = x1 + jnp.dot(o, w_co_ref[...], preferred_element_type=F32)
    x2_ref[...] = x2

    h3 = _rms(x2, g_ffn_ref[...])
    logits = jnp.dot(h3, w_rt_ref[...], preferred_element_type=F32,
                     precision=lax.Precision.HIGHEST) + b_rt_ref[...]
    lane = lax.broadcasted_iota(jnp.int32, (tm, ROUTE_W), 1)
    gmask = lane < N_GROUPS
    gl = jnp.where(gmask, logits, -jnp.inf)
    gmax = jnp.max(gl, axis=-1, keepdims=True)
    gidx = jnp.min(jnp.where(gl == gmax, lane, ROUTE_W), axis=-1, keepdims=True)
    pg_sel = 1.0 / jnp.sum(jnp.where(gmask, jnp.exp(logits - gmax), 0.0), axis=-1, keepdims=True)
    e_lo = EXPERT_LANE0 + gidx * EXPERTS_PER_GROUP
    emask = (lane >= e_lo) & (lane < e_lo + EXPERTS_PER_GROUP)
    el = jnp.where(emask, logits, -jnp.inf)
    emax1 = jnp.max(el, axis=-1, keepdims=True)
    i1 = jnp.min(jnp.where(el == emax1, lane, ROUTE_W), axis=-1, keepdims=True)
    el2 = jnp.where(lane == i1, -jnp.inf, el)
    emax2 = jnp.max(el2, axis=-1, keepdims=True)
    i2 = jnp.min(jnp.where(el2 == emax2, lane, ROUTE_W), axis=-1, keepdims=True)
    t2 = jnp.exp(emax2 - emax1)
    w1 = pg_sel / (1.0 + t2)
    w2 = pg_sel * t2 / (1.0 + t2)

    oh1 = lane == i1
    oh2 = lane == i2
    ohs = jnp.where(oh1 | oh2, 1.0, 0.0)
    before = jnp.dot(tri_ref[...], ohs.astype(BF16), preferred_element_type=F32) + carry_ref[...]
    rank1 = jnp.sum(jnp.where(oh1, before, 0.0), axis=-1, keepdims=True)
    rank2 = jnp.sum(jnp.where(oh2, before, 0.0), axis=-1, keepdims=True)
    carry_ref[...] = carry_ref[...] + jnp.sum(ohs, axis=0, keepdims=True)
    counts_ref[...] = jnp.broadcast_to(carry_ref[...], counts_ref.shape)

    rec = jnp.where(lane == 0, (i1 - EXPERT_LANE0).astype(F32), 0.0)
    rec = jnp.where(lane == 1, (i2 - EXPERT_LANE0).astype(F32), rec)
    rec = jnp.where(lane == 2, w1, rec)
    rec = jnp.where(lane == 3, w2, rec)
    rec = jnp.where(lane == 4, rank1, rec)
    rec = jnp.where(lane == 5, rank2, rec)
    route_ref[...] = rec


def _mix(x, o_groups, lse_groups, o_b, kvm, wts):
    batch, seq, _ = x.shape
    T = batch * seq
    tm = ROW_TILE
    steps = seq // tm
    mem_tokens = kvm.shape[0] // batch
    row = lambda w: pl.BlockSpec((tm, w), lambda b, i: (b * steps + i, 0))
    by_class = [pl.BlockSpec((None, r, tm // r, GROUP_W), lambda b, i: (b, 0, i, 0)) for _, r in A_CONFIGS]
    in_specs = ([pl.BlockSpec((None, tm, D_MODEL), lambda b, i: (b, i, 0))] + by_class + by_class
                + [pl.BlockSpec((None, None, tm, GROUP_W), lambda b, i: (b, 0, i, 0)),
                   pl.BlockSpec((mem_tokens, 2 * X_WIDTH), lambda b, i: (b, 0))]
                + [_const_spec(w.shape) for w in wts])
    return pl.pallas_call(
        _mix_kernel,
        grid=(batch, steps),
        in_specs=in_specs,
        out_specs=[row(D_MODEL), row(ROUTE_W), pl.BlockSpec((8, ROUTE_W), lambda b, i: (0, 0))],
        out_shape=[jax.ShapeDtypeStruct((T, D_MODEL), F32),
                   jax.ShapeDtypeStruct((T, ROUTE_W), F32),
                   jax.ShapeDtypeStruct((8, ROUTE_W), F32)],
        scratch_shapes=[pltpu.VMEM((tm, tm), BF16), pltpu.VMEM((1, ROUTE_W), F32)]
        + [pltpu.VMEM((GROUP_W // LANES, tm, LANES), F32)] * (2 * A_GROUPS),
        compiler_params=pltpu.CompilerParams(dimension_semantics=("arbitrary",) * 2,
                                             vmem_limit_bytes=VMEM_LIMIT),
        name="mix",
    )(x, *o_groups, *lse_groups, o_b, kvm, *wts)


def _dispatch_kernel(dest_ref, pad_ref, x2_ref, g_ref, xs_hbm, hbuf, zbuf, sems):
    tm = x2_ref.shape[0]
    n_pad = pad_ref.shape[-1]
    hbuf[...] = _rms(x2_ref[...], g_ref[...])
    zbuf[...] = jnp.zeros_like(zbuf)

    def row_copy(t, k):
        return pltpu.make_async_copy(hbuf.at[pl.ds(t, 1)], xs_hbm.at[pl.ds(dest_ref[0, k, t], 1)],
                                     sems.at[k])

    def pad_copy(t):
        return pltpu.make_async_copy(zbuf.at[pl.ds(0, 1)], xs_hbm.at[pl.ds(pad_ref[0, 0, t], 1)],
                                     sems.at[2])

    def issue(t, c):
        row_copy(t, 0).start()
        row_copy(t, 1).start()
        return c

    lax.fori_loop(0, tm, issue, 0)

    def issue_pad(t, c):
        pad_copy(t).start()
        return c

    lax.fori_loop(0, n_pad, issue_pad, 0)
    for k in range(2):
        pltpu.make_async_copy(hbuf, xs_hbm.at[pl.ds(0, tm)], sems.at[k]).wait()
    pltpu.make_async_copy(hbuf.at[pl.ds(0, n_pad)], xs_hbm.at[pl.ds(0, n_pad)], sems.at[2]).wait()


def _dispatch(x2, g_ffn, dest, pad_rows, n_rows):
    T = x2.shape[0]
    tm = ROW_TILE
    n_steps = T // tm
    n_pad = pad_rows.shape[0] // n_steps
    dest3 = dest.reshape(2, n_steps, tm).transpose(1, 0, 2)
    pad3 = pad_rows.reshape(n_steps, 1, n_pad)
    return pl.pallas_call(
        _dispatch_kernel,
        grid=(n_steps,),
        in_specs=[pl.BlockSpec((1, 2, tm), lambda i: (i, 0, 0), memory_space=pltpu.SMEM),
                  pl.BlockSpec((1, 1, n_pad), lambda i: (i, 0, 0), memory_space=pltpu.SMEM),
                  pl.BlockSpec((tm, D_MODEL), lambda i: (i, 0)),
                  _const_spec((1, D_MODEL))],
        out_specs=pl.BlockSpec(memory_space=pl.ANY),
        out_shape=jax.ShapeDtypeStruct((n_rows, D_MODEL), F32),
        scratch_shapes=[pltpu.VMEM((tm, D_MODEL), F32), pltpu.VMEM((8, D_MODEL), F32),
                        pltpu.SemaphoreType.DMA((3,))],
        compiler_params=pltpu.CompilerParams(dimension_semantics=("arbitrary",),
                                             has_side_effects=True),
        name="dispatch",
    )(dest3, pad3, x2, g_ffn)


def _expert_kernel(be_ref, xs_ref, wgu_ref, wdn_ref, ys_ref):
    xb = xs_ref[...].astype(BF16)
    gu = jnp.dot(xb, wgu_ref[...], preferred_element_type=F32)
    gate, up = gu[:, :EXPERT_FF], gu[:, EXPERT_FF:]
    act = (gate * jax.nn.sigmoid(gate) * up).astype(BF16)
    ys_ref[...] = jnp.dot(act, wdn_ref[...], preferred_element_type=F32)


def _experts(xs, block_e, w_gu, w_down):
    P = xs.shape[0]
    n_blocks = P // MOE_BLOCK
    grid_spec = pltpu.PrefetchScalarGridSpec(
        num_scalar_prefetch=1,
        grid=(n_blocks,),
        in_specs=[pl.BlockSpec((MOE_BLOCK, D_MODEL), lambda b, be: (b, 0)),
                  pl.BlockSpec((None, D_MODEL, 2 * EXPERT_FF), lambda b, be: (be[b], 0, 0)),
                  pl.BlockSpec((None, EXPERT_FF, D_MODEL), lambda b, be: (be[b], 0, 0))],
        out_specs=pl.BlockSpec((MOE_BLOCK, D_MODEL), lambda b, be: (b, 0)),
    )
    return pl.pallas_call(
        _expert_kernel,
        grid_spec=grid_spec,
        out_shape=jax.ShapeDtypeStruct((P, D_MODEL), F32),
        compiler_params=pltpu.CompilerParams(dimension_semantics=("arbitrary",),
                                             vmem_limit_bytes=VMEM_LIMIT),
        name="experts",
    )(block_e, xs, w_gu, w_down)


def _combine_kernel(dest_ref, x2_ref, route_ref, g_ref, ys_hbm, out_ref, buf, sems):
    tm = x2_ref.shape[0]

    def row_copy(t, k):
        return pltpu.make_async_copy(ys_hbm.at[pl.ds(dest_ref[0, k, t], 1)],
                                     buf.at[k, pl.ds(t, 1)], sems.at[k])

    def issue(t, c):
        row_copy(t, 0).start()
        row_copy(t, 1).start()
        return c

    lax.fori_loop(0, tm, issue, 0)
    for k in range(2):
        pltpu.make_async_copy(ys_hbm.at[pl.ds(0, tm)], buf.at[k], sems.at[k]).wait()
    route = route_ref[...]
    moe = buf[0] * route[:, 2:3] + buf[1] * route[:, 3:4]
    out_ref[...] = _rms(x2_ref[...] + moe, g_ref[...])


def _combine(x2, route, g_final, dest, ys):
    T = x2.shape[0]
    tm = ROW_TILE
    n_steps = T // tm
    dest3 = dest.reshape(2, n_steps, tm).transpose(1, 0, 2)
    return pl.pallas_call(
        _combine_kernel,
        grid=(n_steps,),
        in_specs=[pl.BlockSpec((1, 2, tm), lambda i: (i, 0, 0), memory_space=pltpu.SMEM),
                  pl.BlockSpec((tm, D_MODEL), lambda i: (i, 0)),
                  pl.BlockSpec((tm, ROUTE_W), lambda i: (i, 0)),
                  _const_spec((1, D_MODEL)),
                  pl.BlockSpec(memory_space=pl.ANY)],
        out_specs=pl.BlockSpec((tm, D_MODEL), lambda i: (i, 0)),
        out_shape=jax.ShapeDtypeStruct((T, D_MODEL), F32),
        scratch_shapes=[pltpu.VMEM((2, tm, D_MODEL), F32), pltpu.SemaphoreType.DMA((2,))],
        compiler_params=pltpu.CompilerParams(dimension_semantics=("arbitrary",)),
        name="combine",
    )(dest3, x2, route, g_final, ys)


def _routing_tables(route, counts_rec, n_tokens):
    n_slots = 2 * n_tokens
    n_blocks = n_slots // MOE_BLOCK + N_EXPERTS
    n_rows = n_blocks * MOE_BLOCK
    n_pad = n_rows - n_slots
    counts = counts_rec[0, EXPERT_LANE0:EXPERT_LANE0 + N_EXPERTS].astype(jnp.int32)
    padded = (counts + MOE_BLOCK - 1) // MOE_BLOCK * MOE_BLOCK
    pends = jnp.cumsum(padded)
    pstarts = pends - padded
    expert = route[:, 0:2].astype(jnp.int32)
    rank = route[:, 4:6].astype(jnp.int32)
    lookup = lambda table, idx: jnp.sum(jnp.where(idx[..., None] == jnp.arange(N_EXPERTS), table, 0), axis=-1)
    count_le = lambda sorted_vals, q: jnp.sum(sorted_vals[None, :] <= q[:, None], axis=-1).astype(jnp.int32)
    dest = (lookup(pstarts, expert) + rank).T
    block_e = jnp.minimum(count_le(pends, jnp.arange(n_blocks, dtype=jnp.int32) * MOE_BLOCK), N_EXPERTS - 1)
    npad_e = padded - counts
    cum = jnp.cumsum(npad_e)
    idx = jnp.arange(n_pad, dtype=jnp.int32)
    e_of = count_le(cum, idx)
    in_expert = lookup(pstarts + counts - (cum - npad_e), e_of) + idx
    pad_rows = jnp.where(e_of < N_EXPERTS, in_expert, pends[-1] + (idx - cum[-1])).astype(jnp.int32)
    return dest.astype(jnp.int32), block_e, pad_rows, n_rows


def _encoder_group(x, mem, w):
    batch, seq, _ = x.shape
    T = batch * seq
    *qkv_groups, qkv_win = _proj(x, w["g_mix"], w["w_in"])

    slopes_a = _alibi_slopes(A_HEADS).reshape(A_GROUPS, A_HEADS_PER_GROUP)
    o_groups, lse_groups = [], []
    for g, (window, r) in enumerate(A_CONFIGS):
        o, lse = _banded_attention(qkv_groups[g], half_w=window // (2 * r), kv_width=GROUP_W,
                                   slopes=slopes_a[g] * np.float32(r))
        o_groups.append(o)
        lse_groups.append(lse)
    (o_b,) = _banded_attention(qkv_win.reshape(batch, 1, seq, WIN_W), half_w=B_HALF_WINDOW,
                               kv_width=KV2_W, slopes=_alibi_slopes(B_HEADS), sink=w["sink_b"],
                               want_lse=False)

    kvm = _memkv(mem.reshape(-1, D_MODEL), w["g_mem"], w["w_ckv"])
    mix_w = [w[k] for k in ("g_mix", "w_gate", "b_gate", "w_branch", "w_out", "g_xattn", "w_cq",
                            "w_co", "g_ffn", "w_route", "b_route")]
    x2, route, counts_rec = _mix(x, o_groups, lse_groups, o_b, kvm, mix_w)

    dest, block_e, pad_rows, n_rows = _routing_tables(route, counts_rec, T)
    xs = _dispatch(x2, w["g_ffn"], dest, pad_rows, n_rows)
    ys = _experts(xs, block_e, w["w_gu"], w["w_down"])
    y = _combine(x2, route, w["g_final"], dest, ys)
    return y.reshape(batch, seq, D_MODEL)


def _prep_weights(g_mix, w_in, sink_b, w_gate, b_gate, w_branch, w_out, g_xattn, g_mem, w_cq, w_ckv,
                  w_co, g_ffn, w_rg, b_rg, w_re, b_re, w_gu, w_down, g_final):
    scale = HEAD_DIM ** -0.5
    aw = A_WIDTH
    qa, ka, va = w_in[:, :aw] * scale, w_in[:, aw:2 * aw], w_in[:, 2 * aw:3 * aw]
    qb = w_in[:, 3 * aw:3 * aw + B_Q] * scale
    kb = w_in[:, 3 * aw + B_Q:3 * aw + B_Q + B_KV]
    vb = w_in[:, 3 * aw + B_Q + B_KV:]
    twice = lambda t: jnp.repeat(t.reshape(D_MODEL, B_KV_HEADS, 1, HEAD_DIM), 2, axis=2).reshape(D_MODEL, KV2_W)
    group = lambda t, g: t[:, g * GROUP_W:(g + 1) * GROUP_W]
    cols = [group(t, g) for g in range(A_GROUPS) for t in (qa, ka, va)] + [qb, twice(kb), twice(vb)]
    w_in_x = jnp.concatenate(cols, axis=1).astype(BF16)
    w_route = jnp.zeros((D_MODEL, ROUTE_W), F32)
    w_route = w_route.at[:, :N_GROUPS].set(w_rg).at[:, EXPERT_LANE0:EXPERT_LANE0 + N_EXPERTS].set(w_re)
    b_route = jnp.zeros((1, ROUTE_W), F32)
    b_route = b_route.at[0, :N_GROUPS].set(b_rg).at[0, EXPERT_LANE0:EXPERT_LANE0 + N_EXPERTS].set(b_re)
    vec = lambda v: v.reshape(1, -1).astype(F32)
    return dict(
        g_mix=vec(g_mix), w_in=w_in_x, sink_b=sink_b.astype(F32),
        w_gate=w_gate.astype(BF16), b_gate=vec(b_gate), w_branch=w_branch.astype(BF16),
        w_out=w_out.astype(BF16), g_xattn=vec(g_xattn), g_mem=vec(g_mem), w_cq=w_cq.astype(BF16),
        w_ckv=w_ckv.astype(BF16), w_co=w_co.astype(BF16), g_ffn=vec(g_ffn),
        w_route=w_route, b_route=b_route, w_gu=w_gu.astype(BF16), w_down=w_down.astype(BF16),
        g_final=vec(g_final))


def kernel(x_prompt, x_sample, mem_prompt, mem_sample, g_mix, w_in, sink_b, w_gate, b_gate, w_branch,
           w_out, g_xattn, g_mem, w_cq, w_ckv, w_co, g_ffn, w_rg, b_rg, w_re, b_re, w_gu, w_down,
           g_final):
    assert g_mix.shape[0] == 1, "single-layer encoder"
    w = _prep_weights(g_mix[0], w_in[0], sink_b[0], w_gate[0], b_gate[0], w_branch[0], w_out[0],
                      g_xattn[0], g_mem[0], w_cq[0], w_ckv[0], w_co[0], g_ffn[0], w_rg[0], b_rg[0],
                      w_re[0], b_re[0], w_gu[0], w_down[0], g_final)
    return (_encoder_group(x_prompt, mem_prompt, w), _encoder_group(x_sample, mem_sample, w))
```

```python
import functools

import numpy as np
import jax
import jax.numpy as jnp
from jax import lax
from jax.experimental import pallas as pl
from jax.experimental.pallas import tpu as pltpu

F32 = jnp.float32
BF16 = jnp.bfloat16

D_MODEL = 1024
HEAD_DIM = 64
A_CONFIGS = ((128, 1), (512, 4), (2048, 16))
A_GROUPS = 3
A_HEADS_PER_GROUP = 8
A_HEADS = A_GROUPS * A_HEADS_PER_GROUP
A_WIDTH = A_HEADS * HEAD_DIM
B_HEADS = 8
B_KV_HEADS = 2
B_HALF_WINDOW = 128
B_Q = B_HEADS * HEAD_DIM
B_KV = B_KV_HEADS * HEAD_DIM
X_HEADS = 4
X_HEAD_DIM = 128
X_WIDTH = X_HEADS * X_HEAD_DIM
N_GROUPS = 4
EXPERTS_PER_GROUP = 8
N_EXPERTS = N_GROUPS * EXPERTS_PER_GROUP
EXPERT_FF = 512
MOE_BLOCK = 256
EPS = 1e-6
NEG = -1e30

LANES = 128
GROUP_W = A_HEADS_PER_GROUP * HEAD_DIM
PAIR_W = 2 * HEAD_DIM
N_PAIRS = GROUP_W // PAIR_W
QKV_W = 3 * GROUP_W
KV2_W = 2 * B_KV
WIN_W = B_Q + 2 * KV2_W
PROJ_W = A_GROUPS * QKV_W + WIN_W
ROUTE_W = LANES
EXPERT_LANE0 = 32
VMEM_LIMIT = 52 * 1024 * 1024

ROW_TILE = 256
PROJ_TILE = 512
MIX_TILE = 512
Q_TILE = 128
Q_STEPS = 4
ISSUE_UNROLL = 8


def _rms(x, g):
    ms = jnp.mean(x * x, axis=-1, keepdims=True)
    return x * lax.rsqrt(ms + EPS) * g


def _alibi_slopes(n):
    return 2.0 ** (-8.0 * np.arange(1, n + 1, dtype=np.float32) / n)


def _const_spec(shape):
    return pl.BlockSpec(shape, lambda *_: (0,) * len(shape), pipeline_mode=pl.Buffered(1))


def _proj_kernel(x_ref, g_ref, w_ref, *refs):
    o_refs, win_ref, h_ref = refs[:A_GROUPS], refs[A_GROUPS], refs[A_GROUPS + 1]
    tm = x_ref.shape[0]
    h32 = _rms(x_ref[...], g_ref[...])
    h_nat = h32.astype(BF16)
    n_slabs = h_ref.shape[0]
    for s in range(n_slabs):
        h_ref[s] = h32[:, s * LANES:(s + 1) * LANES]
    for g, (_, r) in enumerate(A_CONFIGS):
        n = tm // r
        if r == 1:
            h = h_nat
        else:
            h = jnp.concatenate(
                [jnp.concatenate([h_ref[s, pl.ds(c, n, stride=r), :] for c in range(r)], axis=0)
                 for s in range(n_slabs)], axis=1).astype(BF16)
        for j in range(3):
            cols = slice(g * QKV_W + j * GROUP_W, g * QKV_W + (j + 1) * GROUP_W)
            res = jnp.dot(h, w_ref[:, cols], preferred_element_type=F32).astype(BF16)
            for c in range(r):
                o_refs[g][c, :, j * GROUP_W:(j + 1) * GROUP_W] = res[c * n:(c + 1) * n]
    for j in range(WIN_W // GROUP_W):
        cols = slice(A_GROUPS * QKV_W + j * GROUP_W, A_GROUPS * QKV_W + (j + 1) * GROUP_W)
        win_ref[:, j * GROUP_W:(j + 1) * GROUP_W] = jnp.dot(
            h_nat, w_ref[:, cols], preferred_element_type=F32).astype(BF16)


def _proj(x, g, w):
    batch, seq, _ = x.shape
    tm = PROJ_TILE
    out_specs = [pl.BlockSpec((None, r, tm // r, QKV_W), lambda b, i: (b, 0, i, 0)) for _, r in A_CONFIGS]
    out_shape = [jax.ShapeDtypeStruct((batch, r, seq // r, QKV_W), BF16) for _, r in A_CONFIGS]
    out_specs.append(pl.BlockSpec((None, tm, WIN_W), lambda b, i: (b, i, 0)))
    out_shape.append(jax.ShapeDtypeStruct((batch, seq, WIN_W), BF16))
    return pl.pallas_call(
        _proj_kernel,
        grid=(batch, seq // tm),
        in_specs=[pl.BlockSpec((None, tm, D_MODEL), lambda b, i: (b, i, 0)),
                  _const_spec((1, D_MODEL)),
                  _const_spec((D_MODEL, PROJ_W))],
        out_specs=out_specs,
        out_shape=out_shape,
        scratch_shapes=[pltpu.VMEM((D_MODEL // LANES, tm, LANES), F32)],
        compiler_params=pltpu.CompilerParams(dimension_semantics=("arbitrary",) * 2,
                                             vmem_limit_bytes=VMEM_LIMIT),
        name="proj",
    )(x, g, w)


def _attn_kernel(*refs, qb, kb, q_steps, m_len, half_w, offsets, kv_shared, has_sink, want_lse):
    refs = list(refs)
    sink_ref = refs.pop(0) if has_sink else None
    bias_ref, q_ref, k_ref, v_ref, o_ref = refs[:5]
    lse_ref = refs[5] if want_lse else None
    lo_q = lax.broadcasted_iota(jnp.int32, (qb, PAIR_W), 1) < HEAD_DIM
    first_head = lax.broadcasted_iota(jnp.int32, (2 * qb, 1), 0) < qb
    zeros_q = jnp.zeros((qb, PAIR_W), BF16)

    def q_block(qi, carry):
        gi = pl.program_id(2) * q_steps + qi
        ks = pl.multiple_of(jnp.clip(gi * qb - half_w, 0, m_len - kb), 16)
        off = gi * qb - ks
        var = sum(jnp.where(off == o, n, 0) for n, o in enumerate(offsets))
        rows = pl.ds(pl.multiple_of(qi * qb, qb), qb)
        for j in range(N_PAIRS):
            cols = slice(j * PAIR_W, (j + 1) * PAIR_W)
            jc = (j // 2) if kv_shared else j
            kcols = slice(jc * PAIR_W, (jc + 1) * PAIR_W)
            qp = q_ref[rows, cols]
            q_st = jnp.concatenate([jnp.where(lo_q, qp, zeros_q), jnp.where(lo_q, zeros_q, qp)], axis=0)
            s = lax.dot_general(q_st, k_ref[pl.ds(ks, kb), kcols], (((1,), (1,)), ((), ())),
                                preferred_element_type=F32) + bias_ref[var, j]
            m = jnp.max(s, axis=-1, keepdims=True)
            if has_sink:
                sk = jnp.where(first_head, sink_ref[2 * j], sink_ref[2 * j + 1])
                m = jnp.maximum(m, sk)
            e = jnp.exp(s - m)
            den = jnp.sum(e, axis=-1, keepdims=True)
            if has_sink:
                den = den + jnp.exp(sk - m)
            o2 = jnp.dot(e.astype(BF16), v_ref[pl.ds(ks, kb), kcols], preferred_element_type=F32) / den
            o_ref[rows, cols] = jnp.where(lo_q, o2[:qb], o2[qb:]).astype(BF16)
            if want_lse:
                l2 = m + jnp.log(den)
                lse_ref[rows, cols] = jnp.where(lo_q, l2[:qb], l2[qb:])
        return carry

    lax.fori_loop(0, q_steps, q_block, 0)


def _bias_tables(offsets, slopes, qb, kb, half_w):
    rel = jnp.arange(qb, dtype=jnp.int32)[:, None] - jnp.arange(kb, dtype=jnp.int32)[None, :]
    dist = jnp.abs(rel[None] + jnp.asarray(offsets, jnp.int32)[:, None, None])
    bias = -jnp.asarray(slopes, F32)[None, :, None, None] * dist.astype(F32)[:, None]
    bias = jnp.where((dist <= half_w)[:, None], bias, NEG)
    return bias.reshape(len(offsets), N_PAIRS, 2 * qb, kb)


def _banded_attention(qkv, *, half_w, kv_width, slopes, sink=None, want_lse=True):
    batch, r, m_len, _ = qkv.shape
    qb = min(Q_TILE, m_len)
    kb = min(qb + 2 * half_w, m_len)
    nq = m_len // qb
    q_steps = min(Q_STEPS, nq)
    kv_shared = kv_width != GROUP_W
    has_sink = sink is not None
    k_block = GROUP_W // kv_width
    offsets = sorted({i * qb - min(max(i * qb - half_w, 0), m_len - kb) for i in range(nq)})
    bias = _bias_tables(offsets, slopes, qb, kb, half_w)

    in_specs = []
    args = []
    if has_sink:
        in_specs.append(pl.BlockSpec(memory_space=pltpu.SMEM))
        args.append(sink)
    in_specs += [
        _const_spec(bias.shape),
        pl.BlockSpec((None, None, q_steps * qb, GROUP_W), lambda b, c, i: (b, c, i, 0)),
        pl.BlockSpec((None, None, m_len, kv_width), lambda b, c, i: (b, c, 0, k_block)),
        pl.BlockSpec((None, None, m_len, kv_width), lambda b, c, i: (b, c, 0, k_block + 1)),
    ]
    args += [bias, qkv, qkv, qkv]
    out_spec = pl.BlockSpec((None, None, q_steps * qb, GROUP_W), lambda b, c, i: (b, c, i, 0))
    out_specs = [out_spec]
    out_shape = [jax.ShapeDtypeStruct((batch, r, m_len, GROUP_W), BF16)]
    if want_lse:
        out_specs.append(out_spec)
        out_shape.append(jax.ShapeDtypeStruct((batch, r, m_len, GROUP_W), F32))
    kern = functools.partial(_attn_kernel, qb=qb, kb=kb, q_steps=q_steps, m_len=m_len, half_w=half_w,
                             offsets=tuple(offsets), kv_shared=kv_shared, has_sink=has_sink,
                             want_lse=want_lse)
    return pl.pallas_call(
        kern,
        grid=(batch, r, nq // q_steps),
        in_specs=in_specs,
        out_specs=out_specs,
        out_shape=out_shape,
        compiler_params=pltpu.CompilerParams(dimension_semantics=("arbitrary",) * 3,
                                             vmem_limit_bytes=VMEM_LIMIT),
        name=f"attn_r{r}_w{half_w}",
    )(*args)


def _memkv_kernel(m_ref, g_ref, w_ref, o_ref):
    h = _rms(m_ref[...], g_ref[...]).astype(BF16)
    o_ref[...] = jnp.dot(h, w_ref[...], preferred_element_type=F32).astype(BF16)


def _memkv(mem2d, g, w):
    R = mem2d.shape[0]
    return pl.pallas_call(
        _memkv_kernel,
        grid=(R // ROW_TILE,),
        in_specs=[pl.BlockSpec((ROW_TILE, D_MODEL), lambda i: (i, 0)),
                  _const_spec((1, D_MODEL)),
                  _const_spec((D_MODEL, 2 * X_WIDTH))],
        out_specs=pl.BlockSpec((ROW_TILE, 2 * X_WIDTH), lambda i: (i, 0)),
        out_shape=jax.ShapeDtypeStruct((R, 2 * X_WIDTH), BF16),
        compiler_params=pltpu.CompilerParams(dimension_semantics=("arbitrary",)),
        name="memkv",
    )(mem2d, g, w)


def _token_order(src_ref, dst_ref, m0, n):
    r = src_ref.shape[0]
    if r == 1:
        return src_ref[0, m0:m0 + n].astype(F32)
    n_slabs = dst_ref.shape[0]
    for c in range(r):
        rows = src_ref[c, m0:m0 + n].astype(F32)
        for s in range(n_slabs):
            dst_ref[s, pl.ds(m0 * r + c, n, stride=r), :] = rows[:, s * LANES:(s + 1) * LANES]
    return jnp.concatenate([dst_ref[s, m0 * r:(m0 + n) * r] for s in range(n_slabs)], axis=1)


def _mix_rows(t0, tm, x_ref, o_refs, l_refs, ob_ref, kv_ref,
              g_mix_ref, w_gate_ref, b_gate_ref, w_br_ref, w_out_ref,
              g_x_ref, w_cq_ref, w_co_ref, g_ffn_ref, w_rt_ref, b_rt_ref,
              x2_ref, route_ref, tri_ref, carry_ref, order_refs):
    rows = slice(t0, t0 + tm)
    x = x_ref[rows]
    h1 = _rms(x, g_mix_ref[...]).astype(BF16)

    by_class = lambda ref, scratch: _token_order(ref, scratch, t0 // ref.shape[0], tm // ref.shape[0])
    l0, l1, l2 = (by_class(l, s) for l, s in zip(l_refs, order_refs[:3]))
    lm = jnp.maximum(jnp.maximum(l0, l1), l2)
    e0, e1, e2 = jnp.exp(l0 - lm), jnp.exp(l1 - lm), jnp.exp(l2 - lm)
    den = e0 + e1 + e2
    o0, o1, o2 = (by_class(o, s) for o, s in zip(o_refs, order_refs[3:]))
    oa = (e0 / den) * o0 + (e1 / den) * o1 + (e2 / den) * o2
    br_a = jnp.dot(oa.astype(BF16), w_br_ref[:GROUP_W, :], preferred_element_type=F32)
    br_b = jnp.dot(ob_ref[rows], w_br_ref[GROUP_W:, :], preferred_element_type=F32)
    ga = jax.nn.sigmoid(jnp.dot(h1, w_gate_ref[:, :D_MODEL], preferred_element_type=F32)
                        + b_gate_ref[:, :D_MODEL])
    merged = ga * br_a
    gb = jax.nn.sigmoid(jnp.dot(h1, w_gate_ref[:, D_MODEL:], preferred_element_type=F32)
                        + b_gate_ref[:, D_MODEL:])
    merged = merged + gb * br_b
    x1 = x + jnp.dot(merged.astype(BF16), w_out_ref[...], preferred_element_type=F32)

    h2 = _rms(x1, g_x_ref[...]).astype(BF16)
    q = jnp.dot(h2, w_cq_ref[...], preferred_element_type=F32).astype(BF16)
    heads = []
    for h in range(X_HEADS):
        cols = slice(h * X_HEAD_DIM, (h + 1) * X_HEAD_DIM)
        kh = kv_ref[:, cols]
        vh = kv_ref[:, X_WIDTH + h * X_HEAD_DIM:X_WIDTH + (h + 1) * X_HEAD_DIM]
        s = lax.dot_general(q[:, cols], kh, (((1,), (1,)), ((), ())),
                            preferred_element_type=F32) * (X_HEAD_DIM ** -0.5)
        m = jnp.max(s, axis=-1, keepdims=True)
        e = jnp.exp(s - m)
        p = e / jnp.sum(e, axis=-1, keepdims=True)
        heads.append(jnp.dot(p.astype(BF16), vh, preferred_element_type=F32))
    o = jnp.concatenate(heads, axis=1).astype(BF16)
    x2 = x1 + jnp.dot(o, w_co_ref[...], preferred_element_type=F32)
    x2_ref[rows] = x2

    h3 = _rms(x2, g_ffn_ref[...])
    h3_hi = h3.astype(BF16)
    h3_lo = (h3 - h3_hi.astype(F32)).astype(BF16)
    by_hi = jnp.dot(h3_hi, w_rt_ref[...], preferred_element_type=F32)
    logits = (by_hi[:, :ROUTE_W] + by_hi[:, ROUTE_W:]
              + jnp.dot(h3_lo, w_rt_ref[:, :ROUTE_W], preferred_element_type=F32)) + b_rt_ref[...]
    lane = lax.broadcasted_iota(jnp.int32, (tm, ROUTE_W), 1)
    gmask = lane < N_GROUPS
    gl = jnp.where(gmask, logits, -jnp.inf)
    gmax = jnp.max(gl, axis=-1, keepdims=True)
    gidx = jnp.min(jnp.where(gl == gmax, lane, ROUTE_W), axis=-1, keepdims=True)
    pg_sel = 1.0 / jnp.sum(jnp.where(gmask, jnp.exp(logits - gmax), 0.0), axis=-1, keepdims=True)
    e_lo = EXPERT_LANE0 + gidx * EXPERTS_PER_GROUP
    emask = (lane >= e_lo) & (lane < e_lo + EXPERTS_PER_GROUP)
    el = jnp.where(emask, logits, -jnp.inf)
    emax1 = jnp.max(el, axis=-1, keepdims=True)
    i1 = jnp.min(jnp.where(el == emax1, lane, ROUTE_W), axis=-1, keepdims=True)
    el2 = jnp.where(lane == i1, -jnp.inf, el)
    emax2 = jnp.max(el2, axis=-1, keepdims=True)
    i2 = jnp.min(jnp.where(el2 == emax2, lane, ROUTE_W), axis=-1, keepdims=True)
    t2 = jnp.exp(emax2 - emax1)
    w1 = pg_sel / (1.0 + t2)
    w2 = pg_sel * t2 / (1.0 + t2)

    oh1 = lane == i1
    oh2 = lane == i2
    ohs = jnp.where(oh1 | oh2, 1.0, 0.0)
    before = jnp.dot(tri_ref[...], ohs.astype(BF16), preferred_element_type=F32) + carry_ref[...]
    rank1 = jnp.sum(jnp.where(oh1, before, 0.0), axis=-1, keepdims=True)
    rank2 = jnp.sum(jnp.where(oh2, before, 0.0), axis=-1, keepdims=True)
    carry_ref[...] = carry_ref[...] + jnp.sum(ohs, axis=0, keepdims=True)

    rec = jnp.where(lane == 0, (i1 - EXPERT_LANE0).astype(F32), 0.0)
    rec = jnp.where(lane == 1, (i2 - EXPERT_LANE0).astype(F32), rec)
    rec = jnp.where(lane == 2, w1, rec)
    rec = jnp.where(lane == 3, w2, rec)
    rec = jnp.where(lane == 4, rank1, rec)
    rec = jnp.where(lane == 5, rank2, rec)
    route_ref[rows] = rec


def _mix_kernel(x_ref, o0_ref, o1_ref, o2_ref, l0_ref, l1_ref, l2_ref, ob_ref, *refs):
    n_w = 12
    w_refs, order_refs = refs[:n_w], refs[n_w + 5:]
    x2_ref, route_ref, counts_ref, tri_ref, carry_ref = refs[n_w:n_w + 5]
    sub = tri_ref.shape[0]

    @pl.when((pl.program_id(0) == 0) & (pl.program_id(1) == 0))
    def _():
        row = lax.broadcasted_iota(jnp.int32, (sub, sub), 0)
        col = lax.broadcasted_iota(jnp.int32, (sub, sub), 1)
        tri_ref[...] = jnp.where(row > col, 1.0, 0.0).astype(BF16)
        carry_ref[...] = jnp.zeros_like(carry_ref)

    for t0 in range(0, x_ref.shape[0], sub):
        _mix_rows(t0, sub, x_ref, (o0_ref, o1_ref, o2_ref), (l0_ref, l1_ref, l2_ref), ob_ref, *w_refs,
                  x2_ref, route_ref, tri_ref, carry_ref, order_refs)
    counts_ref[...] = jnp.broadcast_to(carry_ref[...], counts_ref.shape)


def _mix(x, o_groups, lse_groups, o_b, kvm, wts):
    batch, seq, _ = x.shape
    T = batch * seq
    tm = MIX_TILE
    steps = seq // tm
    mem_tokens = kvm.shape[0] // batch
    row = lambda w: pl.BlockSpec((tm, w), lambda b, i: (b * steps + i, 0))
    by_class = [pl.BlockSpec((None, r, tm // r, GROUP_W), lambda b, i: (b, 0, i, 0)) for _, r in A_CONFIGS]
    in_specs = ([pl.BlockSpec((None, tm, D_MODEL), lambda b, i: (b, i, 0))] + by_class + by_class
                + [pl.BlockSpec((None, None, tm, GROUP_W), lambda b, i: (b, 0, i, 0)),
                   pl.BlockSpec((mem_tokens, 2 * X_WIDTH), lambda b, i: (b, 0))]
                + [_const_spec(w.shape) for w in wts])
    return pl.pallas_call(
        _mix_kernel,
        grid=(batch, steps),
        in_specs=in_specs,
        out_specs=[row(D_MODEL), row(ROUTE_W), pl.BlockSpec((8, ROUTE_W), lambda b, i: (0, 0))],
        out_shape=[jax.ShapeDtypeStruct((T, D_MODEL), F32),
                   jax.ShapeDtypeStruct((T, ROUTE_W), F32),
                   jax.ShapeDtypeStruct((8, ROUTE_W), F32)],
        scratch_shapes=[pltpu.VMEM((ROW_TILE, ROW_TILE), BF16), pltpu.VMEM((1, ROUTE_W), F32)]
        + [pltpu.VMEM((GROUP_W // LANES, tm, LANES), F32)] * (2 * A_GROUPS),
        compiler_params=pltpu.CompilerParams(dimension_semantics=("arbitrary",) * 2,
                                             vmem_limit_bytes=VMEM_LIMIT),
        name="mix",
    )(x, *o_groups, *lse_groups, o_b, kvm, *wts)


def _dispatch_kernel(dest_ref, pad_ref, x2_ref, g_ref, xs_hbm, hbuf, zbuf, sems):
    tm = x2_ref.shape[0]
    n_pad = pad_ref.shape[-1]
    hbuf[...] = _rms(x2_ref[...], g_ref[...])
    zbuf[...] = jnp.zeros_like(zbuf)

    def row_copy(t, k):
        return pltpu.make_async_copy(hbuf.at[pl.ds(t, 1)], xs_hbm.at[pl.ds(dest_ref[0, k, t], 1)],
                                     sems.at[k])

    def pad_copy(t):
        return pltpu.make_async_copy(zbuf.at[pl.ds(0, 1)], xs_hbm.at[pl.ds(pad_ref[0, 0, t], 1)],
                                     sems.at[2])

    def issue(t, c):
        row_copy(t, 0).start()
        row_copy(t, 1).start()
        return c

    lax.fori_loop(0, tm, issue, 0, unroll=ISSUE_UNROLL)

    def issue_pad(t, c):
        pad_copy(t).start()
        return c

    lax.fori_loop(0, n_pad, issue_pad, 0, unroll=ISSUE_UNROLL)
    for k in range(2):
        pltpu.make_async_copy(hbuf, xs_hbm.at[pl.ds(0, tm)], sems.at[k]).wait()
    pltpu.make_async_copy(hbuf.at[pl.ds(0, n_pad)], xs_hbm.at[pl.ds(0, n_pad)], sems.at[2]).wait()


def _dispatch(x2, g_ffn, dest, pad_rows, n_rows):
    T = x2.shape[0]
    tm = ROW_TILE
    n_steps = T // tm
    n_pad = pad_rows.shape[0] // n_steps
    dest3 = dest.reshape(2, n_steps, tm).transpose(1, 0, 2)
    pad3 = pad_rows.reshape(n_steps, 1, n_pad)
    return pl.pallas_call(
        _dispatch_kernel,
        grid=(n_steps,),
        in_specs=[pl.BlockSpec((1, 2, tm), lambda i: (i, 0, 0), memory_space=pltpu.SMEM),
                  pl.BlockSpec((1, 1, n_pad), lambda i: (i, 0, 0), memory_space=pltpu.SMEM),
                  pl.BlockSpec((tm, D_MODEL), lambda i: (i, 0)),
                  _const_spec((1, D_MODEL))],
        out_specs=pl.BlockSpec(memory_space=pl.ANY),
        out_shape=jax.ShapeDtypeStruct((n_rows, D_MODEL), F32),
        scratch_shapes=[pltpu.VMEM((tm, D_MODEL), F32), pltpu.VMEM((8, D_MODEL), F32),
                        pltpu.SemaphoreType.DMA((3,))],
        compiler_params=pltpu.CompilerParams(dimension_semantics=("arbitrary",),
                                             has_side_effects=True),
        name="dispatch",
    )(dest3, pad3, x2, g_ffn)


def _expert_kernel(be_ref, xs_ref, wgu_ref, wdn_ref, ys_ref):
    xb = xs_ref[...].astype(BF16)
    gu = jnp.dot(xb, wgu_ref[...], preferred_element_type=F32)
    gate, up = gu[:, :EXPERT_FF], gu[:, EXPERT_FF:]
    act = (gate * jax.nn.sigmoid(gate) * up).astype(BF16)
    ys_ref[...] = jnp.dot(act, wdn_ref[...], preferred_element_type=F32)


def _experts(xs, block_e, w_gu, w_down):
    P = xs.shape[0]
    n_blocks = P // MOE_BLOCK
    grid_spec = pltpu.PrefetchScalarGridSpec(
        num_scalar_prefetch=1,
        grid=(n_blocks,),
        in_specs=[pl.BlockSpec((MOE_BLOCK, D_MODEL), lambda b, be: (b, 0)),
                  pl.BlockSpec((None, D_MODEL, 2 * EXPERT_FF), lambda b, be: (be[b], 0, 0)),
                  pl.BlockSpec((None, EXPERT_FF, D_MODEL), lambda b, be: (be[b], 0, 0))],
        out_specs=pl.BlockSpec((MOE_BLOCK, D_MODEL), lambda b, be: (b, 0)),
    )
    return pl.pallas_call(
        _expert_kernel,
        grid_spec=grid_spec,
        out_shape=jax.ShapeDtypeStruct((P, D_MODEL), F32),
        compiler_params=pltpu.CompilerParams(dimension_semantics=("arbitrary",),
                                             vmem_limit_bytes=VMEM_LIMIT),
        name="experts",
    )(block_e, xs, w_gu, w_down)


def _combine_kernel(dest_ref, x2_ref, route_ref, g_ref, ys_hbm, out_ref, buf, sems):
    tm = x2_ref.shape[0]

    def row_copy(t, k):
        return pltpu.make_async_copy(ys_hbm.at[pl.ds(dest_ref[0, k, t], 1)],
                                     buf.at[k, pl.ds(t, 1)], sems.at[k])

    def issue(t, c):
        row_copy(t, 0).start()
        row_copy(t, 1).start()
        return c

    lax.fori_loop(0, tm, issue, 0, unroll=ISSUE_UNROLL)
    for k in range(2):
        pltpu.make_async_copy(ys_hbm.at[pl.ds(0, tm)], buf.at[k], sems.at[k]).wait()
    route = route_ref[...]
    moe = buf[0] * route[:, 2:3] + buf[1] * route[:, 3:4]
    out_ref[...] = _rms(x2_ref[...] + moe, g_ref[...])


def _combine(x2, route, g_final, dest, ys):
    T = x2.shape[0]
    tm = ROW_TILE
    n_steps = T // tm
    dest3 = dest.reshape(2, n_steps, tm).transpose(1, 0, 2)
    return pl.pallas_call(
        _combine_kernel,
        grid=(n_steps,),
        in_specs=[pl.BlockSpec((1, 2, tm), lambda i: (i, 0, 0), memory_space=pltpu.SMEM),
                  pl.BlockSpec((tm, D_MODEL), lambda i: (i, 0)),
                  pl.BlockSpec((tm, ROUTE_W), lambda i: (i, 0)),
                  _const_spec((1, D_MODEL)),
                  pl.BlockSpec(memory_space=pl.ANY)],
        out_specs=pl.BlockSpec((tm, D_MODEL), lambda i: (i, 0)),
        out_shape=jax.ShapeDtypeStruct((T, D_MODEL), F32),
        scratch_shapes=[pltpu.VMEM((2, tm, D_MODEL), F32), pltpu.SemaphoreType.DMA((2,))],
        compiler_params=pltpu.CompilerParams(dimension_semantics=("arbitrary",)),
        name="combine",
    )(dest3, x2, route, g_final, ys)


def _routing_tables(route, counts_rec, n_tokens):
    n_slots = 2 * n_tokens
    n_blocks = n_slots // MOE_BLOCK + N_EXPERTS
    n_rows = n_blocks * MOE_BLOCK
    n_pad = n_rows - n_slots
    counts = counts_rec[0, EXPERT_LANE0:EXPERT_LANE0 + N_EXPERTS].astype(jnp.int32)
    padded = (counts + MOE_BLOCK - 1) // MOE_BLOCK * MOE_BLOCK
    pends = jnp.cumsum(padded)
    pstarts = pends - padded
    expert = route[:, 0:2].astype(jnp.int32)
    rank = route[:, 4:6].astype(jnp.int32)
    lookup = lambda table, idx: jnp.sum(jnp.where(idx[..., None] == jnp.arange(N_EXPERTS), table, 0), axis=-1)
    count_le = lambda sorted_vals, q: jnp.sum(sorted_vals[None, :] <= q[:, None], axis=-1).astype(jnp.int32)
    dest = (lookup(pstarts, expert) + rank).T
    block_e = jnp.minimum(count_le(pends, jnp.arange(n_blocks, dtype=jnp.int32) * MOE_BLOCK), N_EXPERTS - 1)
    npad_e = padded - counts
    cum = jnp.cumsum(npad_e)
    idx = jnp.arange(n_pad, dtype=jnp.int32)
    e_of = count_le(cum, idx)
    in_expert = lookup(pstarts + counts - (cum - npad_e), e_of) + idx
    pad_rows = jnp.where(e_of < N_EXPERTS, in_expert, pends[-1] + (idx - cum[-1])).astype(jnp.int32)
    return dest.astype(jnp.int32), block_e, pad_rows, n_rows


def _encoder_group(x, mem, w):
    batch, seq, _ = x.shape
    T = batch * seq
    *qkv_groups, qkv_win = _proj(x, w["g_mix"], w["w_in"])

    slopes_a = _alibi_slopes(A_HEADS).reshape(A_GROUPS, A_HEADS_PER_GROUP)
    o_groups, lse_groups = [], []
    for g, (window, r) in enumerate(A_CONFIGS):
        o, lse = _banded_attention(qkv_groups[g], half_w=window // (2 * r), kv_width=GROUP_W,
                                   slopes=slopes_a[g] * np.float32(r))
        o_groups.append(o)
        lse_groups.append(lse)
    (o_b,) = _banded_attention(qkv_win.reshape(batch, 1, seq, WIN_W), half_w=B_HALF_WINDOW,
                               kv_width=KV2_W, slopes=_alibi_slopes(B_HEADS), sink=w["sink_b"],
                               want_lse=False)

    kvm = _memkv(mem.reshape(-1, D_MODEL), w["g_mem"], w["w_ckv"])
    mix_w = [w[k] for k in ("g_mix", "w_gate", "b_gate", "w_branch", "w_out", "g_xattn", "w_cq",
                            "w_co", "g_ffn", "w_route", "b_route")]
    x2, route, counts_rec = _mix(x, o_groups, lse_groups, o_b, kvm, mix_w)

    dest, block_e, pad_rows, n_rows = _routing_tables(route, counts_rec, T)
    xs = _dispatch(x2, w["g_ffn"], dest, pad_rows, n_rows)
    ys = _experts(xs, block_e, w["w_gu"], w["w_down"])
    y = _combine(x2, route, w["g_final"], dest, ys)
    return y.reshape(batch, seq, D_MODEL)


def _prep_weights(g_mix, w_in, sink_b, w_gate, b_gate, w_branch, w_out, g_xattn, g_mem, w_cq, w_ckv,
                  w_co, g_ffn, w_rg, b_rg, w_re, b_re, w_gu, w_down, g_final):
    scale = HEAD_DIM ** -0.5
    aw = A_WIDTH
    qa, ka, va = w_in[:, :aw] * scale, w_in[:, aw:2 * aw], w_in[:, 2 * aw:3 * aw]
    qb = w_in[:, 3 * aw:3 * aw + B_Q] * scale
    kb = w_in[:, 3 * aw + B_Q:3 * aw + B_Q + B_KV]
    vb = w_in[:, 3 * aw + B_Q + B_KV:]
    twice = lambda t: jnp.repeat(t.reshape(D_MODEL, B_KV_HEADS, 1, HEAD_DIM), 2, axis=2).reshape(D_MODEL, KV2_W)
    group = lambda t, g: t[:, g * GROUP_W:(g + 1) * GROUP_W]
    cols = [group(t, g) for g in range(A_GROUPS) for t in (qa, ka, va)] + [qb, twice(kb), twice(vb)]
    w_in_x = jnp.concatenate(cols, axis=1).astype(BF16)
    w_route = jnp.zeros((D_MODEL, ROUTE_W), F32)
    w_route = w_route.at[:, :N_GROUPS].set(w_rg).at[:, EXPERT_LANE0:EXPERT_LANE0 + N_EXPERTS].set(w_re)
    b_route = jnp.zeros((1, ROUTE_W), F32)
    b_route = b_route.at[0, :N_GROUPS].set(b_rg).at[0, EXPERT_LANE0:EXPERT_LANE0 + N_EXPERTS].set(b_re)
    w_route_hi = w_route.astype(BF16)
    w_route = jnp.concatenate([w_route_hi, (w_route - w_route_hi.astype(F32)).astype(BF16)], axis=1)
    vec = lambda v: v.reshape(1, -1).astype(F32)
    return dict(
        g_mix=vec(g_mix), w_in=w_in_x, sink_b=sink_b.astype(F32),
        w_gate=w_gate.astype(BF16), b_gate=vec(b_gate), w_branch=w_branch.astype(BF16),
        w_out=w_out.astype(BF16), g_xattn=vec(g_xattn), g_mem=vec(g_mem), w_cq=w_cq.astype(BF16),
        w_ckv=w_ckv.astype(BF16), w_co=w_co.astype(BF16), g_ffn=vec(g_ffn),
        w_route=w_route, b_route=b_route, w_gu=w_gu.astype(BF16), w_down=w_down.astype(BF16),
        g_final=vec(g_final))


def kernel(x_prompt, x_sample, mem_prompt, mem_sample, g_mix, w_in, sink_b, w_gate, b_gate, w_branch,
           w_out, g_xattn, g_mem, w_cq, w_ckv, w_co, g_ffn, w_rg, b_rg, w_re, b_re, w_gu, w_down,
           g_final):
    assert g_mix.shape[0] == 1, "single-layer encoder"
    w = _prep_weights(g_mix[0], w_in[0], sink_b[0], w_gate[0], b_gate[0], w_branch[0], w_out[0],
                      g_xattn[0], g_mem[0], w_cq[0], w_ckv[0], w_co[0], g_ffn[0], w_rg[0], b_rg[0],
                      w_re[0], b_re[0], w_gu[0], w_down[0], g_final)
    return (_encoder_group(x_prompt, mem_prompt, w), _encoder_group(x_sample, mem_sample, w))
```

```python
import functools

import numpy as np
import jax
import jax.numpy as jnp
from jax import lax
from jax.experimental import pallas as pl
from jax.experimental.pallas import tpu as pltpu

F32 = jnp.float32
BF16 = jnp.bfloat16

D_MODEL = 1024
HEAD_DIM = 64
A_CONFIGS = ((128, 1), (512, 4), (2048, 16))
A_GROUPS = 3
A_HEADS_PER_GROUP = 8
A_HEADS = A_GROUPS * A_HEADS_PER_GROUP
A_WIDTH = A_HEADS * HEAD_DIM
B_HEADS = 8
B_KV_HEADS = 2
B_HALF_WINDOW = 128
B_Q = B_HEADS * HEAD_DIM
B_KV = B_KV_HEADS * HEAD_DIM
X_HEADS = 4
X_HEAD_DIM = 128
X_WIDTH = X_HEADS * X_HEAD_DIM
N_GROUPS = 4
EXPERTS_PER_GROUP = 8
N_EXPERTS = N_GROUPS * EXPERTS_PER_GROUP
EXPERT_FF = 512
MOE_BLOCK = 256
EPS = 1e-6
NEG = -1e30

LANES = 128
GROUP_W = A_HEADS_PER_GROUP * HEAD_DIM
PAIR_W = 2 * HEAD_DIM
N_PAIRS = GROUP_W // PAIR_W
QKV_W = 3 * GROUP_W
KV2_W = 2 * B_KV
WIN_W = B_Q + 2 * KV2_W
PROJ_W = A_GROUPS * QKV_W + WIN_W
ROUTE_W = LANES
EXPERT_LANE0 = 32
VMEM_LIMIT = 52 * 1024 * 1024

ROW_TILE = 256
PROJ_TILE = 512
MIX_TILE = 512
Q_TILE = 128
Q_STEPS = 4

def _rms(x, g):
    ms = jnp.mean(x * x, axis=-1, keepdims=True)
    return x * lax.rsqrt(ms + EPS) * g


def _alibi_slopes(n):
    return 2.0 ** (-8.0 * np.arange(1, n + 1, dtype=np.float32) / n)


def _const_spec(shape):
    return pl.BlockSpec(shape, lambda *_: (0,) * len(shape), pipeline_mode=pl.Buffered(1))


def _proj_kernel(x_ref, g_ref, w_ref, *refs):
    o_refs, win_ref, h_ref = refs[:A_GROUPS], refs[A_GROUPS], refs[A_GROUPS + 1]
    tm = x_ref.shape[0]
    h32 = _rms(x_ref[...], g_ref[...])
    h_nat = h32.astype(BF16)
    n_slabs = h_ref.shape[0]
    for s in range(n_slabs):
        h_ref[s] = h32[:, s * LANES:(s + 1) * LANES]
    for g, (_, r) in enumerate(A_CONFIGS):
        n = tm // r
        if r == 1:
            h = h_nat
        else:
            h = jnp.concatenate(
                [jnp.concatenate([h_ref[s, pl.ds(c, n, stride=r), :] for c in range(r)], axis=0)
                 for s in range(n_slabs)], axis=1).astype(BF16)
        for j in range(3):
            cols = slice(g * QKV_W + j * GROUP_W, g * QKV_W + (j + 1) * GROUP_W)
            res = jnp.dot(h, w_ref[:, cols], preferred_element_type=F32).astype(BF16)
            for c in range(r):
                o_refs[g][c, :, j * GROUP_W:(j + 1) * GROUP_W] = res[c * n:(c + 1) * n]
    for j in range(WIN_W // GROUP_W):
        cols = slice(A_GROUPS * QKV_W + j * GROUP_W, A_GROUPS * QKV_W + (j + 1) * GROUP_W)
        win_ref[:, j * GROUP_W:(j + 1) * GROUP_W] = jnp.dot(
            h_nat, w_ref[:, cols], preferred_element_type=F32).astype(BF16)


def _proj(x, g, w):
    batch, seq, _ = x.shape
    tm = PROJ_TILE
    out_specs = [pl.BlockSpec((None, r, tm // r, QKV_W), lambda b, i: (b, 0, i, 0)) for _, r in A_CONFIGS]
    out_shape = [jax.ShapeDtypeStruct((batch, r, seq // r, QKV_W), BF16) for _, r in A_CONFIGS]
    out_specs.append(pl.BlockSpec((None, tm, WIN_W), lambda b, i: (b, i, 0)))
    out_shape.append(jax.ShapeDtypeStruct((batch, seq, WIN_W), BF16))
    return pl.pallas_call(
        _proj_kernel,
        grid=(batch, seq // tm),
        in_specs=[pl.BlockSpec((None, tm, D_MODEL), lambda b, i: (b, i, 0)),
                  _const_spec((1, D_MODEL)),
                  _const_spec((D_MODEL, PROJ_W))],
        out_specs=out_specs,
        out_shape=out_shape,
        scratch_shapes=[pltpu.VMEM((D_MODEL // LANES, tm, LANES), F32)],
        compiler_params=pltpu.CompilerParams(dimension_semantics=("arbitrary",) * 2,
                                             vmem_limit_bytes=VMEM_LIMIT),
        name="proj",
    )(x, g, w)


def _attn_kernel(*refs, qb, kb, q_steps, m_len, half_w, offsets, kv_shared, has_sink, want_lse):
    refs = list(refs)
    sink_ref = refs.pop(0) if has_sink else None
    bias_ref, q_ref, k_ref, v_ref, o_ref = refs[:5]
    lse_ref = refs[5] if want_lse else None
    lo_q = lax.broadcasted_iota(jnp.int32, (qb, PAIR_W), 1) < HEAD_DIM
    first_head = lax.broadcasted_iota(jnp.int32, (2 * qb, 1), 0) < qb
    zeros_q = jnp.zeros((qb, PAIR_W), BF16)

    def q_block(it, carry):
        cc, qi = it // q_steps, it % q_steps
        gi = pl.program_id(2) * q_steps + qi
        ks = pl.multiple_of(jnp.clip(gi * qb - half_w, 0, m_len - kb), 16)
        off = gi * qb - ks
        var = sum(jnp.where(off == o, n, 0) for n, o in enumerate(offsets))
        rows = pl.ds(pl.multiple_of(qi * qb, qb), qb)
        for j in range(N_PAIRS):
            cols = slice(j * PAIR_W, (j + 1) * PAIR_W)
            jc = (j // 2) if kv_shared else j
            kcols = slice(jc * PAIR_W, (jc + 1) * PAIR_W)
            qp = q_ref[cc, rows, cols]
            q_st = jnp.concatenate([jnp.where(lo_q, qp, zeros_q), jnp.where(lo_q, zeros_q, qp)], axis=0)
            s = lax.dot_general(q_st, k_ref[cc, pl.ds(ks, kb), kcols], (((1,), (1,)), ((), ())),
                                preferred_element_type=F32) + bias_ref[var, j]
            m = jnp.max(s, axis=-1, keepdims=True)
            if has_sink:
                sk = jnp.where(first_head, sink_ref[2 * j], sink_ref[2 * j + 1])
                m = jnp.maximum(m, sk)
            e = jnp.exp(s - m)
            den = jnp.sum(e, axis=-1, keepdims=True)
            if has_sink:
                den = den + jnp.exp(sk - m)
            o2 = jnp.dot(e.astype(BF16), v_ref[cc, pl.ds(ks, kb), kcols], preferred_element_type=F32) / den
            o_ref[cc, rows, cols] = jnp.where(lo_q, o2[:qb], o2[qb:]).astype(BF16)
            if want_lse:
                l2 = m + jnp.log(den)
                lse_ref[cc, rows, cols] = jnp.where(lo_q, l2[:qb], l2[qb:])
        return carry

    lax.fori_loop(0, q_ref.shape[0] * q_steps, q_block, 0)


def _bias_tables(offsets, slopes, qb, kb, half_w):
    rel = jnp.arange(qb, dtype=jnp.int32)[:, None] - jnp.arange(kb, dtype=jnp.int32)[None, :]
    dist = jnp.abs(rel[None] + jnp.asarray(offsets, jnp.int32)[:, None, None])
    bias = -jnp.asarray(slopes, F32)[None, :, None, None] * dist.astype(F32)[:, None]
    bias = jnp.where((dist <= half_w)[:, None], bias, NEG)
    return bias.reshape(len(offsets), N_PAIRS, 2 * qb, kb)


def _banded_attention(qkv, *, half_w, kv_width, slopes, sink=None, want_lse=True):
    batch, r, m_len, _ = qkv.shape
    qb = min(Q_TILE, m_len)
    kb = min(qb + 2 * half_w, m_len)
    nq = m_len // qb
    q_steps = min(Q_STEPS, nq)
    cb = min(r, Q_STEPS // q_steps)
    kv_shared = kv_width != GROUP_W
    has_sink = sink is not None
    k_block = GROUP_W // kv_width
    offsets = sorted({i * qb - min(max(i * qb - half_w, 0), m_len - kb) for i in range(nq)})
    bias = _bias_tables(offsets, slopes, qb, kb, half_w)

    in_specs = []
    args = []
    if has_sink:
        in_specs.append(pl.BlockSpec(memory_space=pltpu.SMEM))
        args.append(sink)
    in_specs += [
        _const_spec(bias.shape),
        pl.BlockSpec((None, cb, q_steps * qb, GROUP_W), lambda b, c, i: (b, c, i, 0)),
        pl.BlockSpec((None, cb, m_len, kv_width), lambda b, c, i: (b, c, 0, k_block)),
        pl.BlockSpec((None, cb, m_len, kv_width), lambda b, c, i: (b, c, 0, k_block + 1)),
    ]
    args += [bias, qkv, qkv, qkv]
    out_spec = pl.BlockSpec((None, cb, q_steps * qb, GROUP_W), lambda b, c, i: (b, c, i, 0))
    out_specs = [out_spec]
    out_shape = [jax.ShapeDtypeStruct((batch, r, m_len, GROUP_W), BF16)]
    if want_lse:
        out_specs.append(out_spec)
        out_shape.append(jax.ShapeDtypeStruct((batch, r, m_len, GROUP_W), F32))
    kern = functools.partial(_attn_kernel, qb=qb, kb=kb, q_steps=q_steps, m_len=m_len, half_w=half_w,
                             offsets=tuple(offsets), kv_shared=kv_shared, has_sink=has_sink,
                             want_lse=want_lse)
    return pl.pallas_call(
        kern,
        grid=(batch, r // cb, nq // q_steps),
        in_specs=in_specs,
        out_specs=out_specs,
        out_shape=out_shape,
        compiler_params=pltpu.CompilerParams(dimension_semantics=("arbitrary",) * 3,
                                             vmem_limit_bytes=VMEM_LIMIT),
        name=f"attn_r{r}_w{half_w}",
    )(*args)


def _memkv_kernel(m_ref, g_ref, w_ref, o_ref):
    h = _rms(m_ref[...], g_ref[...]).astype(BF16)
    o_ref[...] = jnp.dot(h, w_ref[...], preferred_element_type=F32).astype(BF16)


def _memkv(mem2d, g, w):
    R = mem2d.shape[0]
    return pl.pallas_call(
        _memkv_kernel,
        grid=(R // ROW_TILE,),
        in_specs=[pl.BlockSpec((ROW_TILE, D_MODEL), lambda i: (i, 0)),
                  _const_spec((1, D_MODEL)),
                  _const_spec((D_MODEL, 2 * X_WIDTH))],
        out_specs=pl.BlockSpec((ROW_TILE, 2 * X_WIDTH), lambda i: (i, 0)),
        out_shape=jax.ShapeDtypeStruct((R, 2 * X_WIDTH), BF16),
        compiler_params=pltpu.CompilerParams(dimension_semantics=("arbitrary",)),
        name="memkv",
    )(mem2d, g, w)


def _token_order(src_ref, dst_ref, m0, n):
    r = src_ref.shape[0]
    if r == 1:
        return src_ref[0, m0:m0 + n].astype(F32)
    n_slabs = dst_ref.shape[0]
    for c in range(r):
        rows = src_ref[c, m0:m0 + n].astype(F32)
        for s in range(n_slabs):
            dst_ref[s, pl.ds(m0 * r + c, n, stride=r), :] = rows[:, s * LANES:(s + 1) * LANES]
    return jnp.concatenate([dst_ref[s, m0 * r:(m0 + n) * r] for s in range(n_slabs)], axis=1)


def _mix_rows(t0, tm, x_ref, o_refs, l_refs, ob_ref, kv_ref,
              g_mix_ref, w_gate_ref, b_gate_ref, w_br_ref, w_out_ref,
              g_x_ref, w_cq_ref, w_co_ref, g_ffn_ref, w_rt_ref, b_rt_ref,
              x2_ref, hp_ref, route_ref, tri_ref, carry_ref, order_refs):
    rows = slice(t0, t0 + tm)
    x = x_ref[rows]
    h1 = _rms(x, g_mix_ref[...]).astype(BF16)

    by_class = lambda ref, scratch: _token_order(ref, scratch, t0 // ref.shape[0], tm // ref.shape[0])
    l0, l1, l2 = (by_class(l, s) for l, s in zip(l_refs, order_refs[:3]))
    lm = jnp.maximum(jnp.maximum(l0, l1), l2)
    e0, e1, e2 = jnp.exp(l0 - lm), jnp.exp(l1 - lm), jnp.exp(l2 - lm)
    den = e0 + e1 + e2
    o0, o1, o2 = (by_class(o, s) for o, s in zip(o_refs, order_refs[3:]))
    oa = (e0 / den) * o0 + (e1 / den) * o1 + (e2 / den) * o2
    br_a = jnp.dot(oa.astype(BF16), w_br_ref[:GROUP_W, :], preferred_element_type=F32)
    br_b = jnp.dot(ob_ref[rows], w_br_ref[GROUP_W:, :], preferred_element_type=F32)
    ga = jax.nn.sigmoid(jnp.dot(h1, w_gate_ref[:, :D_MODEL], preferred_element_type=F32)
                        + b_gate_ref[:, :D_MODEL])
    merged = ga * br_a
    gb = jax.nn.sigmoid(jnp.dot(h1, w_gate_ref[:, D_MODEL:], preferred_element_type=F32)
                        + b_gate_ref[:, D_MODEL:])
    merged = merged + gb * br_b
    x1 = x + jnp.dot(merged.astype(BF16), w_out_ref[...], preferred_element_type=F32)

    h2 = _rms(x1, g_x_ref[...]).astype(BF16)
    q = jnp.dot(h2, w_cq_ref[...], preferred_element_type=F32).astype(BF16)
    heads = []
    for h in range(X_HEADS):
        cols = slice(h * X_HEAD_DIM, (h + 1) * X_HEAD_DIM)
        kh = kv_ref[:, cols]
        vh = kv_ref[:, X_WIDTH + h * X_HEAD_DIM:X_WIDTH + (h + 1) * X_HEAD_DIM]
        s = lax.dot_general(q[:, cols], kh, (((1,), (1,)), ((), ())),
                            preferred_element_type=F32) * (X_HEAD_DIM ** -0.5)
        m = jnp.max(s, axis=-1, keepdims=True)
        e = jnp.exp(s - m)
        p = e / jnp.sum(e, axis=-1, keepdims=True)
        heads.append(jnp.dot(p.astype(BF16), vh, preferred_element_type=F32))
    o = jnp.concatenate(heads, axis=1).astype(BF16)
    x2 = x1 + jnp.dot(o, w_co_ref[...], preferred_element_type=F32)
    x2_ref[rows] = x2

    h3 = _rms(x2, g_ffn_ref[...])
    h3_hi = h3.astype(BF16)
    hp_ref[rows] = _pack_bf16_pairs(h3_hi.astype(F32))
    h3_lo = (h3 - h3_hi.astype(F32)).astype(BF16)
    by_hi = jnp.dot(h3_hi, w_rt_ref[...], preferred_element_type=F32)
    logits = (by_hi[:, :ROUTE_W] + by_hi[:, ROUTE_W:]
              + jnp.dot(h3_lo, w_rt_ref[:, :ROUTE_W], preferred_element_type=F32)) + b_rt_ref[...]
    lane = lax.broadcasted_iota(jnp.int32, (tm, ROUTE_W), 1)
    gmask = lane < N_GROUPS
    gl = jnp.where(gmask, logits, -jnp.inf)
    gmax = jnp.max(gl, axis=-1, keepdims=True)
    gidx = jnp.min(jnp.where(gl == gmax, lane, ROUTE_W), axis=-1, keepdims=True)
    pg_sel = 1.0 / jnp.sum(jnp.where(gmask, jnp.exp(logits - gmax), 0.0), axis=-1, keepdims=True)
    e_lo = EXPERT_LANE0 + gidx * EXPERTS_PER_GROUP
    emask = (lane >= e_lo) & (lane < e_lo + EXPERTS_PER_GROUP)
    el = jnp.where(emask, logits, -jnp.inf)
    emax1 = jnp.max(el, axis=-1, keepdims=True)
    i1 = jnp.min(jnp.where(el == emax1, lane, ROUTE_W), axis=-1, keepdims=True)
    el2 = jnp.where(lane == i1, -jnp.inf, el)
    emax2 = jnp.max(el2, axis=-1, keepdims=True)
    i2 = jnp.min(jnp.where(el2 == emax2, lane, ROUTE_W), axis=-1, keepdims=True)
    t2 = jnp.exp(emax2 - emax1)
    w1 = pg_sel / (1.0 + t2)
    w2 = pg_sel * t2 / (1.0 + t2)

    oh1 = lane == i1
    oh2 = lane == i2
    ohs = jnp.where(oh1 | oh2, 1.0, 0.0)
    before = jnp.dot(tri_ref[...], ohs.astype(BF16), preferred_element_type=F32) + carry_ref[...]
    rank1 = jnp.sum(jnp.where(oh1, before, 0.0), axis=-1, keepdims=True)
    rank2 = jnp.sum(jnp.where(oh2, before, 0.0), axis=-1, keepdims=True)
    carry_ref[...] = carry_ref[...] + jnp.sum(ohs, axis=0, keepdims=True)

    rec = jnp.where(lane == 0, (i1 - EXPERT_LANE0).astype(F32), 0.0)
    rec = jnp.where(lane == 1, (i2 - EXPERT_LANE0).astype(F32), rec)
    rec = jnp.where(lane == 2, w1, rec)
    rec = jnp.where(lane == 3, w2, rec)
    rec = jnp.where(lane == 4, rank1, rec)
    rec = jnp.where(lane == 5, rank2, rec)
    route_ref[rows] = rec


def _mix_kernel(x_ref, o0_ref, o1_ref, o2_ref, l0_ref, l1_ref, l2_ref, ob_ref, *refs):
    n_w = 12
    w_refs, order_refs = refs[:n_w], refs[n_w + 6:]
    x2_ref, hp_ref, route_ref, counts_ref, tri_ref, carry_ref = refs[n_w:n_w + 6]
    sub = tri_ref.shape[0]

    @pl.when((pl.program_id(0) == 0) & (pl.program_id(1) == 0))
    def _():
        row = lax.broadcasted_iota(jnp.int32, (sub, sub), 0)
        col = lax.broadcasted_iota(jnp.int32, (sub, sub), 1)
        tri_ref[...] = jnp.where(row > col, 1.0, 0.0).astype(BF16)
        carry_ref[...] = jnp.zeros_like(carry_ref)

    for t0 in range(0, x_ref.shape[0], sub):
        _mix_rows(t0, sub, x_ref, (o0_ref, o1_ref, o2_ref), (l0_ref, l1_ref, l2_ref), ob_ref, *w_refs,
                  x2_ref, hp_ref, route_ref, tri_ref, carry_ref, order_refs)
    counts_ref[...] = jnp.broadcast_to(carry_ref[...], counts_ref.shape)


def _mix(x, o_groups, lse_groups, o_b, kvm, wts):
    batch, seq, _ = x.shape
    T = batch * seq
    tm = MIX_TILE
    steps = seq // tm
    mem_tokens = kvm.shape[0] // batch
    row = lambda w: pl.BlockSpec((tm, w), lambda b, i: (b * steps + i, 0))
    by_class = [pl.BlockSpec((None, r, tm // r, GROUP_W), lambda b, i: (b, 0, i, 0)) for _, r in A_CONFIGS]
    in_specs = ([pl.BlockSpec((None, tm, D_MODEL), lambda b, i: (b, i, 0))] + by_class + by_class
                + [pl.BlockSpec((None, None, tm, GROUP_W), lambda b, i: (b, 0, i, 0)),
                   pl.BlockSpec((mem_tokens, 2 * X_WIDTH), lambda b, i: (b, 0))]
                + [_const_spec(w.shape) for w in wts])
    return pl.pallas_call(
        _mix_kernel,
        grid=(batch, steps),
        in_specs=in_specs,
        out_specs=[row(D_MODEL), row(D_MODEL // 2), row(ROUTE_W),
                   pl.BlockSpec((8, ROUTE_W), lambda b, i: (0, 0))],
        out_shape=[jax.ShapeDtypeStruct((T, D_MODEL), F32),
                   jax.ShapeDtypeStruct((T, D_MODEL // 2), jnp.uint32),
                   jax.ShapeDtypeStruct((T, ROUTE_W), F32),
                   jax.ShapeDtypeStruct((8, ROUTE_W), F32)],
        scratch_shapes=[pltpu.VMEM((ROW_TILE, ROW_TILE), BF16), pltpu.VMEM((1, ROUTE_W), F32)]
        + [pltpu.VMEM((GROUP_W // LANES, tm, LANES), F32)] * (2 * A_GROUPS),
        compiler_params=pltpu.CompilerParams(dimension_semantics=("arbitrary",) * 2,
                                             vmem_limit_bytes=VMEM_LIMIT),
        name="mix",
    )(x, *o_groups, *lse_groups, o_b, kvm, *wts)


def _pack_bf16_pairs(x):
    w = x.shape[1] // 2
    lo = lax.bitcast_convert_type(x[:, :w], jnp.uint32) >> 16
    hi = lax.bitcast_convert_type(x[:, w:], jnp.uint32) & jnp.uint32(0xFFFF0000)
    return lo | hi


def _unpack_bf16_pairs(u):
    lo = lax.bitcast_convert_type(u << 16, F32)
    hi = lax.bitcast_convert_type(u & jnp.uint32(0xFFFF0000), F32)
    return jnp.concatenate([lo, hi], axis=1)


def _expert_kernel(be_ref, src_ref, dst_ref, h_hbm, wgu_ref, wdn_ref, out_hbm, xbuf, ybuf, sems):
    s = pl.program_id(0)
    last = pl.num_programs(0) - 1
    gather_sem, scatter_sem = sems.at[0], sems.at[1]

    def wait_previous_step():
        pltpu.make_async_copy(h_hbm.at[pl.ds(0, MOE_BLOCK)], xbuf.at[0], gather_sem).wait()
        pltpu.make_async_copy(ybuf.at[0], out_hbm.at[pl.ds(0, MOE_BLOCK)], scatter_sem).wait()

    @pl.when(s == 0)
    def _():
        xbuf[...] = jnp.zeros_like(xbuf)
        ybuf[...] = jnp.zeros_like(ybuf)

    @pl.when(s > 0)
    def _():
        wait_previous_step()

    io_slot = s % 2
    mm_slot = 1 - io_slot
    for t in range(MOE_BLOCK):
        pltpu.make_async_copy(h_hbm.at[pl.ds(src_ref[0, 0, t], 1)], xbuf.at[io_slot, pl.ds(t, 1)],
                              gather_sem).start()
        pltpu.make_async_copy(ybuf.at[io_slot, pl.ds(t, 1)], out_hbm.at[pl.ds(dst_ref[0, 0, t], 1)],
                              scatter_sem).start()
    xb = _unpack_bf16_pairs(xbuf[mm_slot]).astype(BF16)
    gu = jnp.dot(xb, wgu_ref[...], preferred_element_type=F32)
    gate, up = gu[:, :EXPERT_FF], gu[:, EXPERT_FF:]
    act = (gate * jax.nn.sigmoid(gate) * up).astype(BF16)
    y = jnp.dot(act, wdn_ref[...], preferred_element_type=F32)
    ybuf[mm_slot] = _pack_bf16_pairs(y.astype(BF16).astype(F32))

    @pl.when(s == last)
    def _():
        wait_previous_step()


def _experts(h_packed, block_e, src_rows, dst_rows, n_out_rows, w_gu, w_down):
    n_blocks = block_e.shape[0]
    half_d = D_MODEL // 2
    src3 = src_rows.reshape(n_blocks, 1, MOE_BLOCK)
    dst3 = dst_rows.reshape(n_blocks + 2, 1, MOE_BLOCK)
    computed = lambda s, be: be[jnp.clip(s - 1, 0, n_blocks - 1)]
    grid_spec = pltpu.PrefetchScalarGridSpec(
        num_scalar_prefetch=1,
        grid=(n_blocks + 2,),
        in_specs=[pl.BlockSpec((1, 1, MOE_BLOCK), lambda s, be: (jnp.minimum(s, n_blocks - 1), 0, 0),
                               memory_space=pltpu.SMEM),
                  pl.BlockSpec((1, 1, MOE_BLOCK), lambda s, be: (s, 0, 0), memory_space=pltpu.SMEM),
                  pl.BlockSpec(memory_space=pl.ANY),
                  pl.BlockSpec((None, D_MODEL, 2 * EXPERT_FF), lambda s, be: (computed(s, be), 0, 0)),
                  pl.BlockSpec((None, EXPERT_FF, D_MODEL), lambda s, be: (computed(s, be), 0, 0))],
        out_specs=pl.BlockSpec(memory_space=pl.ANY),
        scratch_shapes=[pltpu.VMEM((2, MOE_BLOCK, half_d), jnp.uint32),
                        pltpu.VMEM((2, MOE_BLOCK, half_d), jnp.uint32),
                        pltpu.SemaphoreType.DMA((2,))],
    )
    return pl.pallas_call(
        _expert_kernel,
        grid_spec=grid_spec,
        out_shape=jax.ShapeDtypeStruct((n_out_rows, half_d), jnp.uint32),
        compiler_params=pltpu.CompilerParams(dimension_semantics=("arbitrary",),
                                             vmem_limit_bytes=VMEM_LIMIT),
        name="experts",
    )(block_e, src3, dst3, h_packed, w_gu, w_down)


def _combine_kernel(x2_ref, route_ref, g_ref, y1_ref, y2_ref, out_ref):
    route = route_ref[...]
    moe = _unpack_bf16_pairs(y1_ref[...]) * route[:, 2:3] + _unpack_bf16_pairs(y2_ref[...]) * route[:, 3:4]
    out_ref[...] = _rms(x2_ref[...] + moe, g_ref[...])


def _combine(x2, route, g_final, ys):
    T = x2.shape[0]
    tm = MIX_TILE
    n_steps = T // tm
    half_d = D_MODEL // 2
    return pl.pallas_call(
        _combine_kernel,
        grid=(n_steps,),
        in_specs=[pl.BlockSpec((tm, D_MODEL), lambda i: (i, 0)),
                  pl.BlockSpec((tm, ROUTE_W), lambda i: (i, 0)),
                  _const_spec((1, D_MODEL)),
                  pl.BlockSpec((tm, half_d), lambda i: (i, 0)),
                  pl.BlockSpec((tm, half_d), lambda i: (i + n_steps, 0))],
        out_specs=pl.BlockSpec((tm, D_MODEL), lambda i: (i, 0)),
        out_shape=jax.ShapeDtypeStruct((T, D_MODEL), F32),
        compiler_params=pltpu.CompilerParams(dimension_semantics=("arbitrary",)),
        name="combine",
    )(x2, route, g_final, ys, ys)


def _routing_tables(route, counts_rec, n_tokens):
    n_slots = 2 * n_tokens
    n_blocks = n_slots // MOE_BLOCK + N_EXPERTS
    n_rows = n_blocks * MOE_BLOCK
    n_pad = n_rows - n_slots
    counts = counts_rec[0, EXPERT_LANE0:EXPERT_LANE0 + N_EXPERTS].astype(jnp.int32)
    padded = (counts + MOE_BLOCK - 1) // MOE_BLOCK * MOE_BLOCK
    pends = jnp.cumsum(padded)
    pstarts = pends - padded
    expert = route[:, 0:2].astype(jnp.int32)
    rank = route[:, 4:6].astype(jnp.int32)
    lookup = lambda table, idx: jnp.sum(jnp.where(idx[..., None] == jnp.arange(N_EXPERTS), table, 0), axis=-1)
    count_le = lambda sorted_vals, q: jnp.sum(sorted_vals[None, :] <= q[:, None], axis=-1).astype(jnp.int32)
    dest = (lookup(pstarts, expert) + rank).T.reshape(-1)
    block_e = jnp.minimum(count_le(pends, jnp.arange(n_blocks, dtype=jnp.int32) * MOE_BLOCK), N_EXPERTS - 1)
    slot_of = jnp.full((n_rows,), -1, jnp.int32).at[dest].set(
        jnp.arange(n_slots, dtype=jnp.int32), unique_indices=True, indices_are_sorted=False)
    unused = slot_of < 0
    spare = n_slots + jnp.cumsum(unused.astype(jnp.int32)) - 1
    src_rows = jnp.where(unused, 0, jnp.where(slot_of >= n_tokens, slot_of - n_tokens, slot_of))
    dst_rows = jnp.where(unused, spare, slot_of)
    warmup = n_slots + n_pad + jnp.arange(2 * MOE_BLOCK, dtype=jnp.int32)
    n_out_rows = n_slots + n_pad + 2 * MOE_BLOCK
    return block_e, src_rows.astype(jnp.int32), jnp.concatenate([warmup, dst_rows]).astype(jnp.int32), n_out_rows


def _encoder_group(x, mem, w):
    batch, seq, _ = x.shape
    T = batch * seq
    *qkv_groups, qkv_win = _proj(x, w["g_mix"], w["w_in"])

    slopes_a = _alibi_slopes(A_HEADS).reshape(A_GROUPS, A_HEADS_PER_GROUP)
    o_groups, lse_groups = [], []
    for g, (window, r) in enumerate(A_CONFIGS):
        o, lse = _banded_attention(qkv_groups[g], half_w=window // (2 * r), kv_width=GROUP_W,
                                   slopes=slopes_a[g] * np.float32(r))
        o_groups.append(o)
        lse_groups.append(lse)
    (o_b,) = _banded_attention(qkv_win.reshape(batch, 1, seq, WIN_W), half_w=B_HALF_WINDOW,
                               kv_width=KV2_W, slopes=_alibi_slopes(B_HEADS), sink=w["sink_b"],
                               want_lse=False)

    kvm = _memkv(mem.reshape(-1, D_MODEL), w["g_mem"], w["w_ckv"])
    mix_w = [w[k] for k in ("g_mix", "w_gate", "b_gate", "w_branch", "w_out", "g_xattn", "w_cq",
                            "w_co", "g_ffn", "w_route", "b_route")]
    x2, h_packed, route, counts_rec = _mix(x, o_groups, lse_groups, o_b, kvm, mix_w)

    block_e, src_rows, dst_rows, n_out_rows = _routing_tables(route, counts_rec, T)
    ys = _experts(h_packed, block_e, src_rows, dst_rows, n_out_rows, w["w_gu"], w["w_down"])
    y = _combine(x2, route, w["g_final"], ys)
    return y.reshape(batch, seq, D_MODEL)


def _prep_weights(g_mix, w_in, sink_b, w_gate, b_gate, w_branch, w_out, g_xattn, g_mem, w_cq, w_ckv,
                  w_co, g_ffn, w_rg, b_rg, w_re, b_re, w_gu, w_down, g_final):
    scale = HEAD_DIM ** -0.5
    aw = A_WIDTH
    qa, ka, va = w_in[:, :aw] * scale, w_in[:, aw:2 * aw], w_in[:, 2 * aw:3 * aw]
    qb = w_in[:, 3 * aw:3 * aw + B_Q] * scale
    kb = w_in[:, 3 * aw + B_Q:3 * aw + B_Q + B_KV]
    vb = w_in[:, 3 * aw + B_Q + B_KV:]
    twice = lambda t: jnp.repeat(t.reshape(D_MODEL, B_KV_HEADS, 1, HEAD_DIM), 2, axis=2).reshape(D_MODEL, KV2_W)
    group = lambda t, g: t[:, g * GROUP_W:(g + 1) * GROUP_W]
    cols = [group(t, g) for g in range(A_GROUPS) for t in (qa, ka, va)] + [qb, twice(kb), twice(vb)]
    w_in_x = jnp.concatenate(cols, axis=1).astype(BF16)
    w_route = jnp.zeros((D_MODEL, ROUTE_W), F32)
    w_route = w_route.at[:, :N_GROUPS].set(w_rg).at[:, EXPERT_LANE0:EXPERT_LANE0 + N_EXPERTS].set(w_re)
    b_route = jnp.zeros((1, ROUTE_W), F32)
    b_route = b_route.at[0, :N_GROUPS].set(b_rg).at[0, EXPERT_LANE0:EXPERT_LANE0 + N_EXPERTS].set(b_re)
    w_route_hi = w_route.astype(BF16)
    w_route = jnp.concatenate([w_route_hi, (w_route - w_route_hi.astype(F32)).astype(BF16)], axis=1)
    vec = lambda v: v.reshape(1, -1).astype(F32)
    return dict(
        g_mix=vec(g_mix), w_in=w_in_x, sink_b=sink_b.astype(F32),
        w_gate=w_gate.astype(BF16), b_gate=vec(b_gate), w_branch=w_branch.astype(BF16),
        w_out=w_out.astype(BF16), g_xattn=vec(g_xattn), g_mem=vec(g_mem), w_cq=w_cq.astype(BF16),
        w_ckv=w_ckv.astype(BF16), w_co=w_co.astype(BF16), g_ffn=vec(g_ffn),
        w_route=w_route, b_route=b_route, w_gu=w_gu.astype(BF16), w_down=w_down.astype(BF16),
        g_final=vec(g_final))


def kernel(x_prompt, x_sample, mem_prompt, mem_sample, g_mix, w_in, sink_b, w_gate, b_gate, w_branch,
           w_out, g_xattn, g_mem, w_cq, w_ckv, w_co, g_ffn, w_rg, b_rg, w_re, b_re, w_gu, w_down,
           g_final):
    assert g_mix.shape[0] == 1, "single-layer encoder"
    w = _prep_weights(g_mix[0], w_in[0], sink_b[0], w_gate[0], b_gate[0], w_branch[0], w_out[0],
                      g_xattn[0], g_mem[0], w_cq[0], w_ckv[0], w_co[0], g_ffn[0], w_rg[0], b_rg[0],
                      w_re[0], b_re[0], w_gu[0], w_down[0], g_final)
    return (_encoder_group(x_prompt, mem_prompt, w), _encoder_group(x_sample, mem_sample, w))
```

```python
import functools

import numpy as np
import jax
import jax.numpy as jnp
from jax import lax
from jax.experimental import pallas as pl
from jax.experimental.pallas import tpu as pltpu

F32 = jnp.float32
BF16 = jnp.bfloat16

D_MODEL = 1024
HEAD_DIM = 64
A_CONFIGS = ((128, 1), (512, 4), (2048, 16))
A_GROUPS = 3
A_HEADS_PER_GROUP = 8
A_HEADS = A_GROUPS * A_HEADS_PER_GROUP
A_WIDTH = A_HEADS * HEAD_DIM
B_HEADS = 8
B_KV_HEADS = 2
B_HALF_WINDOW = 128
B_Q = B_HEADS * HEAD_DIM
B_KV = B_KV_HEADS * HEAD_DIM
X_HEADS = 4
X_HEAD_DIM = 128
X_WIDTH = X_HEADS * X_HEAD_DIM
N_GROUPS = 4
EXPERTS_PER_GROUP = 8
N_EXPERTS = N_GROUPS * EXPERTS_PER_GROUP
EXPERT_FF = 512
MOE_BLOCK = 256
EPS = 1e-6
NEG = -1e30

LANES = 128
GROUP_W = A_HEADS_PER_GROUP * HEAD_DIM
PAIR_W = 2 * HEAD_DIM
N_PAIRS = GROUP_W // PAIR_W
QKV_W = 3 * GROUP_W
KV2_W = 2 * B_KV
WIN_W = B_Q + 2 * KV2_W
PROJ_W = A_GROUPS * QKV_W + WIN_W
ROUTE_W = LANES
EXPERT_LANE0 = 32
VMEM_LIMIT = 52 * 1024 * 1024

ROW_TILE = 256
PROJ_TILE = 512
MIX_TILE = 512
MIX_SUB = 256
Q_TILE = 128
Q_STEPS = 4
Q_UNROLL = 4
ISSUE_UNROLL = 8

def _rms(x, g):
    ms = jnp.mean(x * x, axis=-1, keepdims=True)
    return x * lax.rsqrt(ms + EPS) * g


def _alibi_slopes(n):
    return 2.0 ** (-8.0 * np.arange(1, n + 1, dtype=np.float32) / n)


def _const_spec(shape):
    return pl.BlockSpec(shape, lambda *_: (0,) * len(shape), pipeline_mode=pl.Buffered(1))


def _proj_kernel(x_ref, g_ref, w_ref, *refs):
    o_refs, win_ref, h_ref = refs[:A_GROUPS], refs[A_GROUPS], refs[A_GROUPS + 1]
    tm = x_ref.shape[0]
    h32 = _rms(x_ref[...], g_ref[...])
    h_nat = h32.astype(BF16)
    n_slabs = h_ref.shape[0]
    for s in range(n_slabs):
        h_ref[s] = h32[:, s * LANES:(s + 1) * LANES]
    for g, (_, r) in enumerate(A_CONFIGS):
        n = tm // r
        if r == 1:
            h = h_nat
        else:
            h = jnp.concatenate(
                [jnp.concatenate([h_ref[s, pl.ds(c, n, stride=r), :] for c in range(r)], axis=0)
                 for s in range(n_slabs)], axis=1).astype(BF16)
        for j in range(3):
            cols = slice(g * QKV_W + j * GROUP_W, g * QKV_W + (j + 1) * GROUP_W)
            res = jnp.dot(h, w_ref[:, cols], preferred_element_type=F32).astype(BF16)
            for c in range(r):
                o_refs[g][c, :, j * GROUP_W:(j + 1) * GROUP_W] = res[c * n:(c + 1) * n]
    for j in range(WIN_W // GROUP_W):
        cols = slice(A_GROUPS * QKV_W + j * GROUP_W, A_GROUPS * QKV_W + (j + 1) * GROUP_W)
        win_ref[:, j * GROUP_W:(j + 1) * GROUP_W] = jnp.dot(
            h_nat, w_ref[:, cols], preferred_element_type=F32).astype(BF16)


def _proj(x, g, w):
    batch, seq, _ = x.shape
    tm = PROJ_TILE
    out_specs = [pl.BlockSpec((None, r, tm // r, QKV_W), lambda b, i: (b, 0, i, 0)) for _, r in A_CONFIGS]
    out_shape = [jax.ShapeDtypeStruct((batch, r, seq // r, QKV_W), BF16) for _, r in A_CONFIGS]
    out_specs.append(pl.BlockSpec((None, tm, WIN_W), lambda b, i: (b, i, 0)))
    out_shape.append(jax.ShapeDtypeStruct((batch, seq, WIN_W), BF16))
    return pl.pallas_call(
        _proj_kernel,
        grid=(batch, seq // tm),
        in_specs=[pl.BlockSpec((None, tm, D_MODEL), lambda b, i: (b, i, 0)),
                  _const_spec((1, D_MODEL)),
                  _const_spec((D_MODEL, PROJ_W))],
        out_specs=out_specs,
        out_shape=out_shape,
        scratch_shapes=[pltpu.VMEM((D_MODEL // LANES, tm, LANES), F32)],
        compiler_params=pltpu.CompilerParams(dimension_semantics=("arbitrary",) * 2,
                                             vmem_limit_bytes=VMEM_LIMIT),
        name="proj",
    )(x, g, w)


def _attn_kernel(*refs, qb, kb, q_steps, m_len, half_w, offsets, kv_shared, has_sink, want_lse):
    refs = list(refs)
    sink_ref = refs.pop(0) if has_sink else None
    bias_ref, q_ref, k_ref, v_ref, o_ref = refs[:5]
    lse_ref = refs[5] if want_lse else None
    lo_q = lax.broadcasted_iota(jnp.int32, (qb, PAIR_W), 1) < HEAD_DIM
    first_head = lax.broadcasted_iota(jnp.int32, (2 * qb, 1), 0) < qb
    zeros_q = jnp.zeros((qb, PAIR_W), BF16)

    def q_block(it, carry):
        cc, qi = it // q_steps, it % q_steps
        gi = pl.program_id(2) * q_steps + qi
        ks = pl.multiple_of(jnp.clip(gi * qb - half_w, 0, m_len - kb), 16)
        off = gi * qb - ks
        var = sum(jnp.where(off == o, n, 0) for n, o in enumerate(offsets))
        rows = pl.ds(pl.multiple_of(qi * qb, qb), qb)
        for j in range(N_PAIRS):
            cols = slice(j * PAIR_W, (j + 1) * PAIR_W)
            jc = (j // 2) if kv_shared else j
            kcols = slice(jc * PAIR_W, (jc + 1) * PAIR_W)
            qp = q_ref[cc, rows, cols]
            q_st = jnp.concatenate([jnp.where(lo_q, qp, zeros_q), jnp.where(lo_q, zeros_q, qp)], axis=0)
            s = lax.dot_general(q_st, k_ref[cc, pl.ds(ks, kb), kcols], (((1,), (1,)), ((), ())),
                                preferred_element_type=F32) + bias_ref[var, j]
            m = jnp.max(s, axis=-1, keepdims=True)
            if has_sink:
                sk = jnp.where(first_head, sink_ref[2 * j], sink_ref[2 * j + 1])
                m = jnp.maximum(m, sk)
            e = jnp.exp(s - m)
            den = jnp.sum(e, axis=-1, keepdims=True)
            if has_sink:
                den = den + jnp.exp(sk - m)
            o2 = jnp.dot(e.astype(BF16), v_ref[cc, pl.ds(ks, kb), kcols], preferred_element_type=F32) / den
            o_ref[cc, rows, cols] = jnp.where(lo_q, o2[:qb], o2[qb:]).astype(BF16)
            if want_lse:
                l2 = m + jnp.log(den)
                lse_ref[cc, rows, cols] = jnp.where(lo_q, l2[:qb], l2[qb:])
        return carry

    n_iter = q_ref.shape[0] * q_steps
    lax.fori_loop(0, n_iter, q_block, 0, unroll=min(Q_UNROLL, n_iter))


def _bias_tables(offsets, slopes, qb, kb, half_w):
    rel = jnp.arange(qb, dtype=jnp.int32)[:, None] - jnp.arange(kb, dtype=jnp.int32)[None, :]
    dist = jnp.abs(rel[None] + jnp.asarray(offsets, jnp.int32)[:, None, None])
    bias = -jnp.asarray(slopes, F32)[None, :, None, None] * dist.astype(F32)[:, None]
    bias = jnp.where((dist <= half_w)[:, None], bias, NEG)
    return bias.reshape(len(offsets), N_PAIRS, 2 * qb, kb)


def _banded_attention(qkv, *, half_w, kv_width, slopes, sink=None, want_lse=True):
    batch, r, m_len, _ = qkv.shape
    qb = min(Q_TILE, m_len)
    kb = min(qb + 2 * half_w, m_len)
    nq = m_len // qb
    q_steps = min(Q_STEPS, nq)
    cb = min(r, Q_STEPS // q_steps)
    kv_shared = kv_width != GROUP_W
    has_sink = sink is not None
    k_block = GROUP_W // kv_width
    offsets = sorted({i * qb - min(max(i * qb - half_w, 0), m_len - kb) for i in range(nq)})
    bias = _bias_tables(offsets, slopes, qb, kb, half_w)

    in_specs = []
    args = []
    if has_sink:
        in_specs.append(pl.BlockSpec(memory_space=pltpu.SMEM))
        args.append(sink)
    in_specs += [
        _const_spec(bias.shape),
        pl.BlockSpec((None, cb, q_steps * qb, GROUP_W), lambda b, c, i: (b, c, i, 0)),
        pl.BlockSpec((None, cb, m_len, kv_width), lambda b, c, i: (b, c, 0, k_block)),
        pl.BlockSpec((None, cb, m_len, kv_width), lambda b, c, i: (b, c, 0, k_block + 1)),
    ]
    args += [bias, qkv, qkv, qkv]
    out_spec = pl.BlockSpec((None, cb, q_steps * qb, GROUP_W), lambda b, c, i: (b, c, i, 0))
    out_specs = [out_spec]
    out_shape = [jax.ShapeDtypeStruct((batch, r, m_len, GROUP_W), BF16)]
    if want_lse:
        out_specs.append(out_spec)
        out_shape.append(jax.ShapeDtypeStruct((batch, r, m_len, GROUP_W), F32))
    kern = functools.partial(_attn_kernel, qb=qb, kb=kb, q_steps=q_steps, m_len=m_len, half_w=half_w,
                             offsets=tuple(offsets), kv_shared=kv_shared, has_sink=has_sink,
                             want_lse=want_lse)
    return pl.pallas_call(
        kern,
        grid=(batch, r // cb, nq // q_steps),
        in_specs=in_specs,
        out_specs=out_specs,
        out_shape=out_shape,
        compiler_params=pltpu.CompilerParams(dimension_semantics=("arbitrary",) * 3,
                                             vmem_limit_bytes=VMEM_LIMIT),
        name=f"attn_r{r}_w{half_w}",
    )(*args)


def _memkv_kernel(m_ref, g_ref, w_ref, o_ref):
    h = _rms(m_ref[...], g_ref[...]).astype(BF16)
    o_ref[...] = jnp.dot(h, w_ref[...], preferred_element_type=F32).astype(BF16)


def _memkv(mem2d, g, w):
    R = mem2d.shape[0]
    return pl.pallas_call(
        _memkv_kernel,
        grid=(R // ROW_TILE,),
        in_specs=[pl.BlockSpec((ROW_TILE, D_MODEL), lambda i: (i, 0)),
                  _const_spec((1, D_MODEL)),
                  _const_spec((D_MODEL, 2 * X_WIDTH))],
        out_specs=pl.BlockSpec((ROW_TILE, 2 * X_WIDTH), lambda i: (i, 0)),
        out_shape=jax.ShapeDtypeStruct((R, 2 * X_WIDTH), BF16),
        compiler_params=pltpu.CompilerParams(dimension_semantics=("arbitrary",)),
        name="memkv",
    )(mem2d, g, w)


def _token_order(src_ref, dst_ref, m0, n):
    r = src_ref.shape[0]
    if r == 1:
        return src_ref[0, m0:m0 + n].astype(F32)
    n_slabs = dst_ref.shape[0]
    for c in range(r):
        rows = src_ref[c, m0:m0 + n].astype(F32)
        for s in range(n_slabs):
            dst_ref[s, pl.ds(m0 * r + c, n, stride=r), :] = rows[:, s * LANES:(s + 1) * LANES]
    return jnp.concatenate([dst_ref[s, m0 * r:(m0 + n) * r] for s in range(n_slabs)], axis=1)


def _mix_rows(t0, tm, x_ref, o_refs, l_refs, ob_ref, kv_ref,
              g_mix_ref, w_gate_ref, b_gate_ref, w_br_ref, w_out_ref,
              g_x_ref, w_cq_ref, w_co_ref, g_ffn_ref, w_rt_ref, b_rt_ref,
              x2_ref, hp_ref, route_ref, tri_ref, carry_ref, order_refs):
    rows = slice(t0, t0 + tm)
    x = x_ref[rows]
    h1 = _rms(x, g_mix_ref[...]).astype(BF16)

    by_class = lambda ref, scratch: _token_order(ref, scratch, t0 // ref.shape[0], tm // ref.shape[0])
    l0, l1, l2 = (by_class(l, s) for l, s in zip(l_refs, order_refs[:3]))
    lm = jnp.maximum(jnp.maximum(l0, l1), l2)
    e0, e1, e2 = jnp.exp(l0 - lm), jnp.exp(l1 - lm), jnp.exp(l2 - lm)
    den = e0 + e1 + e2
    o0, o1, o2 = (by_class(o, s) for o, s in zip(o_refs, order_refs[3:]))
    oa = (e0 / den) * o0 + (e1 / den) * o1 + (e2 / den) * o2
    br_a = jnp.dot(oa.astype(BF16), w_br_ref[:GROUP_W, :], preferred_element_type=F32)
    br_b = jnp.dot(ob_ref[rows], w_br_ref[GROUP_W:, :], preferred_element_type=F32)
    ga = jax.nn.sigmoid(jnp.dot(h1, w_gate_ref[:, :D_MODEL], preferred_element_type=F32)
                        + b_gate_ref[:, :D_MODEL])
    merged = ga * br_a
    gb = jax.nn.sigmoid(jnp.dot(h1, w_gate_ref[:, D_MODEL:], preferred_element_type=F32)
                        + b_gate_ref[:, D_MODEL:])
    merged = merged + gb * br_b
    x1 = x + jnp.dot(merged.astype(BF16), w_out_ref[...], preferred_element_type=F32)

    h2 = _rms(x1, g_x_ref[...]).astype(BF16)
    q = jnp.dot(h2, w_cq_ref[...], preferred_element_type=F32).astype(BF16)
    heads = []
    for h in range(X_HEADS):
        cols = slice(h * X_HEAD_DIM, (h + 1) * X_HEAD_DIM)
        kh = kv_ref[:, cols]
        vh = kv_ref[:, X_WIDTH + h * X_HEAD_DIM:X_WIDTH + (h + 1) * X_HEAD_DIM]
        s = lax.dot_general(q[:, cols], kh, (((1,), (1,)), ((), ())),
                            preferred_element_type=F32) * (X_HEAD_DIM ** -0.5)
        m = jnp.max(s, axis=-1, keepdims=True)
        e = jnp.exp(s - m)
        p = e / jnp.sum(e, axis=-1, keepdims=True)
        heads.append(jnp.dot(p.astype(BF16), vh, preferred_element_type=F32))
    o = jnp.concatenate(heads, axis=1).astype(BF16)
    x2 = x1 + jnp.dot(o, w_co_ref[...], preferred_element_type=F32)
    x2_ref[rows] = x2

    h3 = _rms(x2, g_ffn_ref[...])
    h3_hi = h3.astype(BF16)
    hp_ref[rows] = _pack_bf16_pairs(h3_hi.astype(F32))
    h3_lo = (h3 - h3_hi.astype(F32)).astype(BF16)
    by_hi = jnp.dot(h3_hi, w_rt_ref[...], preferred_element_type=F32)
    logits = (by_hi[:, :ROUTE_W] + by_hi[:, ROUTE_W:]
              + jnp.dot(h3_lo, w_rt_ref[:, :ROUTE_W], preferred_element_type=F32)) + b_rt_ref[...]
    lane = lax.broadcasted_iota(jnp.int32, (tm, ROUTE_W), 1)
    gmask = lane < N_GROUPS
    gl = jnp.where(gmask, logits, -jnp.inf)
    gmax = jnp.max(gl, axis=-1, keepdims=True)
    gidx = jnp.min(jnp.where(gl == gmax, lane, ROUTE_W), axis=-1, keepdims=True)
    pg_sel = 1.0 / jnp.sum(jnp.where(gmask, jnp.exp(logits - gmax), 0.0), axis=-1, keepdims=True)
    e_lo = EXPERT_LANE0 + gidx * EXPERTS_PER_GROUP
    emask = (lane >= e_lo) & (lane < e_lo + EXPERTS_PER_GROUP)
    el = jnp.where(emask, logits, -jnp.inf)
    emax1 = jnp.max(el, axis=-1, keepdims=True)
    i1 = jnp.min(jnp.where(el == emax1, lane, ROUTE_W), axis=-1, keepdims=True)
    el2 = jnp.where(lane == i1, -jnp.inf, el)
    emax2 = jnp.max(el2, axis=-1, keepdims=True)
    i2 = jnp.min(jnp.where(el2 == emax2, lane, ROUTE_W), axis=-1, keepdims=True)
    t2 = jnp.exp(emax2 - emax1)
    w1 = pg_sel / (1.0 + t2)
    w2 = pg_sel * t2 / (1.0 + t2)

    oh1 = lane == i1
    oh2 = lane == i2
    ohs = jnp.where(oh1 | oh2, 1.0, 0.0)
    before = jnp.dot(tri_ref[...], ohs.astype(BF16), preferred_element_type=F32) + carry_ref[...]
    rank1 = jnp.sum(jnp.where(oh1, before, 0.0), axis=-1, keepdims=True)
    rank2 = jnp.sum(jnp.where(oh2, before, 0.0), axis=-1, keepdims=True)
    carry_ref[...] = carry_ref[...] + jnp.sum(ohs, axis=0, keepdims=True)

    rec = jnp.where(lane == 0, (i1 - EXPERT_LANE0).astype(F32), 0.0)
    rec = jnp.where(lane == 1, (i2 - EXPERT_LANE0).astype(F32), rec)
    rec = jnp.where(lane == 2, w1, rec)
    rec = jnp.where(lane == 3, w2, rec)
    rec = jnp.where(lane == 4, rank1, rec)
    rec = jnp.where(lane == 5, rank2, rec)
    route_ref[rows] = rec


def _mix_kernel(x_ref, o0_ref, o1_ref, o2_ref, l0_ref, l1_ref, l2_ref, ob_ref, *refs):
    n_w = 12
    w_refs, order_refs = refs[:n_w], refs[n_w + 6:]
    x2_ref, hp_ref, route_ref, counts_ref, tri_ref, carry_ref = refs[n_w:n_w + 6]
    sub = tri_ref.shape[0]

    @pl.when((pl.program_id(0) == 0) & (pl.program_id(1) == 0))
    def _():
        row = lax.broadcasted_iota(jnp.int32, (sub, sub), 0)
        col = lax.broadcasted_iota(jnp.int32, (sub, sub), 1)
        tri_ref[...] = jnp.where(row > col, 1.0, 0.0).astype(BF16)
        carry_ref[...] = jnp.zeros_like(carry_ref)

    for t0 in range(0, x_ref.shape[0], sub):
        _mix_rows(t0, sub, x_ref, (o0_ref, o1_ref, o2_ref), (l0_ref, l1_ref, l2_ref), ob_ref, *w_refs,
                  x2_ref, hp_ref, route_ref, tri_ref, carry_ref, order_refs)
    counts_ref[...] = jnp.broadcast_to(carry_ref[...], counts_ref.shape)


def _mix(x, o_groups, lse_groups, o_b, kvm, wts):
    batch, seq, _ = x.shape
    T = batch * seq
    tm = MIX_TILE
    steps = seq // tm
    mem_tokens = kvm.shape[0] // batch
    row = lambda w: pl.BlockSpec((tm, w), lambda b, i: (b * steps + i, 0))
    by_class = [pl.BlockSpec((None, r, tm // r, GROUP_W), lambda b, i: (b, 0, i, 0)) for _, r in A_CONFIGS]
    in_specs = ([pl.BlockSpec((None, tm, D_MODEL), lambda b, i: (b, i, 0))] + by_class + by_class
                + [pl.BlockSpec((None, None, tm, GROUP_W), lambda b, i: (b, 0, i, 0)),
                   pl.BlockSpec((mem_tokens, 2 * X_WIDTH), lambda b, i: (b, 0))]
                + [_const_spec(w.shape) for w in wts])
    return pl.pallas_call(
        _mix_kernel,
        grid=(batch, steps),
        in_specs=in_specs,
        out_specs=[row(D_MODEL), row(D_MODEL // 2), row(ROUTE_W),
                   pl.BlockSpec((8, ROUTE_W), lambda b, i: (0, 0))],
        out_shape=[jax.ShapeDtypeStruct((T, D_MODEL), F32),
                   jax.ShapeDtypeStruct((T, D_MODEL // 2), jnp.uint32),
                   jax.ShapeDtypeStruct((T, ROUTE_W), F32),
                   jax.ShapeDtypeStruct((8, ROUTE_W), F32)],
        scratch_shapes=[pltpu.VMEM((MIX_SUB, MIX_SUB), BF16), pltpu.VMEM((1, ROUTE_W), F32)]
        + [pltpu.VMEM((GROUP_W // LANES, tm, LANES), F32)] * (2 * A_GROUPS),
        compiler_params=pltpu.CompilerParams(dimension_semantics=("arbitrary",) * 2,
                                             vmem_limit_bytes=VMEM_LIMIT),
        name="mix",
    )(x, *o_groups, *lse_groups, o_b, kvm, *wts)


def _pack_bf16_pairs(x):
    w = x.shape[1] // 2
    lo = lax.bitcast_convert_type(x[:, :w], jnp.uint32) >> 16
    hi = lax.bitcast_convert_type(x[:, w:], jnp.uint32) & jnp.uint32(0xFFFF0000)
    return lo | hi


def _unpack_bf16_pairs(u):
    lo = lax.bitcast_convert_type(u << 16, F32)
    hi = lax.bitcast_convert_type(u & jnp.uint32(0xFFFF0000), F32)
    return jnp.concatenate([lo, hi], axis=1)


def _dispatch_kernel(dest_ref, pad_ref, hp_ref, xs_hbm, zbuf, sems):
    tm = hp_ref.shape[0]
    n_pad = pad_ref.shape[-1]
    zbuf[...] = jnp.zeros_like(zbuf)

    def issue(t, c):
        for k in range(2):
            pltpu.make_async_copy(hp_ref.at[pl.ds(t, 1)], xs_hbm.at[pl.ds(dest_ref[0, k, t], 1)],
                                  sems.at[k]).start()
        return c

    lax.fori_loop(0, tm, issue, 0, unroll=ISSUE_UNROLL)

    def issue_pad(t, c):
        pltpu.make_async_copy(zbuf.at[pl.ds(0, 1)], xs_hbm.at[pl.ds(pad_ref[0, 0, t], 1)], sems.at[2]).start()
        return c

    lax.fori_loop(0, n_pad, issue_pad, 0, unroll=ISSUE_UNROLL)
    for k in range(2):
        pltpu.make_async_copy(hp_ref, xs_hbm.at[pl.ds(0, tm)], sems.at[k]).wait()
    pltpu.make_async_copy(hp_ref.at[pl.ds(0, n_pad)], xs_hbm.at[pl.ds(0, n_pad)], sems.at[2]).wait()


def _dispatch(h_packed, dest, pad_rows, n_rows):
    T, half_d = h_packed.shape
    tm = ROW_TILE
    n_steps = T // tm
    n_pad = pad_rows.shape[0] // n_steps
    dest3 = dest.reshape(2, n_steps, tm).transpose(1, 0, 2)
    pad3 = pad_rows.reshape(n_steps, 1, n_pad)
    return pl.pallas_call(
        _dispatch_kernel,
        grid=(n_steps,),
        in_specs=[pl.BlockSpec((1, 2, tm), lambda i: (i, 0, 0), memory_space=pltpu.SMEM),
                  pl.BlockSpec((1, 1, n_pad), lambda i: (i, 0, 0), memory_space=pltpu.SMEM),
                  pl.BlockSpec((tm, half_d), lambda i: (i, 0))],
        out_specs=pl.BlockSpec(memory_space=pl.ANY),
        out_shape=jax.ShapeDtypeStruct((n_rows, half_d), jnp.uint32),
        scratch_shapes=[pltpu.VMEM((8, half_d), jnp.uint32), pltpu.SemaphoreType.DMA((3,))],
        compiler_params=pltpu.CompilerParams(dimension_semantics=("arbitrary",)),
        name="dispatch",
    )(dest3, pad3, h_packed)


def _expert_kernel(be_ref, xs_ref, wgu_ref, wdn_ref, ys_ref):
    xb = _unpack_bf16_pairs(xs_ref[...]).astype(BF16)
    gu = jnp.dot(xb, wgu_ref[...], preferred_element_type=F32)
    gate, up = gu[:, :EXPERT_FF], gu[:, EXPERT_FF:]
    act = (gate * jax.nn.sigmoid(gate) * up).astype(BF16)
    y = jnp.dot(act, wdn_ref[...], preferred_element_type=F32)
    ys_ref[...] = _pack_bf16_pairs(y.astype(BF16).astype(F32))


def _experts(xs, block_e, w_gu, w_down):
    P, half_d = xs.shape
    n_blocks = P // MOE_BLOCK
    grid_spec = pltpu.PrefetchScalarGridSpec(
        num_scalar_prefetch=1,
        grid=(n_blocks,),
        in_specs=[pl.BlockSpec((MOE_BLOCK, half_d), lambda b, be: (b, 0)),
                  pl.BlockSpec((None, D_MODEL, 2 * EXPERT_FF), lambda b, be: (be[b], 0, 0)),
                  pl.BlockSpec((None, EXPERT_FF, D_MODEL), lambda b, be: (be[b], 0, 0))],
        out_specs=pl.BlockSpec((MOE_BLOCK, half_d), lambda b, be: (b, 0)),
    )
    return pl.pallas_call(
        _expert_kernel,
        grid_spec=grid_spec,
        out_shape=jax.ShapeDtypeStruct((P, half_d), jnp.uint32),
        compiler_params=pltpu.CompilerParams(dimension_semantics=("arbitrary",),
                                             vmem_limit_bytes=VMEM_LIMIT),
        name="experts",
    )(block_e, xs, w_gu, w_down)


def _combine_kernel(dest_ref, x2_ref, route_ref, g_ref, ys_hbm, out_ref, buf, sems):
    tm = x2_ref.shape[0]

    def issue(t, c):
        for k in range(2):
            pltpu.make_async_copy(ys_hbm.at[pl.ds(dest_ref[0, k, t], 1)], buf.at[k, pl.ds(t, 1)],
                                  sems.at[k]).start()
        return c

    lax.fori_loop(0, tm, issue, 0, unroll=ISSUE_UNROLL)
    for k in range(2):
        pltpu.make_async_copy(ys_hbm.at[pl.ds(0, tm)], buf.at[k], sems.at[k]).wait()
    route = route_ref[...]
    moe = _unpack_bf16_pairs(buf[0]) * route[:, 2:3] + _unpack_bf16_pairs(buf[1]) * route[:, 3:4]
    out_ref[...] = _rms(x2_ref[...] + moe, g_ref[...])


def _combine(x2, route, g_final, dest, ys):
    T = x2.shape[0]
    tm = ROW_TILE
    n_steps = T // tm
    dest3 = dest.reshape(2, n_steps, tm).transpose(1, 0, 2)
    return pl.pallas_call(
        _combine_kernel,
        grid=(n_steps,),
        in_specs=[pl.BlockSpec((1, 2, tm), lambda i: (i, 0, 0), memory_space=pltpu.SMEM),
                  pl.BlockSpec((tm, D_MODEL), lambda i: (i, 0)),
                  pl.BlockSpec((tm, ROUTE_W), lambda i: (i, 0)),
                  _const_spec((1, D_MODEL)),
                  pl.BlockSpec(memory_space=pl.ANY)],
        out_specs=pl.BlockSpec((tm, D_MODEL), lambda i: (i, 0)),
        out_shape=jax.ShapeDtypeStruct((T, D_MODEL), F32),
        scratch_shapes=[pltpu.VMEM((2, tm, D_MODEL // 2), jnp.uint32), pltpu.SemaphoreType.DMA((2,))],
        compiler_params=pltpu.CompilerParams(dimension_semantics=("arbitrary",)),
        name="combine",
    )(dest3, x2, route, g_final, ys)


def _routing_tables(route, counts_rec, n_tokens):
    n_slots = 2 * n_tokens
    n_blocks = n_slots // MOE_BLOCK + N_EXPERTS
    n_rows = n_blocks * MOE_BLOCK
    n_pad = n_rows - n_slots
    counts = counts_rec[0, EXPERT_LANE0:EXPERT_LANE0 + N_EXPERTS].astype(jnp.int32)
    padded = (counts + MOE_BLOCK - 1) // MOE_BLOCK * MOE_BLOCK
    pends = jnp.cumsum(padded)
    pstarts = pends - padded
    expert = route[:, 0:2].astype(jnp.int32)
    rank = route[:, 4:6].astype(jnp.int32)
    lookup = lambda table, idx: jnp.sum(jnp.where(idx[..., None] == jnp.arange(N_EXPERTS), table, 0), axis=-1)
    count_le = lambda sorted_vals, q: jnp.sum(sorted_vals[None, :] <= q[:, None], axis=-1).astype(jnp.int32)
    dest = (lookup(pstarts, expert) + rank).T
    block_e = jnp.minimum(count_le(pends, jnp.arange(n_blocks, dtype=jnp.int32) * MOE_BLOCK), N_EXPERTS - 1)
    npad_e = padded - counts
    cum = jnp.cumsum(npad_e)
    idx = jnp.arange(n_pad, dtype=jnp.int32)
    e_of = count_le(cum, idx)
    in_expert = lookup(pstarts + counts - (cum - npad_e), e_of) + idx
    pad_rows = jnp.where(e_of < N_EXPERTS, in_expert, pends[-1] + (idx - cum[-1])).astype(jnp.int32)
    return dest.astype(jnp.int32), block_e, pad_rows, n_rows


def _encoder_group(x, mem, w):
    batch, seq, _ = x.shape
    T = batch * seq
    *qkv_groups, qkv_win = _proj(x, w["g_mix"], w["w_in"])

    slopes_a = _alibi_slopes(A_HEADS).reshape(A_GROUPS, A_HEADS_PER_GROUP)
    o_groups, lse_groups = [], []
    for g, (window, r) in enumerate(A_CONFIGS):
        o, lse = _banded_attention(qkv_groups[g], half_w=window // (2 * r), kv_width=GROUP_W,
                                   slopes=slopes_a[g] * np.float32(r))
        o_groups.append(o)
        lse_groups.append(lse)
    (o_b,) = _banded_attention(qkv_win.reshape(batch, 1, seq, WIN_W), half_w=B_HALF_WINDOW,
                               kv_width=KV2_W, slopes=_alibi_slopes(B_HEADS), sink=w["sink_b"],
                               want_lse=False)

    kvm = _memkv(mem.reshape(-1, D_MODEL), w["g_mem"], w["w_ckv"])
    mix_w = [w[k] for k in ("g_mix", "w_gate", "b_gate", "w_branch", "w_out", "g_xattn", "w_cq",
                            "w_co", "g_ffn", "w_route", "b_route")]
    x2, h_packed, route, counts_rec = _mix(x, o_groups, lse_groups, o_b, kvm, mix_w)

    dest, block_e, pad_rows, n_rows = _routing_tables(route, counts_rec, T)
    xs = _dispatch(h_packed, dest, pad_rows, n_rows)
    ys = _experts(xs, block_e, w["w_gu"], w["w_down"])
    y = _combine(x2, route, w["g_final"], dest, ys)
    return y.reshape(batch, seq, D_MODEL)


def _prep_weights(g_mix, w_in, sink_b, w_gate, b_gate, w_branch, w_out, g_xattn, g_mem, w_cq, w_ckv,
                  w_co, g_ffn, w_rg, b_rg, w_re, b_re, w_gu, w_down, g_final):
    scale = HEAD_DIM ** -0.5
    aw = A_WIDTH
    qa, ka, va = w_in[:, :aw] * scale, w_in[:, aw:2 * aw], w_in[:, 2 * aw:3 * aw]
    qb = w_in[:, 3 * aw:3 * aw + B_Q] * scale
    kb = w_in[:, 3 * aw + B_Q:3 * aw + B_Q + B_KV]
    vb = w_in[:, 3 * aw + B_Q + B_KV:]
    twice = lambda t: jnp.repeat(t.reshape(D_MODEL, B_KV_HEADS, 1, HEAD_DIM), 2, axis=2).reshape(D_MODEL, KV2_W)
    group = lambda t, g: t[:, g * GROUP_W:(g + 1) * GROUP_W]
    cols = [group(t, g) for g in range(A_GROUPS) for t in (qa, ka, va)] + [qb, twice(kb), twice(vb)]
    w_in_x = jnp.concatenate(cols, axis=1).astype(BF16)
    w_route = jnp.zeros((D_MODEL, ROUTE_W), F32)
    w_route = w_route.at[:, :N_GROUPS].set(w_rg).at[:, EXPERT_LANE0:EXPERT_LANE0 + N_EXPERTS].set(w_re)
    b_route = jnp.zeros((1, ROUTE_W), F32)
    b_route = b_route.at[0, :N_GROUPS].set(b_rg).at[0, EXPERT_LANE0:EXPERT_LANE0 + N_EXPERTS].set(b_re)
    w_route_hi = w_route.astype(BF16)
    w_route = jnp.concatenate([w_route_hi, (w_route - w_route_hi.astype(F32)).astype(BF16)], axis=1)
    vec = lambda v: v.reshape(1, -1).astype(F32)
    return dict(
        g_mix=vec(g_mix), w_in=w_in_x, sink_b=sink_b.astype(F32),
        w_gate=w_gate.astype(BF16), b_gate=vec(b_gate), w_branch=w_branch.astype(BF16),
        w_out=w_out.astype(BF16), g_xattn=vec(g_xattn), g_mem=vec(g_mem), w_cq=w_cq.astype(BF16),
        w_ckv=w_ckv.astype(BF16), w_co=w_co.astype(BF16), g_ffn=vec(g_ffn),
        w_route=w_route, b_route=b_route, w_gu=w_gu.astype(BF16), w_down=w_down.astype(BF16),
        g_final=vec(g_final))


def kernel(x_prompt, x_sample, mem_prompt, mem_sample, g_mix, w_in, sink_b, w_gate, b_gate, w_branch,
           w_out, g_xattn, g_mem, w_cq, w_ckv, w_co, g_ffn, w_rg, b_rg, w_re, b_re, w_gu, w_down,
           g_final):
    assert g_mix.shape[0] == 1, "single-layer encoder"
    w = _prep_weights(g_mix[0], w_in[0], sink_b[0], w_gate[0], b_gate[0], w_branch[0], w_out[0],
                      g_xattn[0], g_mem[0], w_cq[0], w_ckv[0], w_co[0], g_ffn[0], w_rg[0], b_rg[0],
                      w_re[0], b_re[0], w_gu[0], w_down[0], g_final)
    return (_encoder_group(x_prompt, mem_prompt, w), _encoder_group(x_sample, mem_sample, w))
```

```python
import functools

import numpy as np
import jax
import jax.numpy as jnp
from jax import lax
from jax.experimental import pallas as pl
from jax.experimental.pallas import tpu as pltpu
from jax.experimental.pallas import tpu_sc as plsc

F32 = jnp.float32
BF16 = jnp.bfloat16

D_MODEL = 1024
HEAD_DIM = 64
A_CONFIGS = ((128, 1), (512, 4), (2048, 16))
A_GROUPS = 3
A_HEADS_PER_GROUP = 8
A_HEADS = A_GROUPS * A_HEADS_PER_GROUP
A_WIDTH = A_HEADS * HEAD_DIM
B_HEADS = 8
B_KV_HEADS = 2
B_HALF_WINDOW = 128
B_Q = B_HEADS * HEAD_DIM
B_KV = B_KV_HEADS * HEAD_DIM
X_HEADS = 4
X_HEAD_DIM = 128
X_WIDTH = X_HEADS * X_HEAD_DIM
N_GROUPS = 4
EXPERTS_PER_GROUP = 8
N_EXPERTS = N_GROUPS * EXPERTS_PER_GROUP
EXPERT_FF = 512
MOE_BLOCK = 256
EPS = 1e-6
NEG = -1e30

LANES = 128
GROUP_W = A_HEADS_PER_GROUP * HEAD_DIM
PAIR_W = 2 * HEAD_DIM
N_PAIRS = GROUP_W // PAIR_W
QKV_W = 3 * GROUP_W
KV2_W = 2 * B_KV
WIN_W = B_Q + 2 * KV2_W
PROJ_W = A_GROUPS * QKV_W + WIN_W
ROUTE_W = LANES
EXPERT_LANE0 = 32
VMEM_LIMIT = 52 * 1024 * 1024

ROW_TILE = 256
PROJ_TILE = 512
MIX_TILE = 512
MIX_SUB = 256
Q_STEPS = 4
Q_UNROLL = 4
GATHER_WINDOW = 128
Y_SLABS = 2
ISSUE_UNROLL = 8

def _rms(x, g):
    ms = jnp.mean(x * x, axis=-1, keepdims=True)
    return x * lax.rsqrt(ms + EPS) * g


def _alibi_slopes(n):
    return 2.0 ** (-8.0 * np.arange(1, n + 1, dtype=np.float32) / n)


def _const_spec(shape):
    return pl.BlockSpec(shape, lambda *_: (0,) * len(shape), pipeline_mode=pl.Buffered(1))


def _proj_kernel(x_ref, g_ref, w_ref, *refs):
    o_refs, win_ref, h_ref = refs[:A_GROUPS], refs[A_GROUPS], refs[A_GROUPS + 1]
    tm = x_ref.shape[0]
    h32 = _rms(x_ref[...], g_ref[...])
    h_nat = h32.astype(BF16)
    n_slabs = h_ref.shape[0]
    for s in range(n_slabs):
        h_ref[s] = h32[:, s * LANES:(s + 1) * LANES]
    for g, (_, r) in enumerate(A_CONFIGS):
        n = tm // r
        if r == 1:
            h = h_nat
        else:
            h = jnp.concatenate(
                [jnp.concatenate([h_ref[s, pl.ds(c, n, stride=r), :] for c in range(r)], axis=0)
                 for s in range(n_slabs)], axis=1).astype(BF16)
        for j in range(3):
            cols = slice(g * QKV_W + j * GROUP_W, g * QKV_W + (j + 1) * GROUP_W)
            res = jnp.dot(h, w_ref[:, cols], preferred_element_type=F32).astype(BF16)
            for c in range(r):
                o_refs[g][c, :, j * GROUP_W:(j + 1) * GROUP_W] = res[c * n:(c + 1) * n]
    for j in range(WIN_W // GROUP_W):
        cols = slice(A_GROUPS * QKV_W + j * GROUP_W, A_GROUPS * QKV_W + (j + 1) * GROUP_W)
        win_ref[:, j * GROUP_W:(j + 1) * GROUP_W] = jnp.dot(
            h_nat, w_ref[:, cols], preferred_element_type=F32).astype(BF16)


def _proj(x, g, w):
    batch, seq, _ = x.shape
    tm = PROJ_TILE
    out_specs = [pl.BlockSpec((None, r, tm // r, QKV_W), lambda b, i: (b, 0, i, 0)) for _, r in A_CONFIGS]
    out_shape = [jax.ShapeDtypeStruct((batch, r, seq // r, QKV_W), BF16) for _, r in A_CONFIGS]
    out_specs.append(pl.BlockSpec((None, tm, WIN_W), lambda b, i: (b, i, 0)))
    out_shape.append(jax.ShapeDtypeStruct((batch, seq, WIN_W), BF16))
    return pl.pallas_call(
        _proj_kernel,
        grid=(batch, seq // tm),
        in_specs=[pl.BlockSpec((None, tm, D_MODEL), lambda b, i: (b, i, 0)),
                  _const_spec((1, D_MODEL)),
                  _const_spec((D_MODEL, PROJ_W))],
        out_specs=out_specs,
        out_shape=out_shape,
        scratch_shapes=[pltpu.VMEM((D_MODEL // LANES, tm, LANES), F32)],
        compiler_params=pltpu.CompilerParams(dimension_semantics=("arbitrary",) * 2,
                                             vmem_limit_bytes=VMEM_LIMIT),
        name="proj",
    )(x, g, w)


def _attn_kernel(*refs, qb, kb, q_steps, m_len, half_w, offsets, kv_shared, has_sink, want_lse):
    refs = list(refs)
    sink_ref = refs.pop(0) if has_sink else None
    bias_ref, q_ref, k_ref, v_ref, o_ref = refs[:5]
    lse_ref = refs[5] if want_lse else None
    lo_q = lax.broadcasted_iota(jnp.int32, (qb, PAIR_W), 1) < HEAD_DIM
    first_head = lax.broadcasted_iota(jnp.int32, (2 * qb, 1), 0) < qb
    zeros_q = jnp.zeros((qb, PAIR_W), BF16)

    def q_block(it, carry):
        cc, qi = it // q_steps, it % q_steps
        gi = pl.program_id(2) * q_steps + qi
        ks = pl.multiple_of(jnp.clip(gi * qb - half_w, 0, m_len - kb), 16)
        off = gi * qb - ks
        var = sum(jnp.where(off == o, n, 0) for n, o in enumerate(offsets))
        rows = pl.ds(pl.multiple_of(qi * qb, qb), qb)
        for j in range(N_PAIRS):
            cols = slice(j * PAIR_W, (j + 1) * PAIR_W)
            jc = (j // 2) if kv_shared else j
            kcols = slice(jc * PAIR_W, (jc + 1) * PAIR_W)
            qp = q_ref[cc, rows, cols]
            q_st = jnp.concatenate([jnp.where(lo_q, qp, zeros_q), jnp.where(lo_q, zeros_q, qp)], axis=0)
            s = lax.dot_general(q_st, k_ref[cc, pl.ds(ks, kb), kcols], (((1,), (1,)), ((), ())),
                                preferred_element_type=F32) + bias_ref[var, j]
            m = jnp.max(s, axis=-1, keepdims=True)
            if has_sink:
                sk = jnp.where(first_head, sink_ref[2 * j], sink_ref[2 * j + 1])
                m = jnp.maximum(m, sk)
            e = jnp.exp(s - m)
            den = jnp.sum(e, axis=-1, keepdims=True)
            if has_sink:
                den = den + jnp.exp(sk - m)
            o2 = jnp.dot(e.astype(BF16), v_ref[cc, pl.ds(ks, kb), kcols], preferred_element_type=F32) / den
            o_ref[cc, rows, cols] = jnp.where(lo_q, o2[:qb], o2[qb:]).astype(BF16)
            if want_lse:
                l2 = m + jnp.log(den)
                lse_ref[cc, rows, cols] = jnp.where(lo_q, l2[:qb], l2[qb:])
        return carry

    n_iter = q_ref.shape[0] * q_steps
    lax.fori_loop(0, n_iter, q_block, 0, unroll=min(Q_UNROLL, n_iter))


def _bias_tables(offsets, slopes, qb, kb, half_w):
    rel = jnp.arange(qb, dtype=jnp.int32)[:, None] - jnp.arange(kb, dtype=jnp.int32)[None, :]
    dist = jnp.abs(rel[None] + jnp.asarray(offsets, jnp.int32)[:, None, None])
    bias = -jnp.asarray(slopes, F32)[None, :, None, None] * dist.astype(F32)[:, None]
    bias = jnp.where((dist <= half_w)[:, None], bias, NEG)
    return bias.reshape(len(offsets), N_PAIRS, 2 * qb, kb)


def _banded_attention(qkv, *, half_w, kv_width, slopes, sink=None, want_lse=True):
    batch, r, m_len, _ = qkv.shape
    qb = min(2 * half_w, m_len)
    kb = min(qb + 2 * half_w, m_len)
    nq = m_len // qb
    q_steps = min(Q_STEPS, nq)
    cb = min(r, Q_STEPS // q_steps)
    kv_shared = kv_width != GROUP_W
    has_sink = sink is not None
    k_block = GROUP_W // kv_width
    offsets = sorted({i * qb - min(max(i * qb - half_w, 0), m_len - kb) for i in range(nq)})
    bias = _bias_tables(offsets, slopes, qb, kb, half_w)

    in_specs = []
    args = []
    if has_sink:
        in_specs.append(pl.BlockSpec(memory_space=pltpu.SMEM))
        args.append(sink)
    in_specs += [
        _const_spec(bias.shape),
        pl.BlockSpec((None, cb, q_steps * qb, GROUP_W), lambda b, c, i: (b, c, i, 0)),
        pl.BlockSpec((None, cb, m_len, kv_width), lambda b, c, i: (b, c, 0, k_block)),
        pl.BlockSpec((None, cb, m_len, kv_width), lambda b, c, i: (b, c, 0, k_block + 1)),
    ]
    args += [bias, qkv, qkv, qkv]
    out_spec = pl.BlockSpec((None, cb, q_steps * qb, GROUP_W), lambda b, c, i: (b, c, i, 0))
    out_specs = [out_spec]
    out_shape = [jax.ShapeDtypeStruct((batch, r, m_len, GROUP_W), BF16)]
    if want_lse:
        out_specs.append(out_spec)
        out_shape.append(jax.ShapeDtypeStruct((batch, r, m_len, GROUP_W), F32))
    kern = functools.partial(_attn_kernel, qb=qb, kb=kb, q_steps=q_steps, m_len=m_len, half_w=half_w,
                             offsets=tuple(offsets), kv_shared=kv_shared, has_sink=has_sink,
                             want_lse=want_lse)
    return pl.pallas_call(
        kern,
        grid=(batch, r // cb, nq // q_steps),
        in_specs=in_specs,
        out_specs=out_specs,
        out_shape=out_shape,
        compiler_params=pltpu.CompilerParams(dimension_semantics=("arbitrary",) * 3,
                                             vmem_limit_bytes=VMEM_LIMIT),
        name=f"attn_r{r}_w{half_w}",
    )(*args)


def _memkv_kernel(m_ref, g_ref, w_ref, o_ref):
    h = _rms(m_ref[...], g_ref[...]).astype(BF16)
    o_ref[...] = jnp.dot(h, w_ref[...], preferred_element_type=F32).astype(BF16)


def _memkv(mem2d, g, w):
    R = mem2d.shape[0]
    return pl.pallas_call(
        _memkv_kernel,
        grid=(R // ROW_TILE,),
        in_specs=[pl.BlockSpec((ROW_TILE, D_MODEL), lambda i: (i, 0)),
                  _const_spec((1, D_MODEL)),
                  _const_spec((D_MODEL, 2 * X_WIDTH))],
        out_specs=pl.BlockSpec((ROW_TILE, 2 * X_WIDTH), lambda i: (i, 0)),
        out_shape=jax.ShapeDtypeStruct((R, 2 * X_WIDTH), BF16),
        compiler_params=pltpu.CompilerParams(dimension_semantics=("arbitrary",)),
        name="memkv",
    )(mem2d, g, w)


def _token_order(src_ref, dst_ref, m0, n):
    r = src_ref.shape[0]
    if r == 1:
        return src_ref[0, m0:m0 + n].astype(F32)
    n_slabs = dst_ref.shape[0]
    for c in range(r):
        rows = src_ref[c, m0:m0 + n].astype(F32)
        for s in range(n_slabs):
            dst_ref[s, pl.ds(m0 * r + c, n, stride=r), :] = rows[:, s * LANES:(s + 1) * LANES]
    return jnp.concatenate([dst_ref[s, m0 * r:(m0 + n) * r] for s in range(n_slabs)], axis=1)


def _mix_rows(t0, tm, x_ref, o_refs, l_refs, ob_ref, kv_ref,
              g_mix_ref, w_gate_ref, b_gate_ref, w_br_ref, w_out_ref,
              g_x_ref, w_cq_ref, w_co_ref, g_ffn_ref, w_rt_ref, b_rt_ref,
              x2_ref, hp_ref, route_ref, tri_ref, carry_ref, order_refs):
    rows = slice(t0, t0 + tm)
    x = x_ref[rows]
    h1 = _rms(x, g_mix_ref[...]).astype(BF16)

    by_class = lambda ref, scratch: _token_order(ref, scratch, t0 // ref.shape[0], tm // ref.shape[0])
    l0, l1, l2 = (by_class(l, s) for l, s in zip(l_refs, order_refs[:3]))
    lm = jnp.maximum(jnp.maximum(l0, l1), l2)
    e0, e1, e2 = jnp.exp(l0 - lm), jnp.exp(l1 - lm), jnp.exp(l2 - lm)
    den = e0 + e1 + e2
    o0, o1, o2 = (by_class(o, s) for o, s in zip(o_refs, order_refs[3:]))
    oa = (e0 / den) * o0 + (e1 / den) * o1 + (e2 / den) * o2
    br_a = jnp.dot(oa.astype(BF16), w_br_ref[:GROUP_W, :], preferred_element_type=F32)
    br_b = jnp.dot(ob_ref[rows], w_br_ref[GROUP_W:, :], preferred_element_type=F32)
    ga = jax.nn.sigmoid(jnp.dot(h1, w_gate_ref[:, :D_MODEL], preferred_element_type=F32)
                        + b_gate_ref[:, :D_MODEL])
    merged = ga * br_a
    gb = jax.nn.sigmoid(jnp.dot(h1, w_gate_ref[:, D_MODEL:], preferred_element_type=F32)
                        + b_gate_ref[:, D_MODEL:])
    merged = merged + gb * br_b
    x1 = x + jnp.dot(merged.astype(BF16), w_out_ref[...], preferred_element_type=F32)

    h2 = _rms(x1, g_x_ref[...]).astype(BF16)
    q = jnp.dot(h2, w_cq_ref[...], preferred_element_type=F32).astype(BF16)
    heads = []
    for h in range(X_HEADS):
        cols = slice(h * X_HEAD_DIM, (h + 1) * X_HEAD_DIM)
        kh = kv_ref[:, cols]
        vh = kv_ref[:, X_WIDTH + h * X_HEAD_DIM:X_WIDTH + (h + 1) * X_HEAD_DIM]
        s = lax.dot_general(q[:, cols], kh, (((1,), (1,)), ((), ())),
                            preferred_element_type=F32) * (X_HEAD_DIM ** -0.5)
        m = jnp.max(s, axis=-1, keepdims=True)
        e = jnp.exp(s - m)
        p = e / jnp.sum(e, axis=-1, keepdims=True)
        heads.append(jnp.dot(p.astype(BF16), vh, preferred_element_type=F32))
    o = jnp.concatenate(heads, axis=1).astype(BF16)
    x2 = x1 + jnp.dot(o, w_co_ref[...], preferred_element_type=F32)
    x2_ref[rows] = x2

    h3 = _rms(x2, g_ffn_ref[...])
    h3_hi = h3.astype(BF16)
    hp_ref[rows] = _pack_bf16_pairs(h3_hi.astype(F32))
    h3_lo = (h3 - h3_hi.astype(F32)).astype(BF16)
    by_hi = jnp.dot(h3_hi, w_rt_ref[...], preferred_element_type=F32)
    logits = (by_hi[:, :ROUTE_W] + by_hi[:, ROUTE_W:]
              + jnp.dot(h3_lo, w_rt_ref[:, :ROUTE_W], preferred_element_type=F32)) + b_rt_ref[...]
    lane = lax.broadcasted_iota(jnp.int32, (tm, ROUTE_W), 1)
    gmask = lane < N_GROUPS
    gl = jnp.where(gmask, logits, -jnp.inf)
    gmax = jnp.max(gl, axis=-1, keepdims=True)
    gidx = jnp.min(jnp.where(gl == gmax, lane, ROUTE_W), axis=-1, keepdims=True)
    pg_sel = 1.0 / jnp.sum(jnp.where(gmask, jnp.exp(logits - gmax), 0.0), axis=-1, keepdims=True)
    e_lo = EXPERT_LANE0 + gidx * EXPERTS_PER_GROUP
    emask = (lane >= e_lo) & (lane < e_lo + EXPERTS_PER_GROUP)
    el = jnp.where(emask, logits, -jnp.inf)
    emax1 = jnp.max(el, axis=-1, keepdims=True)
    i1 = jnp.min(jnp.where(el == emax1, lane, ROUTE_W), axis=-1, keepdims=True)
    el2 = jnp.where(lane == i1, -jnp.inf, el)
    emax2 = jnp.max(el2, axis=-1, keepdims=True)
    i2 = jnp.min(jnp.where(el2 == emax2, lane, ROUTE_W), axis=-1, keepdims=True)
    t2 = jnp.exp(emax2 - emax1)
    w1 = pg_sel / (1.0 + t2)
    w2 = pg_sel * t2 / (1.0 + t2)

    oh1 = lane == i1
    oh2 = lane == i2
    ohs = jnp.where(oh1 | oh2, 1.0, 0.0)
    before = jnp.dot(tri_ref[...], ohs.astype(BF16), preferred_element_type=F32) + carry_ref[...]
    rank1 = jnp.sum(jnp.where(oh1, before, 0.0), axis=-1, keepdims=True)
    rank2 = jnp.sum(jnp.where(oh2, before, 0.0), axis=-1, keepdims=True)
    carry_ref[...] = carry_ref[...] + jnp.sum(ohs, axis=0, keepdims=True)

    rec = jnp.where(lane == 0, (i1 - EXPERT_LANE0).astype(F32), 0.0)
    rec = jnp.where(lane == 1, (i2 - EXPERT_LANE0).astype(F32), rec)
    rec = jnp.where(lane == 2, w1, rec)
    rec = jnp.where(lane == 3, w2, rec)
    rec = jnp.where(lane == 4, rank1, rec)
    rec = jnp.where(lane == 5, rank2, rec)
    route_ref[rows] = rec


def _mix_kernel(x_ref, o0_ref, o1_ref, o2_ref, l0_ref, l1_ref, l2_ref, ob_ref, *refs):
    n_w = 12
    w_refs, order_refs = refs[:n_w], refs[n_w + 6:]
    x2_ref, hp_ref, route_ref, counts_ref, tri_ref, carry_ref = refs[n_w:n_w + 6]
    sub = tri_ref.shape[0]

    @pl.when((pl.program_id(0) == 0) & (pl.program_id(1) == 0))
    def _():
        row = lax.broadcasted_iota(jnp.int32, (sub, sub), 0)
        col = lax.broadcasted_iota(jnp.int32, (sub, sub), 1)
        tri_ref[...] = jnp.where(row > col, 1.0, 0.0).astype(BF16)
        carry_ref[...] = jnp.zeros_like(carry_ref)

    for t0 in range(0, x_ref.shape[0], sub):
        _mix_rows(t0, sub, x_ref, (o0_ref, o1_ref, o2_ref), (l0_ref, l1_ref, l2_ref), ob_ref, *w_refs,
                  x2_ref, hp_ref, route_ref, tri_ref, carry_ref, order_refs)
    counts_ref[...] = jnp.broadcast_to(carry_ref[...], counts_ref.shape)


def _mix(x, o_groups, lse_groups, o_b, kvm, wts):
    batch, seq, _ = x.shape
    T = batch * seq
    tm = MIX_TILE
    steps = seq // tm
    mem_tokens = kvm.shape[0] // batch
    row = lambda w: pl.BlockSpec((tm, w), lambda b, i: (b * steps + i, 0))
    by_class = [pl.BlockSpec((None, r, tm // r, GROUP_W), lambda b, i: (b, 0, i, 0)) for _, r in A_CONFIGS]
    in_specs = ([pl.BlockSpec((None, tm, D_MODEL), lambda b, i: (b, i, 0))] + by_class + by_class
                + [pl.BlockSpec((None, None, tm, GROUP_W), lambda b, i: (b, 0, i, 0)),
                   pl.BlockSpec((mem_tokens, 2 * X_WIDTH), lambda b, i: (b, 0))]
                + [_const_spec(w.shape) for w in wts])
    return pl.pallas_call(
        _mix_kernel,
        grid=(batch, steps),
        in_specs=in_specs,
        out_specs=[row(D_MODEL), row(D_MODEL // 2), row(ROUTE_W),
                   pl.BlockSpec((8, ROUTE_W), lambda b, i: (0, 0))],
        out_shape=[jax.ShapeDtypeStruct((T, D_MODEL), F32),
                   jax.ShapeDtypeStruct((T, D_MODEL // 2), jnp.uint32),
                   jax.ShapeDtypeStruct((T, ROUTE_W), F32),
                   jax.ShapeDtypeStruct((8, ROUTE_W), F32)],
        scratch_shapes=[pltpu.VMEM((MIX_SUB, MIX_SUB), BF16), pltpu.VMEM((1, ROUTE_W), F32)]
        + [pltpu.VMEM((GROUP_W // LANES, tm, LANES), F32)] * (2 * A_GROUPS),
        compiler_params=pltpu.CompilerParams(dimension_semantics=("arbitrary",) * 2,
                                             vmem_limit_bytes=VMEM_LIMIT),
        name="mix",
    )(x, *o_groups, *lse_groups, o_b, kvm, *wts)


def _pack_bf16_pairs(x):
    w = x.shape[1] // 2
    lo = lax.bitcast_convert_type(x[:, :w], jnp.uint32) >> 16
    hi = lax.bitcast_convert_type(x[:, w:], jnp.uint32) & jnp.uint32(0xFFFF0000)
    return lo | hi


def _unpack_bf16_pairs(u):
    lo = lax.bitcast_convert_type(u << 16, F32)
    hi = lax.bitcast_convert_type(u & jnp.uint32(0xFFFF0000), F32)
    return jnp.concatenate([lo, hi], axis=1)


def _dispatch_kernel(dest_ref, pad_ref, hp_ref, xs_hbm, zbuf, sems):
    tm = hp_ref.shape[0]
    n_pad = pad_ref.shape[-1]
    zbuf[...] = jnp.zeros_like(zbuf)

    def issue(t, c):
        for k in range(2):
            pltpu.make_async_copy(hp_ref.at[pl.ds(t, 1)], xs_hbm.at[pl.ds(dest_ref[0, k, t], 1)],
                                  sems.at[k]).start()
        return c

    lax.fori_loop(0, tm, issue, 0, unroll=ISSUE_UNROLL)

    def issue_pad(t, c):
        pltpu.make_async_copy(zbuf.at[pl.ds(0, 1)], xs_hbm.at[pl.ds(pad_ref[0, 0, t], 1)], sems.at[2]).start()
        return c

    lax.fori_loop(0, n_pad, issue_pad, 0, unroll=ISSUE_UNROLL)
    for k in range(2):
        pltpu.make_async_copy(hp_ref, xs_hbm.at[pl.ds(0, tm)], sems.at[k]).wait()
    pltpu.make_async_copy(hp_ref.at[pl.ds(0, n_pad)], xs_hbm.at[pl.ds(0, n_pad)], sems.at[2]).wait()


def _dispatch(h_packed, dest, pad_rows, n_rows):
    T, half_d = h_packed.shape
    tm = ROW_TILE
    n_steps = T // tm
    n_pad = pad_rows.shape[0] // n_steps
    dest3 = dest.reshape(2, n_steps, tm).transpose(1, 0, 2)
    pad3 = pad_rows.reshape(n_steps, 1, n_pad)
    return pl.pallas_call(
        _dispatch_kernel,
        grid=(n_steps,),
        in_specs=[pl.BlockSpec((1, 2, tm), lambda i: (i, 0, 0), memory_space=pltpu.SMEM),
                  pl.BlockSpec((1, 1, n_pad), lambda i: (i, 0, 0), memory_space=pltpu.SMEM),
                  pl.BlockSpec((tm, half_d), lambda i: (i, 0))],
        out_specs=pl.BlockSpec(memory_space=pl.ANY),
        out_shape=jax.ShapeDtypeStruct((n_rows, half_d), jnp.uint32),
        scratch_shapes=[pltpu.VMEM((8, half_d), jnp.uint32), pltpu.SemaphoreType.DMA((3,))],
        compiler_params=pltpu.CompilerParams(dimension_semantics=("arbitrary",)),
        name="dispatch",
    )(dest3, pad3, h_packed)


def _expert_kernel(be_ref, xs_ref, wgu_ref, wdn_ref, *ys_refs):
    xb = _unpack_bf16_pairs(xs_ref[...]).astype(BF16)
    gu = jnp.dot(xb, wgu_ref[...], preferred_element_type=F32)
    gate, up = gu[:, :EXPERT_FF], gu[:, EXPERT_FF:]
    act = (gate * jax.nn.sigmoid(gate) * up).astype(BF16)
    y = jnp.dot(act, wdn_ref[...], preferred_element_type=F32)
    packed = lax.bitcast_convert_type(_pack_bf16_pairs(y.astype(BF16).astype(F32)), jnp.int32)
    w = packed.shape[1] // len(ys_refs)
    for n, ys_ref in enumerate(ys_refs):
        ys_ref[...] = packed[:, n * w:(n + 1) * w]


def _experts(xs, block_e, w_gu, w_down):
    P, half_d = xs.shape
    n_blocks = P // MOE_BLOCK
    slab_w = half_d // Y_SLABS
    grid_spec = pltpu.PrefetchScalarGridSpec(
        num_scalar_prefetch=1,
        grid=(n_blocks,),
        in_specs=[pl.BlockSpec((MOE_BLOCK, half_d), lambda b, be: (b, 0)),
                  pl.BlockSpec((None, D_MODEL, 2 * EXPERT_FF), lambda b, be: (be[b], 0, 0)),
                  pl.BlockSpec((None, EXPERT_FF, D_MODEL), lambda b, be: (be[b], 0, 0))],
        out_specs=[pl.BlockSpec((MOE_BLOCK, slab_w), lambda b, be: (b, 0))] * Y_SLABS,
    )
    return pl.pallas_call(
        _expert_kernel,
        grid_spec=grid_spec,
        out_shape=[jax.ShapeDtypeStruct((P, slab_w), jnp.int32)] * Y_SLABS,
        compiler_params=pltpu.CompilerParams(dimension_semantics=("arbitrary",),
                                             vmem_limit_bytes=VMEM_LIMIT),
        name="experts",
    )(block_e, xs, w_gu, w_down)


def _gather_rows(table, indices):
    n, width = indices.shape[0], table.shape[1]
    mesh = plsc.VectorSubcoreMesh(core_axis_name="core", subcore_axis_name="subcore")

    @pl.kernel(out_type=jax.ShapeDtypeStruct((n, width), table.dtype), mesh=mesh)
    def gather_kernel(table_hbm, idx_hbm, out_hbm):
        def body(idx_vmem, out_vmem):
            pltpu.sync_copy(table_hbm.at[idx_vmem.at[0]], out_vmem)

        pltpu.emit_pipeline(
            body,
            grid=(n // GATHER_WINDOW,),
            in_specs=[pl.BlockSpec((1, GATHER_WINDOW), index_map=lambda i: (0, i))],
            out_specs=[pl.BlockSpec((GATHER_WINDOW, width), index_map=lambda i: (i, 0))],
            core_axis_name="subcore",
            dimension_semantics=(pltpu.PARALLEL,),
        )(idx_hbm, out_hbm)

    return gather_kernel(table, indices.reshape(1, n))


def _combine_kernel(x2_ref, route_ref, g_ref, *refs):
    y_refs, out_ref = refs[:-1], refs[-1]
    route = route_ref[...]

    def expert_rows(slabs):
        packed = jnp.concatenate([s[...] for s in slabs], axis=1)
        return _unpack_bf16_pairs(lax.bitcast_convert_type(packed, jnp.uint32))

    moe = expert_rows(y_refs[:Y_SLABS]) * route[:, 2:3] + expert_rows(y_refs[Y_SLABS:]) * route[:, 3:4]
    out_ref[...] = _rms(x2_ref[...] + moe, g_ref[...])


def _combine(x2, route, g_final, dest, ys_slabs):
    T = x2.shape[0]
    tm = MIX_TILE
    n_steps = T // tm
    rows = [_gather_rows(ys, dest.reshape(-1)) for ys in ys_slabs]
    slab_w = rows[0].shape[1]
    first = [pl.BlockSpec((tm, slab_w), lambda i: (i, 0))] * Y_SLABS
    second = [pl.BlockSpec((tm, slab_w), lambda i: (i + n_steps, 0))] * Y_SLABS
    return pl.pallas_call(
        _combine_kernel,
        grid=(n_steps,),
        in_specs=[pl.BlockSpec((tm, D_MODEL), lambda i: (i, 0)),
                  pl.BlockSpec((tm, ROUTE_W), lambda i: (i, 0)),
                  _const_spec((1, D_MODEL))] + first + second,
        out_specs=pl.BlockSpec((tm, D_MODEL), lambda i: (i, 0)),
        out_shape=jax.ShapeDtypeStruct((T, D_MODEL), F32),
        compiler_params=pltpu.CompilerParams(dimension_semantics=("arbitrary",)),
        name="combine",
    )(x2, route, g_final, *rows, *rows)


def _routing_tables(route, counts_rec, n_tokens):
    n_slots = 2 * n_tokens
    n_blocks = n_slots // MOE_BLOCK + N_EXPERTS
    n_rows = n_blocks * MOE_BLOCK
    n_pad = n_rows - n_slots
    counts = counts_rec[0, EXPERT_LANE0:EXPERT_LANE0 + N_EXPERTS].astype(jnp.int32)
    padded = (counts + MOE_BLOCK - 1) // MOE_BLOCK * MOE_BLOCK
    pends = jnp.cumsum(padded)
    pstarts = pends - padded
    expert = route[:, 0:2].astype(jnp.int32)
    rank = route[:, 4:6].astype(jnp.int32)
    lookup = lambda table, idx: jnp.sum(jnp.where(idx[..., None] == jnp.arange(N_EXPERTS), table, 0), axis=-1)
    count_le = lambda sorted_vals, q: jnp.sum(sorted_vals[None, :] <= q[:, None], axis=-1).astype(jnp.int32)
    dest = (lookup(pstarts, expert) + rank).T
    block_e = jnp.minimum(count_le(pends, jnp.arange(n_blocks, dtype=jnp.int32) * MOE_BLOCK), N_EXPERTS - 1)
    npad_e = padded - counts
    cum = jnp.cumsum(npad_e)
    idx = jnp.arange(n_pad, dtype=jnp.int32)
    e_of = count_le(cum, idx)
    in_expert = lookup(pstarts + counts - (cum - npad_e), e_of) + idx
    pad_rows = jnp.where(e_of < N_EXPERTS, in_expert, pends[-1] + (idx - cum[-1])).astype(jnp.int32)
    return dest.astype(jnp.int32), block_e, pad_rows, n_rows


def _encoder_group(x, mem, w):
    batch, seq, _ = x.shape
    T = batch * seq
    *qkv_groups, qkv_win = _proj(x, w["g_mix"], w["w_in"])

    slopes_a = _alibi_slopes(A_HEADS).reshape(A_GROUPS, A_HEADS_PER_GROUP)
    o_groups, lse_groups = [], []
    for g, (window, r) in enumerate(A_CONFIGS):
        o, lse = _banded_attention(qkv_groups[g], half_w=window // (2 * r), kv_width=GROUP_W,
                                   slopes=slopes_a[g] * np.float32(r))
        o_groups.append(o)
        lse_groups.append(lse)
    (o_b,) = _banded_attention(qkv_win.reshape(batch, 1, seq, WIN_W), half_w=B_HALF_WINDOW,
                               kv_width=KV2_W, slopes=_alibi_slopes(B_HEADS), sink=w["sink_b"],
                               want_lse=False)

    kvm = _memkv(mem.reshape(-1, D_MODEL), w["g_mem"], w["w_ckv"])
    mix_w = [w[k] for k in ("g_mix", "w_gate", "b_gate", "w_branch", "w_out", "g_xattn", "w_cq",
                            "w_co", "g_ffn", "w_route", "b_route")]
    x2, h_packed, route, counts_rec = _mix(x, o_groups, lse_groups, o_b, kvm, mix_w)

    dest, block_e, pad_rows, n_rows = _routing_tables(route, counts_rec, T)
    xs = _dispatch(h_packed, dest, pad_rows, n_rows)
    ys = _experts(xs, block_e, w["w_gu"], w["w_down"])
    y = _combine(x2, route, w["g_final"], dest, ys)
    return y.reshape(batch, seq, D_MODEL)


def _prep_weights(g_mix, w_in, sink_b, w_gate, b_gate, w_branch, w_out, g_xattn, g_mem, w_cq, w_ckv,
                  w_co, g_ffn, w_rg, b_rg, w_re, b_re, w_gu, w_down, g_final):
    scale = HEAD_DIM ** -0.5
    aw = A_WIDTH
    qa, ka, va = w_in[:, :aw] * scale, w_in[:, aw:2 * aw], w_in[:, 2 * aw:3 * aw]
    qb = w_in[:, 3 * aw:3 * aw + B_Q] * scale
    kb = w_in[:, 3 * aw + B_Q:3 * aw + B_Q + B_KV]
    vb = w_in[:, 3 * aw + B_Q + B_KV:]
    twice = lambda t: jnp.repeat(t.reshape(D_MODEL, B_KV_HEADS, 1, HEAD_DIM), 2, axis=2).reshape(D_MODEL, KV2_W)
    group = lambda t, g: t[:, g * GROUP_W:(g + 1) * GROUP_W]
    cols = [group(t, g) for g in range(A_GROUPS) for t in (qa, ka, va)] + [qb, twice(kb), twice(vb)]
    w_in_x = jnp.concatenate(cols, axis=1).astype(BF16)
    w_route = jnp.zeros((D_MODEL, ROUTE_W), F32)
    w_route = w_route.at[:, :N_GROUPS].set(w_rg).at[:, EXPERT_LANE0:EXPERT_LANE0 + N_EXPERTS].set(w_re)
    b_route = jnp.zeros((1, ROUTE_W), F32)
    b_route = b_route.at[0, :N_GROUPS].set(b_rg).at[0, EXPERT_LANE0:EXPERT_LANE0 + N_EXPERTS].set(b_re)
    w_route_hi = w_route.astype(BF16)
    w_route = jnp.concatenate([w_route_hi, (w_route - w_route_hi.astype(F32)).astype(BF16)], axis=1)
    vec = lambda v: v.reshape(1, -1).astype(F32)
    return dict(
        g_mix=vec(g_mix), w_in=w_in_x, sink_b=sink_b.astype(F32),
        w_gate=w_gate.astype(BF16), b_gate=vec(b_gate), w_branch=w_branch.astype(BF16),
        w_out=w_out.astype(BF16), g_xattn=vec(g_xattn), g_mem=vec(g_mem), w_cq=w_cq.astype(BF16),
        w_ckv=w_ckv.astype(BF16), w_co=w_co.astype(BF16), g_ffn=vec(g_ffn),
        w_route=w_route, b_route=b_route, w_gu=w_gu.astype(BF16), w_down=w_down.astype(BF16),
        g_final=vec(g_final))


def kernel(x_prompt, x_sample, mem_prompt, mem_sample, g_mix, w_in, sink_b, w_gate, b_gate, w_branch,
           w_out, g_xattn, g_mem, w_cq, w_ckv, w_co, g_ffn, w_rg, b_rg, w_re, b_re, w_gu, w_down,
           g_final):
    assert g_mix.shape[0] == 1, "single-layer encoder"
    w = _prep_weights(g_mix[0], w_in[0], sink_b[0], w_gate[0], b_gate[0], w_branch[0], w_out[0],
                      g_xattn[0], g_mem[0], w_cq[0], w_ckv[0], w_co[0], g_ffn[0], w_rg[0], b_rg[0],
                      w_re[0], b_re[0], w_gu[0], w_down[0], g_final)
    return (_encoder_group(x_prompt, mem_prompt, w), _encoder_group(x_sample, mem_sample, w))
```

```python
import functools

import numpy as np
import jax
import jax.numpy as jnp
from jax import lax
from jax.experimental import pallas as pl
from jax.experimental.pallas import tpu as pltpu
from jax.experimental.pallas import tpu_sc as plsc

F32 = jnp.float32
BF16 = jnp.bfloat16

D_MODEL = 1024
HEAD_DIM = 64
A_CONFIGS = ((128, 1), (512, 4), (2048, 16))
A_GROUPS = 3
A_HEADS_PER_GROUP = 8
A_HEADS = A_GROUPS * A_HEADS_PER_GROUP
A_WIDTH = A_HEADS * HEAD_DIM
B_HEADS = 8
B_KV_HEADS = 2
B_HALF_WINDOW = 128
B_Q = B_HEADS * HEAD_DIM
B_KV = B_KV_HEADS * HEAD_DIM
X_HEADS = 4
X_HEAD_DIM = 128
X_WIDTH = X_HEADS * X_HEAD_DIM
N_GROUPS = 4
EXPERTS_PER_GROUP = 8
N_EXPERTS = N_GROUPS * EXPERTS_PER_GROUP
EXPERT_FF = 512
MOE_BLOCK = 256
EPS = 1e-6
NEG = -1e30

LANES = 128
GROUP_W = A_HEADS_PER_GROUP * HEAD_DIM
PAIR_W = 2 * HEAD_DIM
N_PAIRS = GROUP_W // PAIR_W
QKV_W = 3 * GROUP_W
KV2_W = 2 * B_KV
WIN_W = B_Q + 2 * KV2_W
PROJ_W = A_GROUPS * QKV_W + WIN_W
ROUTE_W = LANES
EXPERT_LANE0 = 32
VMEM_LIMIT = 52 * 1024 * 1024

ROW_TILE = 256
PROJ_TILE = 512
MIX_TILE = 512
MIX_SUB = 256
Q_STEPS = 4
Q_UNROLL = 4
GATHER_WINDOW = 128
Y_SLABS = 2
def _rms(x, g):
    ms = jnp.mean(x * x, axis=-1, keepdims=True)
    return x * lax.rsqrt(ms + EPS) * g


def _alibi_slopes(n):
    return 2.0 ** (-8.0 * np.arange(1, n + 1, dtype=np.float32) / n)


def _const_spec(shape):
    return pl.BlockSpec(shape, lambda *_: (0,) * len(shape), pipeline_mode=pl.Buffered(1))


def _proj_kernel(x_ref, g_ref, w_ref, *refs):
    o_refs, win_ref, h_ref = refs[:A_GROUPS], refs[A_GROUPS], refs[A_GROUPS + 1]
    tm = x_ref.shape[0]
    h32 = _rms(x_ref[...], g_ref[...])
    h_nat = h32.astype(BF16)
    n_slabs = h_ref.shape[0]
    for s in range(n_slabs):
        h_ref[s] = h32[:, s * LANES:(s + 1) * LANES]
    for g, (_, r) in enumerate(A_CONFIGS):
        n = tm // r
        if r == 1:
            h = h_nat
        else:
            h = jnp.concatenate(
                [jnp.concatenate([h_ref[s, pl.ds(c, n, stride=r), :] for c in range(r)], axis=0)
                 for s in range(n_slabs)], axis=1).astype(BF16)
        for j in range(3):
            cols = slice(g * QKV_W + j * GROUP_W, g * QKV_W + (j + 1) * GROUP_W)
            res = jnp.dot(h, w_ref[:, cols], preferred_element_type=F32).astype(BF16)
            for c in range(r):
                o_refs[g][c, :, j * GROUP_W:(j + 1) * GROUP_W] = res[c * n:(c + 1) * n]
    for j in range(WIN_W // GROUP_W):
        cols = slice(A_GROUPS * QKV_W + j * GROUP_W, A_GROUPS * QKV_W + (j + 1) * GROUP_W)
        win_ref[:, j * GROUP_W:(j + 1) * GROUP_W] = jnp.dot(
            h_nat, w_ref[:, cols], preferred_element_type=F32).astype(BF16)


def _proj(x, g, w):
    batch, seq, _ = x.shape
    tm = PROJ_TILE
    out_specs = [pl.BlockSpec((None, r, tm // r, QKV_W), lambda b, i: (b, 0, i, 0)) for _, r in A_CONFIGS]
    out_shape = [jax.ShapeDtypeStruct((batch, r, seq // r, QKV_W), BF16) for _, r in A_CONFIGS]
    out_specs.append(pl.BlockSpec((None, tm, WIN_W), lambda b, i: (b, i, 0)))
    out_shape.append(jax.ShapeDtypeStruct((batch, seq, WIN_W), BF16))
    return pl.pallas_call(
        _proj_kernel,
        grid=(batch, seq // tm),
        in_specs=[pl.BlockSpec((None, tm, D_MODEL), lambda b, i: (b, i, 0)),
                  _const_spec((1, D_MODEL)),
                  _const_spec((D_MODEL, PROJ_W))],
        out_specs=out_specs,
        out_shape=out_shape,
        scratch_shapes=[pltpu.VMEM((D_MODEL // LANES, tm, LANES), F32)],
        compiler_params=pltpu.CompilerParams(dimension_semantics=("arbitrary",) * 2,
                                             vmem_limit_bytes=VMEM_LIMIT),
        name="proj",
    )(x, g, w)


def _attn_kernel(*refs, qb, kb, q_steps, m_len, half_w, offsets, kv_shared, has_sink, want_lse):
    refs = list(refs)
    sink_ref = refs.pop(0) if has_sink else None
    bias_ref, q_ref, k_ref, v_ref, o_ref = refs[:5]
    lse_ref = refs[5] if want_lse else None
    lo_q = lax.broadcasted_iota(jnp.int32, (qb, PAIR_W), 1) < HEAD_DIM
    first_head = lax.broadcasted_iota(jnp.int32, (2 * qb, 1), 0) < qb
    zeros_q = jnp.zeros((qb, PAIR_W), BF16)

    def q_block(it, carry):
        cc, qi = it // q_steps, it % q_steps
        gi = pl.program_id(2) * q_steps + qi
        ks = pl.multiple_of(jnp.clip(gi * qb - half_w, 0, m_len - kb), 16)
        off = gi * qb - ks
        var = sum(jnp.where(off == o, n, 0) for n, o in enumerate(offsets))
        rows = pl.ds(pl.multiple_of(qi * qb, qb), qb)
        for j in range(N_PAIRS):
            cols = slice(j * PAIR_W, (j + 1) * PAIR_W)
            jc = (j // 2) if kv_shared else j
            kcols = slice(jc * PAIR_W, (jc + 1) * PAIR_W)
            qp = q_ref[cc, rows, cols]
            q_st = jnp.concatenate([jnp.where(lo_q, qp, zeros_q), jnp.where(lo_q, zeros_q, qp)], axis=0)
            s = lax.dot_general(q_st, k_ref[cc, pl.ds(ks, kb), kcols], (((1,), (1,)), ((), ())),
                                preferred_element_type=F32) + bias_ref[var, j]
            m = jnp.max(s, axis=-1, keepdims=True)
            if has_sink:
                sk = jnp.where(first_head, sink_ref[2 * j], sink_ref[2 * j + 1])
                m = jnp.maximum(m, sk)
            e = jnp.exp(s - m)
            den = jnp.sum(e, axis=-1, keepdims=True)
            if has_sink:
                den = den + jnp.exp(sk - m)
            o2 = jnp.dot(e.astype(BF16), v_ref[cc, pl.ds(ks, kb), kcols], preferred_element_type=F32) / den
            o_ref[cc, rows, cols] = jnp.where(lo_q, o2[:qb], o2[qb:]).astype(BF16)
            if want_lse:
                l2 = m + jnp.log(den)
                lse_ref[cc, rows, cols] = jnp.where(lo_q, l2[:qb], l2[qb:])
        return carry

    n_iter = q_ref.shape[0] * q_steps
    lax.fori_loop(0, n_iter, q_block, 0, unroll=min(Q_UNROLL, n_iter))


def _bias_tables(offsets, slopes, qb, kb, half_w):
    rel = jnp.arange(qb, dtype=jnp.int32)[:, None] - jnp.arange(kb, dtype=jnp.int32)[None, :]
    dist = jnp.abs(rel[None] + jnp.asarray(offsets, jnp.int32)[:, None, None])
    bias = -jnp.asarray(slopes, F32)[None, :, None, None] * dist.astype(F32)[:, None]
    bias = jnp.where((dist <= half_w)[:, None], bias, NEG)
    return bias.reshape(len(offsets), N_PAIRS, 2 * qb, kb)


def _banded_attention(qkv, *, half_w, kv_width, slopes, sink=None, want_lse=True):
    batch, r, m_len, _ = qkv.shape
    qb = min(2 * half_w, m_len)
    kb = min(qb + 2 * half_w, m_len)
    nq = m_len // qb
    q_steps = min(Q_STEPS, nq)
    cb = min(r, Q_STEPS // q_steps)
    kv_shared = kv_width != GROUP_W
    has_sink = sink is not None
    k_block = GROUP_W // kv_width
    offsets = sorted({i * qb - min(max(i * qb - half_w, 0), m_len - kb) for i in range(nq)})
    bias = _bias_tables(offsets, slopes, qb, kb, half_w)

    in_specs = []
    args = []
    if has_sink:
        in_specs.append(pl.BlockSpec(memory_space=pltpu.SMEM))
        args.append(sink)
    in_specs += [
        _const_spec(bias.shape),
        pl.BlockSpec((None, cb, q_steps * qb, GROUP_W), lambda b, c, i: (b, c, i, 0)),
        pl.BlockSpec((None, cb, m_len, kv_width), lambda b, c, i: (b, c, 0, k_block)),
        pl.BlockSpec((None, cb, m_len, kv_width), lambda b, c, i: (b, c, 0, k_block + 1)),
    ]
    args += [bias, qkv, qkv, qkv]
    out_spec = pl.BlockSpec((None, cb, q_steps * qb, GROUP_W), lambda b, c, i: (b, c, i, 0))
    out_specs = [out_spec]
    out_shape = [jax.ShapeDtypeStruct((batch, r, m_len, GROUP_W), BF16)]
    if want_lse:
        out_specs.append(out_spec)
        out_shape.append(jax.ShapeDtypeStruct((batch, r, m_len, GROUP_W), F32))
    kern = functools.partial(_attn_kernel, qb=qb, kb=kb, q_steps=q_steps, m_len=m_len, half_w=half_w,
                             offsets=tuple(offsets), kv_shared=kv_shared, has_sink=has_sink,
                             want_lse=want_lse)
    return pl.pallas_call(
        kern,
        grid=(batch, r // cb, nq // q_steps),
        in_specs=in_specs,
        out_specs=out_specs,
        out_shape=out_shape,
        compiler_params=pltpu.CompilerParams(dimension_semantics=("arbitrary",) * 3,
                                             vmem_limit_bytes=VMEM_LIMIT),
        name=f"attn_r{r}_w{half_w}",
    )(*args)


def _memkv_kernel(m_ref, g_ref, w_ref, o_ref):
    h = _rms(m_ref[...], g_ref[...]).astype(BF16)
    o_ref[...] = jnp.dot(h, w_ref[...], preferred_element_type=F32).astype(BF16)


def _memkv(mem2d, g, w):
    R = mem2d.shape[0]
    return pl.pallas_call(
        _memkv_kernel,
        grid=(R // ROW_TILE,),
        in_specs=[pl.BlockSpec((ROW_TILE, D_MODEL), lambda i: (i, 0)),
                  _const_spec((1, D_MODEL)),
                  _const_spec((D_MODEL, 2 * X_WIDTH))],
        out_specs=pl.BlockSpec((ROW_TILE, 2 * X_WIDTH), lambda i: (i, 0)),
        out_shape=jax.ShapeDtypeStruct((R, 2 * X_WIDTH), BF16),
        compiler_params=pltpu.CompilerParams(dimension_semantics=("arbitrary",)),
        name="memkv",
    )(mem2d, g, w)


def _token_order(src_ref, dst_ref, m0, n):
    r = src_ref.shape[0]
    if r == 1:
        return src_ref[0, m0:m0 + n].astype(F32)
    n_slabs = dst_ref.shape[0]
    for c in range(r):
        rows = src_ref[c, m0:m0 + n].astype(F32)
        for s in range(n_slabs):
            dst_ref[s, pl.ds(m0 * r + c, n, stride=r), :] = rows[:, s * LANES:(s + 1) * LANES]
    return jnp.concatenate([dst_ref[s, m0 * r:(m0 + n) * r] for s in range(n_slabs)], axis=1)


def _mix_rows(t0, tm, x_ref, o_refs, l_refs, ob_ref, kv_ref,
              g_mix_ref, w_gate_ref, b_gate_ref, w_br_ref, w_out_ref,
              g_x_ref, w_cq_ref, w_co_ref, g_ffn_ref, w_rt_ref, b_rt_ref,
              x2_ref, hp_refs, route_ref, tri_ref, carry_ref, order_refs):
    rows = slice(t0, t0 + tm)
    x = x_ref[rows]
    h1 = _rms(x, g_mix_ref[...]).astype(BF16)

    by_class = lambda ref, scratch: _token_order(ref, scratch, t0 // ref.shape[0], tm // ref.shape[0])
    l0, l1, l2 = (by_class(l, s) for l, s in zip(l_refs, order_refs[:3]))
    lm = jnp.maximum(jnp.maximum(l0, l1), l2)
    e0, e1, e2 = jnp.exp(l0 - lm), jnp.exp(l1 - lm), jnp.exp(l2 - lm)
    den = e0 + e1 + e2
    o0, o1, o2 = (by_class(o, s) for o, s in zip(o_refs, order_refs[3:]))
    oa = (e0 / den) * o0 + (e1 / den) * o1 + (e2 / den) * o2
    br_a = jnp.dot(oa.astype(BF16), w_br_ref[:GROUP_W, :], preferred_element_type=F32)
    br_b = jnp.dot(ob_ref[rows], w_br_ref[GROUP_W:, :], preferred_element_type=F32)
    ga = jax.nn.sigmoid(jnp.dot(h1, w_gate_ref[:, :D_MODEL], preferred_element_type=F32)
                        + b_gate_ref[:, :D_MODEL])
    merged = ga * br_a
    gb = jax.nn.sigmoid(jnp.dot(h1, w_gate_ref[:, D_MODEL:], preferred_element_type=F32)
                        + b_gate_ref[:, D_MODEL:])
    merged = merged + gb * br_b
    x1 = x + jnp.dot(merged.astype(BF16), w_out_ref[...], preferred_element_type=F32)

    h2 = _rms(x1, g_x_ref[...]).astype(BF16)
    q = jnp.dot(h2, w_cq_ref[...], preferred_element_type=F32).astype(BF16)
    heads = []
    for h in range(X_HEADS):
        cols = slice(h * X_HEAD_DIM, (h + 1) * X_HEAD_DIM)
        kh = kv_ref[:, cols]
        vh = kv_ref[:, X_WIDTH + h * X_HEAD_DIM:X_WIDTH + (h + 1) * X_HEAD_DIM]
        s = lax.dot_general(q[:, cols], kh, (((1,), (1,)), ((), ())),
                            preferred_element_type=F32) * (X_HEAD_DIM ** -0.5)
        m = jnp.max(s, axis=-1, keepdims=True)
        e = jnp.exp(s - m)
        p = e / jnp.sum(e, axis=-1, keepdims=True)
        heads.append(jnp.dot(p.astype(BF16), vh, preferred_element_type=F32))
    o = jnp.concatenate(heads, axis=1).astype(BF16)
    x2 = x1 + jnp.dot(o, w_co_ref[...], preferred_element_type=F32)
    x2_ref[rows] = x2

    h3 = _rms(x2, g_ffn_ref[...])
    h3_hi = h3.astype(BF16)
    packed = lax.bitcast_convert_type(_pack_bf16_pairs(h3_hi.astype(F32)), jnp.int32)
    slab_w = packed.shape[1] // len(hp_refs)
    for n, hp_ref in enumerate(hp_refs):
        hp_ref[rows] = packed[:, n * slab_w:(n + 1) * slab_w]
    h3_lo = (h3 - h3_hi.astype(F32)).astype(BF16)
    by_hi = jnp.dot(h3_hi, w_rt_ref[...], preferred_element_type=F32)
    logits = (by_hi[:, :ROUTE_W] + by_hi[:, ROUTE_W:]
              + jnp.dot(h3_lo, w_rt_ref[:, :ROUTE_W], preferred_element_type=F32)) + b_rt_ref[...]
    lane = lax.broadcasted_iota(jnp.int32, (tm, ROUTE_W), 1)
    gmask = lane < N_GROUPS
    gl = jnp.where(gmask, logits, -jnp.inf)
    gmax = jnp.max(gl, axis=-1, keepdims=True)
    gidx = jnp.min(jnp.where(gl == gmax, lane, ROUTE_W), axis=-1, keepdims=True)
    pg_sel = 1.0 / jnp.sum(jnp.where(gmask, jnp.exp(logits - gmax), 0.0), axis=-1, keepdims=True)
    e_lo = EXPERT_LANE0 + gidx * EXPERTS_PER_GROUP
    emask = (lane >= e_lo) & (lane < e_lo + EXPERTS_PER_GROUP)
    el = jnp.where(emask, logits, -jnp.inf)
    emax1 = jnp.max(el, axis=-1, keepdims=True)
    i1 = jnp.min(jnp.where(el == emax1, lane, ROUTE_W), axis=-1, keepdims=True)
    el2 = jnp.where(lane == i1, -jnp.inf, el)
    emax2 = jnp.max(el2, axis=-1, keepdims=True)
    i2 = jnp.min(jnp.where(el2 == emax2, lane, ROUTE_W), axis=-1, keepdims=True)
    t2 = jnp.exp(emax2 - emax1)
    w1 = pg_sel / (1.0 + t2)
    w2 = pg_sel * t2 / (1.0 + t2)

    oh1 = lane == i1
    oh2 = lane == i2
    ohs = jnp.where(oh1 | oh2, 1.0, 0.0)
    before = jnp.dot(tri_ref[...], ohs.astype(BF16), preferred_element_type=F32) + carry_ref[...]
    rank1 = jnp.sum(jnp.where(oh1, before, 0.0), axis=-1, keepdims=True)
    rank2 = jnp.sum(jnp.where(oh2, before, 0.0), axis=-1, keepdims=True)
    carry_ref[...] = carry_ref[...] + jnp.sum(ohs, axis=0, keepdims=True)

    rec = jnp.where(lane == 0, (i1 - EXPERT_LANE0).astype(F32), 0.0)
    rec = jnp.where(lane == 1, (i2 - EXPERT_LANE0).astype(F32), rec)
    rec = jnp.where(lane == 2, w1, rec)
    rec = jnp.where(lane == 3, w2, rec)
    rec = jnp.where(lane == 4, rank1, rec)
    rec = jnp.where(lane == 5, rank2, rec)
    route_ref[rows] = rec


def _mix_kernel(x_ref, o0_ref, o1_ref, o2_ref, l0_ref, l1_ref, l2_ref, ob_ref, *refs):
    n_w = 12
    w_refs, refs = refs[:n_w], refs[n_w:]
    x2_ref, hp_refs, refs = refs[0], refs[1:1 + Y_SLABS], refs[1 + Y_SLABS:]
    route_ref, counts_ref, tri_ref, carry_ref = refs[:4]
    order_refs = refs[4:]
    sub = tri_ref.shape[0]

    @pl.when((pl.program_id(0) == 0) & (pl.program_id(1) == 0))
    def _():
        row = lax.broadcasted_iota(jnp.int32, (sub, sub), 0)
        col = lax.broadcasted_iota(jnp.int32, (sub, sub), 1)
        tri_ref[...] = jnp.where(row > col, 1.0, 0.0).astype(BF16)
        carry_ref[...] = jnp.zeros_like(carry_ref)

    for t0 in range(0, x_ref.shape[0], sub):
        _mix_rows(t0, sub, x_ref, (o0_ref, o1_ref, o2_ref), (l0_ref, l1_ref, l2_ref), ob_ref, *w_refs,
                  x2_ref, hp_refs, route_ref, tri_ref, carry_ref, order_refs)
    counts_ref[...] = jnp.broadcast_to(carry_ref[...], counts_ref.shape)


def _mix(x, o_groups, lse_groups, o_b, kvm, wts):
    batch, seq, _ = x.shape
    T = batch * seq
    tm = MIX_TILE
    steps = seq // tm
    mem_tokens = kvm.shape[0] // batch
    row = lambda w: pl.BlockSpec((tm, w), lambda b, i: (b * steps + i, 0))
    by_class = [pl.BlockSpec((None, r, tm // r, GROUP_W), lambda b, i: (b, 0, i, 0)) for _, r in A_CONFIGS]
    in_specs = ([pl.BlockSpec((None, tm, D_MODEL), lambda b, i: (b, i, 0))] + by_class + by_class
                + [pl.BlockSpec((None, None, tm, GROUP_W), lambda b, i: (b, 0, i, 0)),
                   pl.BlockSpec((mem_tokens, 2 * X_WIDTH), lambda b, i: (b, 0))]
                + [_const_spec(w.shape) for w in wts])
    return pl.pallas_call(
        _mix_kernel,
        grid=(batch, steps),
        in_specs=in_specs,
        out_specs=[row(D_MODEL)] + [row(D_MODEL // 2 // Y_SLABS)] * Y_SLABS
        + [row(ROUTE_W), pl.BlockSpec((8, ROUTE_W), lambda b, i: (0, 0))],
        out_shape=[jax.ShapeDtypeStruct((T, D_MODEL), F32)]
        + [jax.ShapeDtypeStruct((T, D_MODEL // 2 // Y_SLABS), jnp.int32)] * Y_SLABS
        + [jax.ShapeDtypeStruct((T, ROUTE_W), F32),
                   jax.ShapeDtypeStruct((8, ROUTE_W), F32)],
        scratch_shapes=[pltpu.VMEM((MIX_SUB, MIX_SUB), BF16), pltpu.VMEM((1, ROUTE_W), F32)]
        + [pltpu.VMEM((GROUP_W // LANES, tm, LANES), F32)] * (2 * A_GROUPS),
        compiler_params=pltpu.CompilerParams(dimension_semantics=("arbitrary",) * 2,
                                             vmem_limit_bytes=VMEM_LIMIT),
        name="mix",
    )(x, *o_groups, *lse_groups, o_b, kvm, *wts)


def _pack_bf16_pairs(x):
    w = x.shape[1] // 2
    lo = lax.bitcast_convert_type(x[:, :w], jnp.uint32) >> 16
    hi = lax.bitcast_convert_type(x[:, w:], jnp.uint32) & jnp.uint32(0xFFFF0000)
    return lo | hi


def _unpack_bf16_pairs(u):
    lo = lax.bitcast_convert_type(u << 16, F32)
    hi = lax.bitcast_convert_type(u & jnp.uint32(0xFFFF0000), F32)
    return jnp.concatenate([lo, hi], axis=1)


def _scatter_rows(rows, dest, n_rows):
    T, width = rows.shape
    mesh = plsc.VectorSubcoreMesh(core_axis_name="core", subcore_axis_name="subcore")

    @pl.kernel(out_type=jax.ShapeDtypeStruct((n_rows, width), rows.dtype), mesh=mesh, scratch_types=[])
    def scatter_kernel(rows_hbm, idx0_hbm, idx1_hbm, out_hbm):
        def body(rows_vmem, idx0_vmem, idx1_vmem):
            pltpu.sync_copy(rows_vmem, out_hbm.at[idx0_vmem.at[0]])
            pltpu.sync_copy(rows_vmem, out_hbm.at[idx1_vmem.at[0]])

        idx_spec = pl.BlockSpec((1, GATHER_WINDOW), index_map=lambda i: (0, i))
        pltpu.emit_pipeline(
            body,
            grid=(T // GATHER_WINDOW,),
            in_specs=[pl.BlockSpec((GATHER_WINDOW, width), index_map=lambda i: (i, 0)), idx_spec, idx_spec],
            out_specs=[],
            core_axis_name="subcore",
            dimension_semantics=(pltpu.PARALLEL,),
        )(rows_hbm, idx0_hbm, idx1_hbm)

    return scatter_kernel(rows, dest[0].reshape(1, T), dest[1].reshape(1, T))


def _expert_kernel(be_ref, nv_ref, *refs):
    xs_refs, (wgu_ref, wdn_ref), ys_refs = refs[:Y_SLABS], refs[Y_SLABS:Y_SLABS + 2], refs[Y_SLABS + 2:]
    packed_in = jnp.concatenate([r[...] for r in xs_refs], axis=1)
    row = lax.broadcasted_iota(jnp.int32, packed_in.shape, 0)
    packed_in = jnp.where(row < nv_ref[pl.program_id(0)], packed_in, 0)
    xb = _unpack_bf16_pairs(lax.bitcast_convert_type(packed_in, jnp.uint32)).astype(BF16)
    gu = jnp.dot(xb, wgu_ref[...], preferred_element_type=F32)
    gate, up = gu[:, :EXPERT_FF], gu[:, EXPERT_FF:]
    act = (gate * jax.nn.sigmoid(gate) * up).astype(BF16)
    y = jnp.dot(act, wdn_ref[...], preferred_element_type=F32)
    packed = lax.bitcast_convert_type(_pack_bf16_pairs(y.astype(BF16).astype(F32)), jnp.int32)
    w = packed.shape[1] // len(ys_refs)
    for n, ys_ref in enumerate(ys_refs):
        ys_ref[...] = packed[:, n * w:(n + 1) * w]


def _experts(xs_slabs, block_e, block_valid, w_gu, w_down):
    P, slab_w = xs_slabs[0].shape
    n_blocks = P // MOE_BLOCK
    slab_spec = pl.BlockSpec((MOE_BLOCK, slab_w), lambda b, be, nv: (b, 0))
    grid_spec = pltpu.PrefetchScalarGridSpec(
        num_scalar_prefetch=2,
        grid=(n_blocks,),
        in_specs=[slab_spec] * Y_SLABS
        + [pl.BlockSpec((None, D_MODEL, 2 * EXPERT_FF), lambda b, be, nv: (be[b], 0, 0)),
           pl.BlockSpec((None, EXPERT_FF, D_MODEL), lambda b, be, nv: (be[b], 0, 0))],
        out_specs=[slab_spec] * Y_SLABS,
    )
    return pl.pallas_call(
        _expert_kernel,
        grid_spec=grid_spec,
        out_shape=[jax.ShapeDtypeStruct((P, slab_w), jnp.int32)] * Y_SLABS,
        compiler_params=pltpu.CompilerParams(dimension_semantics=("arbitrary",),
                                             vmem_limit_bytes=VMEM_LIMIT),
        name="experts",
    )(block_e, block_valid, *xs_slabs, w_gu, w_down)


def _gather_rows(table, indices):
    n, width = indices.shape[0], table.shape[1]
    mesh = plsc.VectorSubcoreMesh(core_axis_name="core", subcore_axis_name="subcore")

    @pl.kernel(out_type=jax.ShapeDtypeStruct((n, width), table.dtype), mesh=mesh)
    def gather_kernel(table_hbm, idx_hbm, out_hbm):
        def body(idx_vmem, out_vmem):
            pltpu.sync_copy(table_hbm.at[idx_vmem.at[0]], out_vmem)

        pltpu.emit_pipeline(
            body,
            grid=(n // GATHER_WINDOW,),
            in_specs=[pl.BlockSpec((1, GATHER_WINDOW), index_map=lambda i: (0, i))],
            out_specs=[pl.BlockSpec((GATHER_WINDOW, width), index_map=lambda i: (i, 0))],
            core_axis_name="subcore",
            dimension_semantics=(pltpu.PARALLEL,),
        )(idx_hbm, out_hbm)

    return gather_kernel(table, indices.reshape(1, n))


def _combine_kernel(x2_ref, route_ref, g_ref, *refs):
    y_refs, out_ref = refs[:-1], refs[-1]
    route = route_ref[...]

    def expert_rows(slabs):
        packed = jnp.concatenate([s[...] for s in slabs], axis=1)
        return _unpack_bf16_pairs(lax.bitcast_convert_type(packed, jnp.uint32))

    moe = expert_rows(y_refs[:Y_SLABS]) * route[:, 2:3] + expert_rows(y_refs[Y_SLABS:]) * route[:, 3:4]
    out_ref[...] = _rms(x2_ref[...] + moe, g_ref[...])


def _combine(x2, route, g_final, dest, ys_slabs):
    T = x2.shape[0]
    tm = MIX_TILE
    n_steps = T // tm
    rows = [_gather_rows(ys, dest.reshape(-1)) for ys in ys_slabs]
    slab_w = rows[0].shape[1]
    first = [pl.BlockSpec((tm, slab_w), lambda i: (i, 0))] * Y_SLABS
    second = [pl.BlockSpec((tm, slab_w), lambda i: (i + n_steps, 0))] * Y_SLABS
    return pl.pallas_call(
        _combine_kernel,
        grid=(n_steps,),
        in_specs=[pl.BlockSpec((tm, D_MODEL), lambda i: (i, 0)),
                  pl.BlockSpec((tm, ROUTE_W), lambda i: (i, 0)),
                  _const_spec((1, D_MODEL))] + first + second,
        out_specs=pl.BlockSpec((tm, D_MODEL), lambda i: (i, 0)),
        out_shape=jax.ShapeDtypeStruct((T, D_MODEL), F32),
        compiler_params=pltpu.CompilerParams(dimension_semantics=("arbitrary",)),
        name="combine",
    )(x2, route, g_final, *rows, *rows)


def _routing_tables(route, counts_rec, n_tokens):
    n_slots = 2 * n_tokens
    n_blocks = n_slots // MOE_BLOCK + N_EXPERTS
    n_rows = n_blocks * MOE_BLOCK
    counts = counts_rec[0, EXPERT_LANE0:EXPERT_LANE0 + N_EXPERTS].astype(jnp.int32)
    padded = (counts + MOE_BLOCK - 1) // MOE_BLOCK * MOE_BLOCK
    pends = jnp.cumsum(padded)
    pstarts = pends - padded
    expert = route[:, 0:2].astype(jnp.int32)
    rank = route[:, 4:6].astype(jnp.int32)
    lookup = lambda table, idx: jnp.sum(jnp.where(idx[..., None] == jnp.arange(N_EXPERTS), table, 0), axis=-1)
    count_le = lambda sorted_vals, q: jnp.sum(sorted_vals[None, :] <= q[:, None], axis=-1).astype(jnp.int32)
    dest = (lookup(pstarts, expert) + rank).T
    block_row0 = jnp.arange(n_blocks, dtype=jnp.int32) * MOE_BLOCK
    owner = count_le(pends, block_row0)
    block_e = jnp.minimum(owner, N_EXPERTS - 1)
    block_valid = jnp.clip(lookup(pstarts + counts, owner) - block_row0, 0, MOE_BLOCK)
    block_valid = jnp.where(owner < N_EXPERTS, block_valid, 0).astype(jnp.int32)
    return dest.astype(jnp.int32), block_e, block_valid, n_rows


def _encoder_group(x, mem, w):
    batch, seq, _ = x.shape
    T = batch * seq
    *qkv_groups, qkv_win = _proj(x, w["g_mix"], w["w_in"])

    slopes_a = _alibi_slopes(A_HEADS).reshape(A_GROUPS, A_HEADS_PER_GROUP)
    o_groups, lse_groups = [], []
    for g, (window, r) in enumerate(A_CONFIGS):
        o, lse = _banded_attention(qkv_groups[g], half_w=window // (2 * r), kv_width=GROUP_W,
                                   slopes=slopes_a[g] * np.float32(r))
        o_groups.append(o)
        lse_groups.append(lse)
    (o_b,) = _banded_attention(qkv_win.reshape(batch, 1, seq, WIN_W), half_w=B_HALF_WINDOW,
                               kv_width=KV2_W, slopes=_alibi_slopes(B_HEADS), sink=w["sink_b"],
                               want_lse=False)

    kvm = _memkv(mem.reshape(-1, D_MODEL), w["g_mem"], w["w_ckv"])
    mix_w = [w[k] for k in ("g_mix", "w_gate", "b_gate", "w_branch", "w_out", "g_xattn", "w_cq",
                            "w_co", "g_ffn", "w_route", "b_route")]
    x2, *h_slabs, route, counts_rec = _mix(x, o_groups, lse_groups, o_b, kvm, mix_w)

    dest, block_e, block_valid, n_rows = _routing_tables(route, counts_rec, T)
    xs_slabs = [_scatter_rows(h, dest, n_rows) for h in h_slabs]
    ys_slabs = _experts(xs_slabs, block_e, block_valid, w["w_gu"], w["w_down"])
    y = _combine(x2, route, w["g_final"], dest, ys_slabs)
    return y.reshape(batch, seq, D_MODEL)


def _prep_weights(g_mix, w_in, sink_b, w_gate, b_gate, w_branch, w_out, g_xattn, g_mem, w_cq, w_ckv,
                  w_co, g_ffn, w_rg, b_rg, w_re, b_re, w_gu, w_down, g_final):
    scale = HEAD_DIM ** -0.5
    aw = A_WIDTH
    qa, ka, va = w_in[:, :aw] * scale, w_in[:, aw:2 * aw], w_in[:, 2 * aw:3 * aw]
    qb = w_in[:, 3 * aw:3 * aw + B_Q] * scale
    kb = w_in[:, 3 * aw + B_Q:3 * aw + B_Q + B_KV]
    vb = w_in[:, 3 * aw + B_Q + B_KV:]
    twice = lambda t: jnp.repeat(t.reshape(D_MODEL, B_KV_HEADS, 1, HEAD_DIM), 2, axis=2).reshape(D_MODEL, KV2_W)
    group = lambda t, g: t[:, g * GROUP_W:(g + 1) * GROUP_W]
    cols = [group(t, g) for g in range(A_GROUPS) for t in (qa, ka, va)] + [qb, twice(kb), twice(vb)]
    w_in_x = jnp.concatenate(cols, axis=1).astype(BF16)
    w_route = jnp.zeros((D_MODEL, ROUTE_W), F32)
    w_route = w_route.at[:, :N_GROUPS].set(w_rg).at[:, EXPERT_LANE0:EXPERT_LANE0 + N_EXPERTS].set(w_re)
    b_route = jnp.zeros((1, ROUTE_W), F32)
    b_route = b_route.at[0, :N_GROUPS].set(b_rg).at[0, EXPERT_LANE0:EXPERT_LANE0 + N_EXPERTS].set(b_re)
    w_route_hi = w_route.astype(BF16)
    w_route = jnp.concatenate([w_route_hi, (w_route - w_route_hi.astype(F32)).astype(BF16)], axis=1)
    vec = lambda v: v.reshape(1, -1).astype(F32)
    return dict(
        g_mix=vec(g_mix), w_in=w_in_x, sink_b=sink_b.astype(F32),
        w_gate=w_gate.astype(BF16), b_gate=vec(b_gate), w_branch=w_branch.astype(BF16),
        w_out=w_out.astype(BF16), g_xattn=vec(g_xattn), g_mem=vec(g_mem), w_cq=w_cq.astype(BF16),
        w_ckv=w_ckv.astype(BF16), w_co=w_co.astype(BF16), g_ffn=vec(g_ffn),
        w_route=w_route, b_route=b_route, w_gu=w_gu.astype(BF16), w_down=w_down.astype(BF16),
        g_final=vec(g_final))


def kernel(x_prompt, x_sample, mem_prompt, mem_sample, g_mix, w_in, sink_b, w_gate, b_gate, w_branch,
           w_out, g_xattn, g_mem, w_cq, w_ckv, w_co, g_ffn, w_rg, b_rg, w_re, b_re, w_gu, w_down,
           g_final):
    assert g_mix.shape[0] == 1, "single-layer encoder"
    w = _prep_weights(g_mix[0], w_in[0], sink_b[0], w_gate[0], b_gate[0], w_branch[0], w_out[0],
                      g_xattn[0], g_mem[0], w_cq[0], w_ckv[0], w_co[0], g_ffn[0], w_rg[0], b_rg[0],
                      w_re[0], b_re[0], w_gu[0], w_down[0], g_final)
    return (_encoder_group(x_prompt, mem_prompt, w), _encoder_group(x_sample, mem_sample, w))
```

```python
import functools

import numpy as np
import jax
import jax.numpy as jnp
from jax import lax
from jax.experimental import pallas as pl
from jax.experimental.pallas import tpu as pltpu
from jax.experimental.pallas import tpu_sc as plsc

F32 = jnp.float32
BF16 = jnp.bfloat16

D_MODEL = 1024
HEAD_DIM = 64
A_CONFIGS = ((128, 1), (512, 4), (2048, 16))
A_GROUPS = 3
A_HEADS_PER_GROUP = 8
A_HEADS = A_GROUPS * A_HEADS_PER_GROUP
A_WIDTH = A_HEADS * HEAD_DIM
B_HEADS = 8
B_KV_HEADS = 2
B_HALF_WINDOW = 128
B_Q = B_HEADS * HEAD_DIM
B_KV = B_KV_HEADS * HEAD_DIM
X_HEADS = 4
X_HEAD_DIM = 128
X_WIDTH = X_HEADS * X_HEAD_DIM
N_GROUPS = 4
EXPERTS_PER_GROUP = 8
N_EXPERTS = N_GROUPS * EXPERTS_PER_GROUP
EXPERT_FF = 512
MOE_BLOCK = 256
EPS = 1e-6
NEG = -1e30

LANES = 128
GROUP_W = A_HEADS_PER_GROUP * HEAD_DIM
PAIR_W = 2 * HEAD_DIM
N_PAIRS = GROUP_W // PAIR_W
QKV_W = 3 * GROUP_W
KV2_W = 2 * B_KV
WIN_W = B_Q + 2 * KV2_W
PROJ_W = A_GROUPS * QKV_W + WIN_W
ROUTE_W = LANES
EXPERT_LANE0 = 32
VMEM_LIMIT = 52 * 1024 * 1024

ROW_TILE = 256
PROJ_TILE = 512
MIX_TILE = 512
MIX_SUB = 256
Q_STEPS = 4
Q_UNROLL = 4
GATHER_WINDOW = 128
BLOCKS_PER_STEP = 2
Y_SLABS = 2
def _rms(x, g):
    ms = jnp.mean(x * x, axis=-1, keepdims=True)
    return x * lax.rsqrt(ms + EPS) * g


def _alibi_slopes(n):
    return 2.0 ** (-8.0 * np.arange(1, n + 1, dtype=np.float32) / n)


def _const_spec(shape):
    return pl.BlockSpec(shape, lambda *_: (0,) * len(shape), pipeline_mode=pl.Buffered(1))


def _proj_kernel(x_ref, g_ref, w_ref, *refs):
    o_refs, win_ref, h_ref = refs[:A_GROUPS], refs[A_GROUPS], refs[A_GROUPS + 1]
    tm = x_ref.shape[0]
    h32 = _rms(x_ref[...], g_ref[...])
    h_nat = h32.astype(BF16)
    n_slabs = h_ref.shape[0]
    for s in range(n_slabs):
        h_ref[s] = h32[:, s * LANES:(s + 1) * LANES]
    for g, (_, r) in enumerate(A_CONFIGS):
        n = tm // r
        if r == 1:
            h = h_nat
        else:
            h = jnp.concatenate(
                [jnp.concatenate([h_ref[s, pl.ds(c, n, stride=r), :] for c in range(r)], axis=0)
                 for s in range(n_slabs)], axis=1).astype(BF16)
        for j in range(3):
            cols = slice(g * QKV_W + j * GROUP_W, g * QKV_W + (j + 1) * GROUP_W)
            res = jnp.dot(h, w_ref[:, cols], preferred_element_type=F32).astype(BF16)
            for c in range(r):
                o_refs[g][c, :, j * GROUP_W:(j + 1) * GROUP_W] = res[c * n:(c + 1) * n]
    for j in range(WIN_W // GROUP_W):
        cols = slice(A_GROUPS * QKV_W + j * GROUP_W, A_GROUPS * QKV_W + (j + 1) * GROUP_W)
        win_ref[:, j * GROUP_W:(j + 1) * GROUP_W] = jnp.dot(
            h_nat, w_ref[:, cols], preferred_element_type=F32).astype(BF16)


def _proj(x, g, w):
    batch, seq, _ = x.shape
    tm = PROJ_TILE
    out_specs = [pl.BlockSpec((None, r, tm // r, QKV_W), lambda b, i: (b, 0, i, 0)) for _, r in A_CONFIGS]
    out_shape = [jax.ShapeDtypeStruct((batch, r, seq // r, QKV_W), BF16) for _, r in A_CONFIGS]
    out_specs.append(pl.BlockSpec((None, tm, WIN_W), lambda b, i: (b, i, 0)))
    out_shape.append(jax.ShapeDtypeStruct((batch, seq, WIN_W), BF16))
    return pl.pallas_call(
        _proj_kernel,
        grid=(batch, seq // tm),
        in_specs=[pl.BlockSpec((None, tm, D_MODEL), lambda b, i: (b, i, 0)),
                  _const_spec((1, D_MODEL)),
                  _const_spec((D_MODEL, PROJ_W))],
        out_specs=out_specs,
        out_shape=out_shape,
        scratch_shapes=[pltpu.VMEM((D_MODEL // LANES, tm, LANES), F32)],
        compiler_params=pltpu.CompilerParams(dimension_semantics=("arbitrary",) * 2,
                                             vmem_limit_bytes=VMEM_LIMIT),
        name="proj",
    )(x, g, w)


def _attn_kernel(*refs, qb, kb, q_steps, m_len, half_w, offsets, kv_shared, has_sink, want_lse):
    refs = list(refs)
    sink_ref = refs.pop(0) if has_sink else None
    bias_ref, q_ref, k_ref, v_ref, o_ref = refs[:5]
    lse_ref = refs[5] if want_lse else None
    lo_q = lax.broadcasted_iota(jnp.int32, (qb, PAIR_W), 1) < HEAD_DIM
    first_head = lax.broadcasted_iota(jnp.int32, (2 * qb, 1), 0) < qb
    zeros_q = jnp.zeros((qb, PAIR_W), BF16)

    def q_block(it, carry):
        cc, qi = it // q_steps, it % q_steps
        gi = pl.program_id(2) * q_steps + qi
        ks = pl.multiple_of(jnp.clip(gi * qb - half_w, 0, m_len - kb), 16)
        off = gi * qb - ks
        var = sum(jnp.where(off == o, n, 0) for n, o in enumerate(offsets))
        rows = pl.ds(pl.multiple_of(qi * qb, qb), qb)
        for j in range(N_PAIRS):
            cols = slice(j * PAIR_W, (j + 1) * PAIR_W)
            jc = (j // 2) if kv_shared else j
            kcols = slice(jc * PAIR_W, (jc + 1) * PAIR_W)
            qp = q_ref[cc, rows, cols]
            q_st = jnp.concatenate([jnp.where(lo_q, qp, zeros_q), jnp.where(lo_q, zeros_q, qp)], axis=0)
            s = lax.dot_general(q_st, k_ref[cc, pl.ds(ks, kb), kcols], (((1,), (1,)), ((), ())),
                                preferred_element_type=F32) + bias_ref[var, j]
            m = jnp.max(s, axis=-1, keepdims=True)
            if has_sink:
                sk = jnp.where(first_head, sink_ref[2 * j], sink_ref[2 * j + 1])
                m = jnp.maximum(m, sk)
            e = jnp.exp(s - m)
            den = jnp.sum(e, axis=-1, keepdims=True)
            if has_sink:
                den = den + jnp.exp(sk - m)
            o2 = jnp.dot(e.astype(BF16), v_ref[cc, pl.ds(ks, kb), kcols], preferred_element_type=F32) / den
            o_ref[cc, rows, cols] = jnp.where(lo_q, o2[:qb], o2[qb:]).astype(BF16)
            if want_lse:
                l2 = m + jnp.log(den)
                lse_ref[cc, rows, cols] = jnp.where(lo_q, l2[:qb], l2[qb:])
        return carry

    n_iter = q_ref.shape[0] * q_steps
    lax.fori_loop(0, n_iter, q_block, 0, unroll=min(Q_UNROLL, n_iter))


def _bias_tables(offsets, slopes, qb, kb, half_w):
    rel = jnp.arange(qb, dtype=jnp.int32)[:, None] - jnp.arange(kb, dtype=jnp.int32)[None, :]
    dist = jnp.abs(rel[None] + jnp.asarray(offsets, jnp.int32)[:, None, None])
    bias = -jnp.asarray(slopes, F32)[None, :, None, None] * dist.astype(F32)[:, None]
    bias = jnp.where((dist <= half_w)[:, None], bias, NEG)
    return bias.reshape(len(offsets), N_PAIRS, 2 * qb, kb)


def _banded_attention(qkv, *, half_w, kv_width, slopes, sink=None, want_lse=True):
    batch, r, m_len, _ = qkv.shape
    qb = min(2 * half_w, m_len)
    kb = min(qb + 2 * half_w, m_len)
    nq = m_len // qb
    q_steps = min(Q_STEPS, nq)
    cb = min(r, Q_STEPS // q_steps)
    kv_shared = kv_width != GROUP_W
    has_sink = sink is not None
    k_block = GROUP_W // kv_width
    offsets = sorted({i * qb - min(max(i * qb - half_w, 0), m_len - kb) for i in range(nq)})
    bias = _bias_tables(offsets, slopes, qb, kb, half_w)

    in_specs = []
    args = []
    if has_sink:
        in_specs.append(pl.BlockSpec(memory_space=pltpu.SMEM))
        args.append(sink)
    in_specs += [
        _const_spec(bias.shape),
        pl.BlockSpec((None, cb, q_steps * qb, GROUP_W), lambda b, c, i: (b, c, i, 0)),
        pl.BlockSpec((None, cb, m_len, kv_width), lambda b, c, i: (b, c, 0, k_block)),
        pl.BlockSpec((None, cb, m_len, kv_width), lambda b, c, i: (b, c, 0, k_block + 1)),
    ]
    args += [bias, qkv, qkv, qkv]
    out_spec = pl.BlockSpec((None, cb, q_steps * qb, GROUP_W), lambda b, c, i: (b, c, i, 0))
    out_specs = [out_spec]
    out_shape = [jax.ShapeDtypeStruct((batch, r, m_len, GROUP_W), BF16)]
    if want_lse:
        out_specs.append(out_spec)
        out_shape.append(jax.ShapeDtypeStruct((batch, r, m_len, GROUP_W), F32))
    kern = functools.partial(_attn_kernel, qb=qb, kb=kb, q_steps=q_steps, m_len=m_len, half_w=half_w,
                             offsets=tuple(offsets), kv_shared=kv_shared, has_sink=has_sink,
                             want_lse=want_lse)
    return pl.pallas_call(
        kern,
        grid=(batch, r // cb, nq // q_steps),
        in_specs=in_specs,
        out_specs=out_specs,
        out_shape=out_shape,
        compiler_params=pltpu.CompilerParams(dimension_semantics=("arbitrary",) * 3,
                                             vmem_limit_bytes=VMEM_LIMIT),
        name=f"attn_r{r}_w{half_w}",
    )(*args)


def _memkv_kernel(m_ref, g_ref, w_ref, o_ref):
    h = _rms(m_ref[...], g_ref[...]).astype(BF16)
    o_ref[...] = jnp.dot(h, w_ref[...], preferred_element_type=F32).astype(BF16)


def _memkv(mem2d, g, w):
    R = mem2d.shape[0]
    return pl.pallas_call(
        _memkv_kernel,
        grid=(R // ROW_TILE,),
        in_specs=[pl.BlockSpec((ROW_TILE, D_MODEL), lambda i: (i, 0)),
                  _const_spec((1, D_MODEL)),
                  _const_spec((D_MODEL, 2 * X_WIDTH))],
        out_specs=pl.BlockSpec((ROW_TILE, 2 * X_WIDTH), lambda i: (i, 0)),
        out_shape=jax.ShapeDtypeStruct((R, 2 * X_WIDTH), BF16),
        compiler_params=pltpu.CompilerParams(dimension_semantics=("arbitrary",)),
        name="memkv",
    )(mem2d, g, w)


def _token_order(src_ref, dst_ref, m0, n):
    r = src_ref.shape[0]
    if r == 1:
        return src_ref[0, m0:m0 + n].astype(F32)
    n_slabs = dst_ref.shape[0]
    for c in range(r):
        rows = src_ref[c, m0:m0 + n].astype(F32)
        for s in range(n_slabs):
            dst_ref[s, pl.ds(m0 * r + c, n, stride=r), :] = rows[:, s * LANES:(s + 1) * LANES]
    return jnp.concatenate([dst_ref[s, m0 * r:(m0 + n) * r] for s in range(n_slabs)], axis=1)


def _mix_rows(t0, tm, x_ref, o_refs, l_refs, ob_ref, kv_ref,
              g_mix_ref, w_gate_ref, b_gate_ref, w_br_ref, w_out_ref,
              g_x_ref, w_cq_ref, w_co_ref, g_ffn_ref, w_rt_ref, b_rt_ref,
              x2_ref, hp_refs, route_ref, tri_ref, carry_ref, order_refs):
    rows = slice(t0, t0 + tm)
    x = x_ref[rows]
    h1 = _rms(x, g_mix_ref[...]).astype(BF16)

    by_class = lambda ref, scratch: _token_order(ref, scratch, t0 // ref.shape[0], tm // ref.shape[0])
    l0, l1, l2 = (by_class(l, s) for l, s in zip(l_refs, order_refs[:3]))
    lm = jnp.maximum(jnp.maximum(l0, l1), l2)
    e0, e1, e2 = jnp.exp(l0 - lm), jnp.exp(l1 - lm), jnp.exp(l2 - lm)
    den = e0 + e1 + e2
    o0, o1, o2 = (by_class(o, s) for o, s in zip(o_refs, order_refs[3:]))
    oa = (e0 / den) * o0 + (e1 / den) * o1 + (e2 / den) * o2
    br_a = jnp.dot(oa.astype(BF16), w_br_ref[:GROUP_W, :], preferred_element_type=F32)
    br_b = jnp.dot(ob_ref[rows], w_br_ref[GROUP_W:, :], preferred_element_type=F32)
    ga = jax.nn.sigmoid(jnp.dot(h1, w_gate_ref[:, :D_MODEL], preferred_element_type=F32)
                        + b_gate_ref[:, :D_MODEL])
    merged = ga * br_a
    gb = jax.nn.sigmoid(jnp.dot(h1, w_gate_ref[:, D_MODEL:], preferred_element_type=F32)
                        + b_gate_ref[:, D_MODEL:])
    merged = merged + gb * br_b
    x1 = x + jnp.dot(merged.astype(BF16), w_out_ref[...], preferred_element_type=F32)

    h2 = _rms(x1, g_x_ref[...]).astype(BF16)
    q = jnp.dot(h2, w_cq_ref[...], preferred_element_type=F32).astype(BF16)
    heads = []
    for h in range(X_HEADS):
        cols = slice(h * X_HEAD_DIM, (h + 1) * X_HEAD_DIM)
        kh = kv_ref[:, cols]
        vh = kv_ref[:, X_WIDTH + h * X_HEAD_DIM:X_WIDTH + (h + 1) * X_HEAD_DIM]
        s = lax.dot_general(q[:, cols], kh, (((1,), (1,)), ((), ())),
                            preferred_element_type=F32) * (X_HEAD_DIM ** -0.5)
        m = jnp.max(s, axis=-1, keepdims=True)
        e = jnp.exp(s - m)
        p = e / jnp.sum(e, axis=-1, keepdims=True)
        heads.append(jnp.dot(p.astype(BF16), vh, preferred_element_type=F32))
    o = jnp.concatenate(heads, axis=1).astype(BF16)
    x2 = x1 + jnp.dot(o, w_co_ref[...], preferred_element_type=F32)
    x2_ref[rows] = x2

    h3 = _rms(x2, g_ffn_ref[...])
    h3_hi = h3.astype(BF16)
    packed = lax.bitcast_convert_type(_pack_bf16_pairs(h3_hi.astype(F32)), jnp.int32)
    slab_w = packed.shape[1] // len(hp_refs)
    for n, hp_ref in enumerate(hp_refs):
        hp_ref[rows] = packed[:, n * slab_w:(n + 1) * slab_w]
    h3_lo = (h3 - h3_hi.astype(F32)).astype(BF16)
    by_hi = jnp.dot(h3_hi, w_rt_ref[...], preferred_element_type=F32)
    logits = (by_hi[:, :ROUTE_W] + by_hi[:, ROUTE_W:]
              + jnp.dot(h3_lo, w_rt_ref[:, :ROUTE_W], preferred_element_type=F32)) + b_rt_ref[...]
    lane = lax.broadcasted_iota(jnp.int32, (tm, ROUTE_W), 1)
    gmask = lane < N_GROUPS
    gl = jnp.where(gmask, logits, -jnp.inf)
    gmax = jnp.max(gl, axis=-1, keepdims=True)
    gidx = jnp.min(jnp.where(gl == gmax, lane, ROUTE_W), axis=-1, keepdims=True)
    pg_sel = 1.0 / jnp.sum(jnp.where(gmask, jnp.exp(logits - gmax), 0.0), axis=-1, keepdims=True)
    e_lo = EXPERT_LANE0 + gidx * EXPERTS_PER_GROUP
    emask = (lane >= e_lo) & (lane < e_lo + EXPERTS_PER_GROUP)
    el = jnp.where(emask, logits, -jnp.inf)
    emax1 = jnp.max(el, axis=-1, keepdims=True)
    i1 = jnp.min(jnp.where(el == emax1, lane, ROUTE_W), axis=-1, keepdims=True)
    el2 = jnp.where(lane == i1, -jnp.inf, el)
    emax2 = jnp.max(el2, axis=-1, keepdims=True)
    i2 = jnp.min(jnp.where(el2 == emax2, lane, ROUTE_W), axis=-1, keepdims=True)
    t2 = jnp.exp(emax2 - emax1)
    w1 = pg_sel / (1.0 + t2)
    w2 = pg_sel * t2 / (1.0 + t2)

    oh1 = lane == i1
    oh2 = lane == i2
    ohs = jnp.where(oh1 | oh2, 1.0, 0.0)
    before = jnp.dot(tri_ref[...], ohs.astype(BF16), preferred_element_type=F32) + carry_ref[...]
    rank1 = jnp.sum(jnp.where(oh1, before, 0.0), axis=-1, keepdims=True)
    rank2 = jnp.sum(jnp.where(oh2, before, 0.0), axis=-1, keepdims=True)
    carry_ref[...] = carry_ref[...] + jnp.sum(ohs, axis=0, keepdims=True)

    rec = jnp.where(lane == 0, (i1 - EXPERT_LANE0).astype(F32), 0.0)
    rec = jnp.where(lane == 1, (i2 - EXPERT_LANE0).astype(F32), rec)
    rec = jnp.where(lane == 2, w1, rec)
    rec = jnp.where(lane == 3, w2, rec)
    rec = jnp.where(lane == 4, rank1, rec)
    rec = jnp.where(lane == 5, rank2, rec)
    route_ref[rows] = rec


def _mix_kernel(x_ref, o0_ref, o1_ref, o2_ref, l0_ref, l1_ref, l2_ref, ob_ref, *refs):
    n_w = 12
    w_refs, refs = refs[:n_w], refs[n_w:]
    x2_ref, hp_refs, refs = refs[0], refs[1:1 + Y_SLABS], refs[1 + Y_SLABS:]
    route_ref, counts_ref, tri_ref, carry_ref = refs[:4]
    order_refs = refs[4:]
    sub = tri_ref.shape[0]

    @pl.when((pl.program_id(0) == 0) & (pl.program_id(1) == 0))
    def _():
        row = lax.broadcasted_iota(jnp.int32, (sub, sub), 0)
        col = lax.broadcasted_iota(jnp.int32, (sub, sub), 1)
        tri_ref[...] = jnp.where(row > col, 1.0, 0.0).astype(BF16)
        carry_ref[...] = jnp.zeros_like(carry_ref)

    for t0 in range(0, x_ref.shape[0], sub):
        _mix_rows(t0, sub, x_ref, (o0_ref, o1_ref, o2_ref), (l0_ref, l1_ref, l2_ref), ob_ref, *w_refs,
                  x2_ref, hp_refs, route_ref, tri_ref, carry_ref, order_refs)
    counts_ref[...] = jnp.broadcast_to(carry_ref[...], counts_ref.shape)


def _mix(x, o_groups, lse_groups, o_b, kvm, wts):
    batch, seq, _ = x.shape
    T = batch * seq
    tm = MIX_TILE
    steps = seq // tm
    mem_tokens = kvm.shape[0] // batch
    row = lambda w: pl.BlockSpec((tm, w), lambda b, i: (b * steps + i, 0))
    by_class = [pl.BlockSpec((None, r, tm // r, GROUP_W), lambda b, i: (b, 0, i, 0)) for _, r in A_CONFIGS]
    in_specs = ([pl.BlockSpec((None, tm, D_MODEL), lambda b, i: (b, i, 0))] + by_class + by_class
                + [pl.BlockSpec((None, None, tm, GROUP_W), lambda b, i: (b, 0, i, 0)),
                   pl.BlockSpec((mem_tokens, 2 * X_WIDTH), lambda b, i: (b, 0))]
                + [_const_spec(w.shape) for w in wts])
    return pl.pallas_call(
        _mix_kernel,
        grid=(batch, steps),
        in_specs=in_specs,
        out_specs=[row(D_MODEL)] + [row(D_MODEL // 2 // Y_SLABS)] * Y_SLABS
        + [row(ROUTE_W), pl.BlockSpec((8, ROUTE_W), lambda b, i: (0, 0))],
        out_shape=[jax.ShapeDtypeStruct((T, D_MODEL), F32)]
        + [jax.ShapeDtypeStruct((T, D_MODEL // 2 // Y_SLABS), jnp.int32)] * Y_SLABS
        + [jax.ShapeDtypeStruct((T, ROUTE_W), F32),
                   jax.ShapeDtypeStruct((8, ROUTE_W), F32)],
        scratch_shapes=[pltpu.VMEM((MIX_SUB, MIX_SUB), BF16), pltpu.VMEM((1, ROUTE_W), F32)]
        + [pltpu.VMEM((GROUP_W // LANES, tm, LANES), F32)] * (2 * A_GROUPS),
        compiler_params=pltpu.CompilerParams(dimension_semantics=("arbitrary",) * 2,
                                             vmem_limit_bytes=VMEM_LIMIT),
        name="mix",
    )(x, *o_groups, *lse_groups, o_b, kvm, *wts)


def _pack_bf16_pairs(x):
    w = x.shape[1] // 2
    lo = lax.bitcast_convert_type(x[:, :w], jnp.uint32) >> 16
    hi = lax.bitcast_convert_type(x[:, w:], jnp.uint32) & jnp.uint32(0xFFFF0000)
    return lo | hi


def _unpack_bf16_pairs(u):
    lo = lax.bitcast_convert_type(u << 16, F32)
    hi = lax.bitcast_convert_type(u & jnp.uint32(0xFFFF0000), F32)
    return jnp.concatenate([lo, hi], axis=1)


def _scatter_rows(rows, dest, n_rows):
    T, width = rows.shape
    mesh = plsc.VectorSubcoreMesh(core_axis_name="core", subcore_axis_name="subcore")

    @pl.kernel(out_type=jax.ShapeDtypeStruct((n_rows, width), rows.dtype), mesh=mesh, scratch_types=[])
    def scatter_kernel(rows_hbm, idx0_hbm, idx1_hbm, out_hbm):
        def body(rows_vmem, idx0_vmem, idx1_vmem):
            pltpu.sync_copy(rows_vmem, out_hbm.at[idx0_vmem.at[0]])
            pltpu.sync_copy(rows_vmem, out_hbm.at[idx1_vmem.at[0]])

        idx_spec = pl.BlockSpec((1, GATHER_WINDOW), index_map=lambda i: (0, i))
        pltpu.emit_pipeline(
            body,
            grid=(T // GATHER_WINDOW,),
            in_specs=[pl.BlockSpec((GATHER_WINDOW, width), index_map=lambda i: (i, 0)), idx_spec, idx_spec],
            out_specs=[],
            core_axis_name="subcore",
            dimension_semantics=(pltpu.PARALLEL,),
        )(rows_hbm, idx0_hbm, idx1_hbm)

    return scatter_kernel(rows, dest[0].reshape(1, T), dest[1].reshape(1, T))


def _expert_block(rows, n_valid, xs_refs, wgu_ref, wdn_ref, ys_refs):
    packed_in = jnp.concatenate([r[rows] for r in xs_refs], axis=1)
    row = lax.broadcasted_iota(jnp.int32, packed_in.shape, 0)
    packed_in = jnp.where(row < n_valid, packed_in, 0)
    xb = _unpack_bf16_pairs(lax.bitcast_convert_type(packed_in, jnp.uint32)).astype(BF16)
    gu = jnp.dot(xb, wgu_ref[...], preferred_element_type=F32)
    gate, up = gu[:, :EXPERT_FF], gu[:, EXPERT_FF:]
    act = (gate * jax.nn.sigmoid(gate) * up).astype(BF16)
    y = jnp.dot(act, wdn_ref[...], preferred_element_type=F32)
    packed = lax.bitcast_convert_type(_pack_bf16_pairs(y.astype(BF16).astype(F32)), jnp.int32)
    w = packed.shape[1] // len(ys_refs)
    for n, ys_ref in enumerate(ys_refs):
        ys_ref[rows] = packed[:, n * w:(n + 1) * w]


def _expert_kernel(be_ref, nv_ref, *refs):
    xs_refs, w_refs, ys_refs = refs[:Y_SLABS], refs[Y_SLABS:-Y_SLABS], refs[-Y_SLABS:]
    for j in range(BLOCKS_PER_STEP):
        _expert_block(slice(j * MOE_BLOCK, (j + 1) * MOE_BLOCK),
                      nv_ref[pl.program_id(0) * BLOCKS_PER_STEP + j],
                      xs_refs, w_refs[2 * j], w_refs[2 * j + 1], ys_refs)


def _experts(xs_slabs, block_e, block_valid, w_gu, w_down):
    P, slab_w = xs_slabs[0].shape
    n_blocks = P // MOE_BLOCK
    bps = BLOCKS_PER_STEP
    slab_spec = pl.BlockSpec((bps * MOE_BLOCK, slab_w), lambda b, be, nv: (b, 0))
    w_specs, w_args = [], []
    for j in range(bps):
        w_specs += [
            pl.BlockSpec((None, D_MODEL, 2 * EXPERT_FF), lambda b, be, nv, j=j: (be[b * bps + j], 0, 0)),
            pl.BlockSpec((None, EXPERT_FF, D_MODEL), lambda b, be, nv, j=j: (be[b * bps + j], 0, 0))]
        w_args += [w_gu, w_down]
    grid_spec = pltpu.PrefetchScalarGridSpec(
        num_scalar_prefetch=2,
        grid=(n_blocks // bps,),
        in_specs=[slab_spec] * Y_SLABS + w_specs,
        out_specs=[slab_spec] * Y_SLABS,
    )
    return pl.pallas_call(
        _expert_kernel,
        grid_spec=grid_spec,
        out_shape=[jax.ShapeDtypeStruct((P, slab_w), jnp.int32)] * Y_SLABS,
        compiler_params=pltpu.CompilerParams(dimension_semantics=("arbitrary",),
                                             vmem_limit_bytes=VMEM_LIMIT),
        name="experts",
    )(block_e, block_valid, *xs_slabs, *w_args)


def _gather_rows(table, indices):
    n, width = indices.shape[0], table.shape[1]
    mesh = plsc.VectorSubcoreMesh(core_axis_name="core", subcore_axis_name="subcore")

    @pl.kernel(out_type=jax.ShapeDtypeStruct((n, width), table.dtype), mesh=mesh)
    def gather_kernel(table_hbm, idx_hbm, out_hbm):
        def body(idx_vmem, out_vmem):
            pltpu.sync_copy(table_hbm.at[idx_vmem.at[0]], out_vmem)

        pltpu.emit_pipeline(
            body,
            grid=(n // GATHER_WINDOW,),
            in_specs=[pl.BlockSpec((1, GATHER_WINDOW), index_map=lambda i: (0, i))],
            out_specs=[pl.BlockSpec((GATHER_WINDOW, width), index_map=lambda i: (i, 0))],
            core_axis_name="subcore",
            dimension_semantics=(pltpu.PARALLEL,),
        )(idx_hbm, out_hbm)

    return gather_kernel(table, indices.reshape(1, n))


def _combine_kernel(x2_ref, route_ref, g_ref, *refs):
    y_refs, out_ref = refs[:-1], refs[-1]
    route = route_ref[...]

    def expert_rows(slabs):
        packed = jnp.concatenate([s[...] for s in slabs], axis=1)
        return _unpack_bf16_pairs(lax.bitcast_convert_type(packed, jnp.uint32))

    moe = expert_rows(y_refs[:Y_SLABS]) * route[:, 2:3] + expert_rows(y_refs[Y_SLABS:]) * route[:, 3:4]
    out_ref[...] = _rms(x2_ref[...] + moe, g_ref[...])


def _combine(x2, route, g_final, dest, ys_slabs):
    T = x2.shape[0]
    tm = MIX_TILE
    n_steps = T // tm
    rows = [_gather_rows(ys, dest.reshape(-1)) for ys in ys_slabs]
    slab_w = rows[0].shape[1]
    first = [pl.BlockSpec((tm, slab_w), lambda i: (i, 0))] * Y_SLABS
    second = [pl.BlockSpec((tm, slab_w), lambda i: (i + n_steps, 0))] * Y_SLABS
    return pl.pallas_call(
        _combine_kernel,
        grid=(n_steps,),
        in_specs=[pl.BlockSpec((tm, D_MODEL), lambda i: (i, 0)),
                  pl.BlockSpec((tm, ROUTE_W), lambda i: (i, 0)),
                  _const_spec((1, D_MODEL))] + first + second,
        out_specs=pl.BlockSpec((tm, D_MODEL), lambda i: (i, 0)),
        out_shape=jax.ShapeDtypeStruct((T, D_MODEL), F32),
        compiler_params=pltpu.CompilerParams(dimension_semantics=("arbitrary",)),
        name="combine",
    )(x2, route, g_final, *rows, *rows)


def _routing_tables(route, counts_rec, n_tokens):
    n_slots = 2 * n_tokens
    n_blocks = n_slots // MOE_BLOCK + N_EXPERTS
    n_rows = n_blocks * MOE_BLOCK
    counts = counts_rec[0, EXPERT_LANE0:EXPERT_LANE0 + N_EXPERTS].astype(jnp.int32)
    padded = (counts + MOE_BLOCK - 1) // MOE_BLOCK * MOE_BLOCK
    pends = jnp.cumsum(padded)
    pstarts = pends - padded
    expert = route[:, 0:2].astype(jnp.int32)
    rank = route[:, 4:6].astype(jnp.int32)
    lookup = lambda table, idx: jnp.sum(jnp.where(idx[..., None] == jnp.arange(N_EXPERTS), table, 0), axis=-1)
    count_le = lambda sorted_vals, q: jnp.sum(sorted_vals[None, :] <= q[:, None], axis=-1).astype(jnp.int32)
    dest = (lookup(pstarts, expert) + rank).T
    block_row0 = jnp.arange(n_blocks, dtype=jnp.int32) * MOE_BLOCK
    owner = count_le(pends, block_row0)
    block_e = jnp.minimum(owner, N_EXPERTS - 1)
    block_valid = jnp.clip(lookup(pstarts + counts, owner) - block_row0, 0, MOE_BLOCK)
    block_valid = jnp.where(owner < N_EXPERTS, block_valid, 0).astype(jnp.int32)
    return dest.astype(jnp.int32), block_e, block_valid, n_rows


def _encoder_group(x, mem, w):
    batch, seq, _ = x.shape
    T = batch * seq
    *qkv_groups, qkv_win = _proj(x, w["g_mix"], w["w_in"])

    slopes_a = _alibi_slopes(A_HEADS).reshape(A_GROUPS, A_HEADS_PER_GROUP)
    o_groups, lse_groups = [], []
    for g, (window, r) in enumerate(A_CONFIGS):
        o, lse = _banded_attention(qkv_groups[g], half_w=window // (2 * r), kv_width=GROUP_W,
                                   slopes=slopes_a[g] * np.float32(r))
        o_groups.append(o)
        lse_groups.append(lse)
    (o_b,) = _banded_attention(qkv_win.reshape(batch, 1, seq, WIN_W), half_w=B_HALF_WINDOW,
                               kv_width=KV2_W, slopes=_alibi_slopes(B_HEADS), sink=w["sink_b"],
                               want_lse=False)

    kvm = _memkv(mem.reshape(-1, D_MODEL), w["g_mem"], w["w_ckv"])
    mix_w = [w[k] for k in ("g_mix", "w_gate", "b_gate", "w_branch", "w_out", "g_xattn", "w_cq",
                            "w_co", "g_ffn", "w_route", "b_route")]
    x2, *h_slabs, route, counts_rec = _mix(x, o_groups, lse_groups, o_b, kvm, mix_w)

    dest, block_e, block_valid, n_rows = _routing_tables(route, counts_rec, T)
    xs_slabs = [_scatter_rows(h, dest, n_rows) for h in h_slabs]
    ys_slabs = _experts(xs_slabs, block_e, block_valid, w["w_gu"], w["w_down"])
    y = _combine(x2, route, w["g_final"], dest, ys_slabs)
    return y.reshape(batch, seq, D_MODEL)


def _prep_weights(g_mix, w_in, sink_b, w_gate, b_gate, w_branch, w_out, g_xattn, g_mem, w_cq, w_ckv,
                  w_co, g_ffn, w_rg, b_rg, w_re, b_re, w_gu, w_down, g_final):
    scale = HEAD_DIM ** -0.5
    aw = A_WIDTH
    qa, ka, va = w_in[:, :aw] * scale, w_in[:, aw:2 * aw], w_in[:, 2 * aw:3 * aw]
    qb = w_in[:, 3 * aw:3 * aw + B_Q] * scale
    kb = w_in[:, 3 * aw + B_Q:3 * aw + B_Q + B_KV]
    vb = w_in[:, 3 * aw + B_Q + B_KV:]
    twice = lambda t: jnp.repeat(t.reshape(D_MODEL, B_KV_HEADS, 1, HEAD_DIM), 2, axis=2).reshape(D_MODEL, KV2_W)
    group = lambda t, g: t[:, g * GROUP_W:(g + 1) * GROUP_W]
    cols = [group(t, g) for g in range(A_GROUPS) for t in (qa, ka, va)] + [qb, twice(kb), twice(vb)]
    w_in_x = jnp.concatenate(cols, axis=1).astype(BF16)
    w_route = jnp.zeros((D_MODEL, ROUTE_W), F32)
    w_route = w_route.at[:, :N_GROUPS].set(w_rg).at[:, EXPERT_LANE0:EXPERT_LANE0 + N_EXPERTS].set(w_re)
    b_route = jnp.zeros((1, ROUTE_W), F32)
    b_route = b_route.at[0, :N_GROUPS].set(b_rg).at[0, EXPERT_LANE0:EXPERT_LANE0 + N_EXPERTS].set(b_re)
    w_route_hi = w_route.astype(BF16)
    w_route = jnp.concatenate([w_route_hi, (w_route - w_route_hi.astype(F32)).astype(BF16)], axis=1)
    vec = lambda v: v.reshape(1, -1).astype(F32)
    return dict(
        g_mix=vec(g_mix), w_in=w_in_x, sink_b=sink_b.astype(F32),
        w_gate=w_gate.astype(BF16), b_gate=vec(b_gate), w_branch=w_branch.astype(BF16),
        w_out=w_out.astype(BF16), g_xattn=vec(g_xattn), g_mem=vec(g_mem), w_cq=w_cq.astype(BF16),
        w_ckv=w_ckv.astype(BF16), w_co=w_co.astype(BF16), g_ffn=vec(g_ffn),
        w_route=w_route, b_route=b_route, w_gu=w_gu.astype(BF16), w_down=w_down.astype(BF16),
        g_final=vec(g_final))


def kernel(x_prompt, x_sample, mem_prompt, mem_sample, g_mix, w_in, sink_b, w_gate, b_gate, w_branch,
           w_out, g_xattn, g_mem, w_cq, w_ckv, w_co, g_ffn, w_rg, b_rg, w_re, b_re, w_gu, w_down,
           g_final):
    assert g_mix.shape[0] == 1, "single-layer encoder"
    w = _prep_weights(g_mix[0], w_in[0], sink_b[0], w_gate[0], b_gate[0], w_branch[0], w_out[0],
                      g_xattn[0], g_mem[0], w_cq[0], w_ckv[0], w_co[0], g_ffn[0], w_rg[0], b_rg[0],
                      w_re[0], b_re[0], w_gu[0], w_down[0], g_final)
    return (_encoder_group(x_prompt, mem_prompt, w), _encoder_group(x_sample, mem_sample, w))
```

```python
import functools

import numpy as np
import jax
import jax.numpy as jnp
from jax import lax
from jax.experimental import pallas as pl
from jax.experimental.pallas import tpu as pltpu
from jax.experimental.pallas import tpu_sc as plsc

F32 = jnp.float32
BF16 = jnp.bfloat16

D_MODEL = 1024
HEAD_DIM = 64
A_CONFIGS = ((128, 1), (512, 4), (2048, 16))
A_GROUPS = 3
A_HEADS_PER_GROUP = 8
A_HEADS = A_GROUPS * A_HEADS_PER_GROUP
A_WIDTH = A_HEADS * HEAD_DIM
B_HEADS = 8
B_KV_HEADS = 2
B_HALF_WINDOW = 128
B_Q = B_HEADS * HEAD_DIM
B_KV = B_KV_HEADS * HEAD_DIM
X_HEADS = 4
X_HEAD_DIM = 128
X_WIDTH = X_HEADS * X_HEAD_DIM
N_GROUPS = 4
EXPERTS_PER_GROUP = 8
N_EXPERTS = N_GROUPS * EXPERTS_PER_GROUP
EXPERT_FF = 512
MOE_BLOCK = 256
EPS = 1e-6
NEG = -1e30

LANES = 128
GROUP_W = A_HEADS_PER_GROUP * HEAD_DIM
PAIR_W = 2 * HEAD_DIM
N_PAIRS = GROUP_W // PAIR_W
QKV_W = 3 * GROUP_W
KV2_W = 2 * B_KV
WIN_W = B_Q + 2 * KV2_W
PROJ_W = A_GROUPS * QKV_W + WIN_W
ROUTE_W = LANES
EXPERT_LANE0 = 32
VMEM_LIMIT = 52 * 1024 * 1024

ROW_TILE = 256
PROJ_TILE = 512
MIX_TILE = 512
MIX_SUB = 256
Q_STEPS = 8
Q_UNROLL = 4
GATHER_WINDOW = 128
SC_CORES = 2
BLOCKS_PER_STEP = 2
Y_SLABS = 2
def _rms(x, g):
    ms = jnp.mean(x * x, axis=-1, keepdims=True)
    return x * lax.rsqrt(ms + EPS) * g


def _alibi_slopes(n):
    return 2.0 ** (-8.0 * np.arange(1, n + 1, dtype=np.float32) / n)


def _const_spec(shape):
    return pl.BlockSpec(shape, lambda *_: (0,) * len(shape), pipeline_mode=pl.Buffered(1))


def _proj_kernel(x_ref, g_ref, w_ref, *refs):
    o_refs, win_ref, h_ref = refs[:A_GROUPS], refs[A_GROUPS], refs[A_GROUPS + 1]
    tm = x_ref.shape[0]
    h32 = _rms(x_ref[...], g_ref[...])
    h_nat = h32.astype(BF16)
    n_slabs = h_ref.shape[0]
    for s in range(n_slabs):
        h_ref[s] = h32[:, s * LANES:(s + 1) * LANES]
    for g, (_, r) in enumerate(A_CONFIGS):
        n = tm // r
        if r == 1:
            h = h_nat
        else:
            h = jnp.concatenate(
                [jnp.concatenate([h_ref[s, pl.ds(c, n, stride=r), :] for c in range(r)], axis=0)
                 for s in range(n_slabs)], axis=1).astype(BF16)
        for j in range(3):
            cols = slice(g * QKV_W + j * GROUP_W, g * QKV_W + (j + 1) * GROUP_W)
            res = jnp.dot(h, w_ref[:, cols], preferred_element_type=F32).astype(BF16)
            for c in range(r):
                o_refs[g][c, :, j * GROUP_W:(j + 1) * GROUP_W] = res[c * n:(c + 1) * n]
    for j in range(WIN_W // GROUP_W):
        cols = slice(A_GROUPS * QKV_W + j * GROUP_W, A_GROUPS * QKV_W + (j + 1) * GROUP_W)
        win_ref[:, j * GROUP_W:(j + 1) * GROUP_W] = jnp.dot(
            h_nat, w_ref[:, cols], preferred_element_type=F32).astype(BF16)


def _proj(x, g, w):
    batch, seq, _ = x.shape
    tm = PROJ_TILE
    out_specs = [pl.BlockSpec((None, r, tm // r, QKV_W), lambda b, i: (b, 0, i, 0)) for _, r in A_CONFIGS]
    out_shape = [jax.ShapeDtypeStruct((batch, r, seq // r, QKV_W), BF16) for _, r in A_CONFIGS]
    out_specs.append(pl.BlockSpec((None, tm, WIN_W), lambda b, i: (b, i, 0)))
    out_shape.append(jax.ShapeDtypeStruct((batch, seq, WIN_W), BF16))
    return pl.pallas_call(
        _proj_kernel,
        grid=(batch, seq // tm),
        in_specs=[pl.BlockSpec((None, tm, D_MODEL), lambda b, i: (b, i, 0)),
                  _const_spec((1, D_MODEL)),
                  _const_spec((D_MODEL, PROJ_W))],
        out_specs=out_specs,
        out_shape=out_shape,
        scratch_shapes=[pltpu.VMEM((D_MODEL // LANES, tm, LANES), F32)],
        compiler_params=pltpu.CompilerParams(dimension_semantics=("arbitrary",) * 2,
                                             vmem_limit_bytes=VMEM_LIMIT),
        name="proj",
    )(x, g, w)


def _attn_kernel(*refs, qb, kb, q_steps, m_len, half_w, offsets, kv_shared, has_sink, want_lse):
    refs = list(refs)
    sink_ref = refs.pop(0) if has_sink else None
    bias_ref, q_ref, k_ref, v_ref, o_ref = refs[:5]
    lse_ref = refs[5] if want_lse else None
    lo_q = lax.broadcasted_iota(jnp.int32, (qb, PAIR_W), 1) < HEAD_DIM
    first_head = lax.broadcasted_iota(jnp.int32, (2 * qb, 1), 0) < qb
    zeros_q = jnp.zeros((qb, PAIR_W), BF16)

    def q_block(it, carry):
        cc, qi = it // q_steps, it % q_steps
        gi = pl.program_id(2) * q_steps + qi
        ks = pl.multiple_of(jnp.clip(gi * qb - half_w, 0, m_len - kb), 16)
        off = gi * qb - ks
        var = sum(jnp.where(off == o, n, 0) for n, o in enumerate(offsets))
        rows = pl.ds(pl.multiple_of(qi * qb, qb), qb)
        for j in range(N_PAIRS):
            cols = slice(j * PAIR_W, (j + 1) * PAIR_W)
            jc = (j // 2) if kv_shared else j
            kcols = slice(jc * PAIR_W, (jc + 1) * PAIR_W)
            qp = q_ref[cc, rows, cols]
            q_st = jnp.concatenate([jnp.where(lo_q, qp, zeros_q), jnp.where(lo_q, zeros_q, qp)], axis=0)
            s = lax.dot_general(q_st, k_ref[cc, pl.ds(ks, kb), kcols], (((1,), (1,)), ((), ())),
                                preferred_element_type=F32) + bias_ref[var, j]
            m = jnp.max(s, axis=-1, keepdims=True)
            if has_sink:
                sk = jnp.where(first_head, sink_ref[2 * j], sink_ref[2 * j + 1])
                m = jnp.maximum(m, sk)
            e = jnp.exp(s - m)
            den = jnp.sum(e, axis=-1, keepdims=True)
            if has_sink:
                den = den + jnp.exp(sk - m)
            o2 = jnp.dot(e.astype(BF16), v_ref[cc, pl.ds(ks, kb), kcols], preferred_element_type=F32) / den
            o_ref[cc, rows, cols] = jnp.where(lo_q, o2[:qb], o2[qb:]).astype(BF16)
            if want_lse:
                l2 = m + jnp.log(den)
                lse_ref[cc, rows, cols] = jnp.where(lo_q, l2[:qb], l2[qb:])
        return carry

    n_iter = q_ref.shape[0] * q_steps
    lax.fori_loop(0, n_iter, q_block, 0, unroll=min(Q_UNROLL, n_iter))


def _bias_tables(offsets, slopes, qb, kb, half_w):
    rel = jnp.arange(qb, dtype=jnp.int32)[:, None] - jnp.arange(kb, dtype=jnp.int32)[None, :]
    dist = jnp.abs(rel[None] + jnp.asarray(offsets, jnp.int32)[:, None, None])
    bias = -jnp.asarray(slopes, F32)[None, :, None, None] * dist.astype(F32)[:, None]
    bias = jnp.where((dist <= half_w)[:, None], bias, NEG)
    return bias.reshape(len(offsets), N_PAIRS, 2 * qb, kb)


def _banded_attention(qkv, *, half_w, kv_width, slopes, sink=None, want_lse=True):
    batch, r, m_len, _ = qkv.shape
    qb = min(2 * half_w, m_len)
    kb = min(qb + 2 * half_w, m_len)
    nq = m_len // qb
    q_steps = min(Q_STEPS, nq)
    cb = min(r, Q_STEPS // q_steps)
    kv_shared = kv_width != GROUP_W
    has_sink = sink is not None
    k_block = GROUP_W // kv_width
    offsets = sorted({i * qb - min(max(i * qb - half_w, 0), m_len - kb) for i in range(nq)})
    bias = _bias_tables(offsets, slopes, qb, kb, half_w)

    in_specs = []
    args = []
    if has_sink:
        in_specs.append(pl.BlockSpec(memory_space=pltpu.SMEM))
        args.append(sink)
    in_specs += [
        _const_spec(bias.shape),
        pl.BlockSpec((None, cb, q_steps * qb, GROUP_W), lambda b, c, i: (b, c, i, 0)),
        pl.BlockSpec((None, cb, m_len, kv_width), lambda b, c, i: (b, c, 0, k_block)),
        pl.BlockSpec((None, cb, m_len, kv_width), lambda b, c, i: (b, c, 0, k_block + 1)),
    ]
    args += [bias, qkv, qkv, qkv]
    out_spec = pl.BlockSpec((None, cb, q_steps * qb, GROUP_W), lambda b, c, i: (b, c, i, 0))
    out_specs = [out_spec]
    out_shape = [jax.ShapeDtypeStruct((batch, r, m_len, GROUP_W), BF16)]
    if want_lse:
        out_specs.append(out_spec)
        out_shape.append(jax.ShapeDtypeStruct((batch, r, m_len, GROUP_W), F32))
    kern = functools.partial(_attn_kernel, qb=qb, kb=kb, q_steps=q_steps, m_len=m_len, half_w=half_w,
                             offsets=tuple(offsets), kv_shared=kv_shared, has_sink=has_sink,
                             want_lse=want_lse)
    return pl.pallas_call(
        kern,
        grid=(batch, r // cb, nq // q_steps),
        in_specs=in_specs,
        out_specs=out_specs,
        out_shape=out_shape,
        compiler_params=pltpu.CompilerParams(dimension_semantics=("arbitrary",) * 3,
                                             vmem_limit_bytes=VMEM_LIMIT),
        name=f"attn_r{r}_w{half_w}",
    )(*args)


def _memkv_kernel(m_ref, g_ref, w_ref, o_ref):
    h = _rms(m_ref[...], g_ref[...]).astype(BF16)
    o_ref[...] = jnp.dot(h, w_ref[...], preferred_element_type=F32).astype(BF16)


def _memkv(mem2d, g, w):
    R = mem2d.shape[0]
    return pl.pallas_call(
        _memkv_kernel,
        grid=(R // ROW_TILE,),
        in_specs=[pl.BlockSpec((ROW_TILE, D_MODEL), lambda i: (i, 0)),
                  _const_spec((1, D_MODEL)),
                  _const_spec((D_MODEL, 2 * X_WIDTH))],
        out_specs=pl.BlockSpec((ROW_TILE, 2 * X_WIDTH), lambda i: (i, 0)),
        out_shape=jax.ShapeDtypeStruct((R, 2 * X_WIDTH), BF16),
        compiler_params=pltpu.CompilerParams(dimension_semantics=("arbitrary",)),
        name="memkv",
    )(mem2d, g, w)


def _token_order(src_ref, dst_ref, m0, n):
    r = src_ref.shape[0]
    if r == 1:
        return src_ref[0, m0:m0 + n].astype(F32)
    n_slabs = dst_ref.shape[0]
    for c in range(r):
        rows = src_ref[c, m0:m0 + n].astype(F32)
        for s in range(n_slabs):
            dst_ref[s, pl.ds(m0 * r + c, n, stride=r), :] = rows[:, s * LANES:(s + 1) * LANES]
    return jnp.concatenate([dst_ref[s, m0 * r:(m0 + n) * r] for s in range(n_slabs)], axis=1)


def _mix_rows(t0, tm, x_ref, o_refs, l_refs, ob_ref, kv_ref,
              g_mix_ref, w_gate_ref, b_gate_ref, w_br_ref, w_out_ref,
              g_x_ref, w_cq_ref, w_co_ref, g_ffn_ref, w_rt_ref, b_rt_ref,
              x2_ref, hp_refs, route_ref, tri_ref, carry_ref, order_refs):
    rows = slice(t0, t0 + tm)
    x = x_ref[rows]
    h1 = _rms(x, g_mix_ref[...]).astype(BF16)

    by_class = lambda ref, scratch: _token_order(ref, scratch, t0 // ref.shape[0], tm // ref.shape[0])
    l0, l1, l2 = (by_class(l, s) for l, s in zip(l_refs, order_refs[:3]))
    lm = jnp.maximum(jnp.maximum(l0, l1), l2)
    e0, e1, e2 = jnp.exp(l0 - lm), jnp.exp(l1 - lm), jnp.exp(l2 - lm)
    den = e0 + e1 + e2
    o0, o1, o2 = (by_class(o, s) for o, s in zip(o_refs, order_refs[3:]))
    oa = (e0 / den) * o0 + (e1 / den) * o1 + (e2 / den) * o2
    br_a = jnp.dot(oa.astype(BF16), w_br_ref[:GROUP_W, :], preferred_element_type=F32)
    br_b = jnp.dot(ob_ref[rows], w_br_ref[GROUP_W:, :], preferred_element_type=F32)
    ga = jax.nn.sigmoid(jnp.dot(h1, w_gate_ref[:, :D_MODEL], preferred_element_type=F32)
                        + b_gate_ref[:, :D_MODEL])
    merged = ga * br_a
    gb = jax.nn.sigmoid(jnp.dot(h1, w_gate_ref[:, D_MODEL:], preferred_element_type=F32)
                        + b_gate_ref[:, D_MODEL:])
    merged = merged + gb * br_b
    x1 = x + jnp.dot(merged.astype(BF16), w_out_ref[...], preferred_element_type=F32)

    h2 = _rms(x1, g_x_ref[...]).astype(BF16)
    q = jnp.dot(h2, w_cq_ref[...], preferred_element_type=F32).astype(BF16)
    heads = []
    for h in range(X_HEADS):
        cols = slice(h * X_HEAD_DIM, (h + 1) * X_HEAD_DIM)
        kh = kv_ref[:, cols]
        vh = kv_ref[:, X_WIDTH + h * X_HEAD_DIM:X_WIDTH + (h + 1) * X_HEAD_DIM]
        s = lax.dot_general(q[:, cols], kh, (((1,), (1,)), ((), ())),
                            preferred_element_type=F32) * (X_HEAD_DIM ** -0.5)
        m = jnp.max(s, axis=-1, keepdims=True)
        e = jnp.exp(s - m)
        p = e / jnp.sum(e, axis=-1, keepdims=True)
        heads.append(jnp.dot(p.astype(BF16), vh, preferred_element_type=F32))
    o = jnp.concatenate(heads, axis=1).astype(BF16)
    x2 = x1 + jnp.dot(o, w_co_ref[...], preferred_element_type=F32)
    x2_ref[rows] = x2

    h3 = _rms(x2, g_ffn_ref[...])
    h3_hi = h3.astype(BF16)
    packed = lax.bitcast_convert_type(_pack_bf16_pairs(h3_hi.astype(F32)), jnp.int32)
    slab_w = packed.shape[1] // len(hp_refs)
    for n, hp_ref in enumerate(hp_refs):
        hp_ref[rows] = packed[:, n * slab_w:(n + 1) * slab_w]
    h3_lo = (h3 - h3_hi.astype(F32)).astype(BF16)
    by_hi = jnp.dot(h3_hi, w_rt_ref[...], preferred_element_type=F32)
    logits = (by_hi[:, :ROUTE_W] + by_hi[:, ROUTE_W:]
              + jnp.dot(h3_lo, w_rt_ref[:, :ROUTE_W], preferred_element_type=F32)) + b_rt_ref[...]
    lane = lax.broadcasted_iota(jnp.int32, (tm, ROUTE_W), 1)
    gmask = lane < N_GROUPS
    gl = jnp.where(gmask, logits, -jnp.inf)
    gmax = jnp.max(gl, axis=-1, keepdims=True)
    gidx = jnp.min(jnp.where(gl == gmax, lane, ROUTE_W), axis=-1, keepdims=True)
    pg_sel = 1.0 / jnp.sum(jnp.where(gmask, jnp.exp(logits - gmax), 0.0), axis=-1, keepdims=True)
    e_lo = EXPERT_LANE0 + gidx * EXPERTS_PER_GROUP
    emask = (lane >= e_lo) & (lane < e_lo + EXPERTS_PER_GROUP)
    el = jnp.where(emask, logits, -jnp.inf)
    emax1 = jnp.max(el, axis=-1, keepdims=True)
    i1 = jnp.min(jnp.where(el == emax1, lane, ROUTE_W), axis=-1, keepdims=True)
    el2 = jnp.where(lane == i1, -jnp.inf, el)
    emax2 = jnp.max(el2, axis=-1, keepdims=True)
    i2 = jnp.min(jnp.where(el2 == emax2, lane, ROUTE_W), axis=-1, keepdims=True)
    t2 = jnp.exp(emax2 - emax1)
    w1 = pg_sel / (1.0 + t2)
    w2 = pg_sel * t2 / (1.0 + t2)

    oh1 = lane == i1
    oh2 = lane == i2
    ohs = jnp.where(oh1 | oh2, 1.0, 0.0)
    before = jnp.dot(tri_ref[...], ohs.astype(BF16), preferred_element_type=F32) + carry_ref[...]
    rank1 = jnp.sum(jnp.where(oh1, before, 0.0), axis=-1, keepdims=True)
    rank2 = jnp.sum(jnp.where(oh2, before, 0.0), axis=-1, keepdims=True)
    carry_ref[...] = carry_ref[...] + jnp.sum(ohs, axis=0, keepdims=True)

    rec = jnp.where(lane == 0, (i1 - EXPERT_LANE0).astype(F32), 0.0)
    rec = jnp.where(lane == 1, (i2 - EXPERT_LANE0).astype(F32), rec)
    rec = jnp.where(lane == 2, w1, rec)
    rec = jnp.where(lane == 3, w2, rec)
    rec = jnp.where(lane == 4, rank1, rec)
    rec = jnp.where(lane == 5, rank2, rec)
    route_ref[rows] = rec


def _mix_kernel(x_ref, o0_ref, o1_ref, o2_ref, l0_ref, l1_ref, l2_ref, ob_ref, *refs):
    n_w = 12
    w_refs, refs = refs[:n_w], refs[n_w:]
    x2_ref, hp_refs, refs = refs[0], refs[1:1 + Y_SLABS], refs[1 + Y_SLABS:]
    route_ref, counts_ref, tri_ref, carry_ref = refs[:4]
    order_refs = refs[4:]
    sub = tri_ref.shape[0]

    @pl.when((pl.program_id(0) == 0) & (pl.program_id(1) == 0))
    def _():
        row = lax.broadcasted_iota(jnp.int32, (sub, sub), 0)
        col = lax.broadcasted_iota(jnp.int32, (sub, sub), 1)
        tri_ref[...] = jnp.where(row > col, 1.0, 0.0).astype(BF16)
        carry_ref[...] = jnp.zeros_like(carry_ref)

    for t0 in range(0, x_ref.shape[0], sub):
        _mix_rows(t0, sub, x_ref, (o0_ref, o1_ref, o2_ref), (l0_ref, l1_ref, l2_ref), ob_ref, *w_refs,
                  x2_ref, hp_refs, route_ref, tri_ref, carry_ref, order_refs)
    counts_ref[...] = jnp.broadcast_to(carry_ref[...], counts_ref.shape)


def _mix(x, o_groups, lse_groups, o_b, kvm, wts):
    batch, seq, _ = x.shape
    T = batch * seq
    tm = MIX_TILE
    steps = seq // tm
    mem_tokens = kvm.shape[0] // batch
    row = lambda w: pl.BlockSpec((tm, w), lambda b, i: (b * steps + i, 0))
    by_class = [pl.BlockSpec((None, r, tm // r, GROUP_W), lambda b, i: (b, 0, i, 0)) for _, r in A_CONFIGS]
    in_specs = ([pl.BlockSpec((None, tm, D_MODEL), lambda b, i: (b, i, 0))] + by_class + by_class
                + [pl.BlockSpec((None, None, tm, GROUP_W), lambda b, i: (b, 0, i, 0)),
                   pl.BlockSpec((mem_tokens, 2 * X_WIDTH), lambda b, i: (b, 0))]
                + [_const_spec(w.shape) for w in wts])
    return pl.pallas_call(
        _mix_kernel,
        grid=(batch, steps),
        in_specs=in_specs,
        out_specs=[row(D_MODEL)] + [row(D_MODEL // 2 // Y_SLABS)] * Y_SLABS
        + [row(ROUTE_W), pl.BlockSpec((8, ROUTE_W), lambda b, i: (0, 0))],
        out_shape=[jax.ShapeDtypeStruct((T, D_MODEL), F32)]
        + [jax.ShapeDtypeStruct((T, D_MODEL // 2 // Y_SLABS), jnp.int32)] * Y_SLABS
        + [jax.ShapeDtypeStruct((T, ROUTE_W), F32),
                   jax.ShapeDtypeStruct((8, ROUTE_W), F32)],
        scratch_shapes=[pltpu.VMEM((MIX_SUB, MIX_SUB), BF16), pltpu.VMEM((1, ROUTE_W), F32)]
        + [pltpu.VMEM((GROUP_W // LANES, tm, LANES), F32)] * (2 * A_GROUPS),
        compiler_params=pltpu.CompilerParams(dimension_semantics=("arbitrary",) * 2,
                                             vmem_limit_bytes=VMEM_LIMIT),
        name="mix",
    )(x, *o_groups, *lse_groups, o_b, kvm, *wts)


def _pack_bf16_pairs(x):
    w = x.shape[1] // 2
    lo = lax.bitcast_convert_type(x[:, :w], jnp.uint32) >> 16
    hi = lax.bitcast_convert_type(x[:, w:], jnp.uint32) & jnp.uint32(0xFFFF0000)
    return lo | hi


def _unpack_bf16_pairs(u):
    lo = lax.bitcast_convert_type(u << 16, F32)
    hi = lax.bitcast_convert_type(u & jnp.uint32(0xFFFF0000), F32)
    return jnp.concatenate([lo, hi], axis=1)


def _scatter_rows(rows, dest, n_rows):
    T, width = rows.shape
    mesh = plsc.VectorSubcoreMesh(core_axis_name="core", subcore_axis_name="subcore")

    @pl.kernel(out_type=jax.ShapeDtypeStruct((n_rows, width), rows.dtype), mesh=mesh, scratch_types=[])
    def scatter_kernel(rows_hbm, idx0_hbm, idx1_hbm, out_hbm):
        def body(rows_vmem, idx0_vmem, idx1_vmem):
            pltpu.sync_copy(rows_vmem, out_hbm.at[idx0_vmem.at[0]])
            pltpu.sync_copy(rows_vmem, out_hbm.at[idx1_vmem.at[0]])

        per_core = T // GATHER_WINDOW // SC_CORES
        idx_spec = pl.BlockSpec((1, GATHER_WINDOW), index_map=lambda c, i: (0, c * per_core + i))
        pltpu.emit_pipeline(
            body,
            grid=(SC_CORES, per_core),
            in_specs=[pl.BlockSpec((GATHER_WINDOW, width), index_map=lambda c, i: (c * per_core + i, 0)),
                      idx_spec, idx_spec],
            out_specs=[],
            core_axis_name=("core", "subcore"),
            dimension_semantics=(pltpu.PARALLEL, pltpu.PARALLEL),
        )(rows_hbm, idx0_hbm, idx1_hbm)

    return scatter_kernel(rows, dest[0].reshape(1, T), dest[1].reshape(1, T))


def _expert_block(rows, n_valid, xs_refs, wgu_ref, wdn_ref, ys_refs):
    packed_in = jnp.concatenate([r[rows] for r in xs_refs], axis=1)
    row = lax.broadcasted_iota(jnp.int32, packed_in.shape, 0)
    packed_in = jnp.where(row < n_valid, packed_in, 0)
    xb = _unpack_bf16_pairs(lax.bitcast_convert_type(packed_in, jnp.uint32)).astype(BF16)
    gu = jnp.dot(xb, wgu_ref[...], preferred_element_type=F32)
    gate, up = gu[:, :EXPERT_FF], gu[:, EXPERT_FF:]
    act = (gate * jax.nn.sigmoid(gate) * up).astype(BF16)
    y = jnp.dot(act, wdn_ref[...], preferred_element_type=F32)
    packed = lax.bitcast_convert_type(_pack_bf16_pairs(y.astype(BF16).astype(F32)), jnp.int32)
    w = packed.shape[1] // len(ys_refs)
    for n, ys_ref in enumerate(ys_refs):
        ys_ref[rows] = packed[:, n * w:(n + 1) * w]


def _expert_kernel(be_ref, nv_ref, *refs):
    xs_refs, w_refs, ys_refs = refs[:Y_SLABS], refs[Y_SLABS:-Y_SLABS], refs[-Y_SLABS:]
    for j in range(BLOCKS_PER_STEP):
        _expert_block(slice(j * MOE_BLOCK, (j + 1) * MOE_BLOCK),
                      nv_ref[pl.program_id(0) * BLOCKS_PER_STEP + j],
                      xs_refs, w_refs[2 * j], w_refs[2 * j + 1], ys_refs)


def _experts(xs_slabs, block_e, block_valid, w_gu, w_down):
    P, slab_w = xs_slabs[0].shape
    n_blocks = P // MOE_BLOCK
    bps = BLOCKS_PER_STEP
    slab_spec = pl.BlockSpec((bps * MOE_BLOCK, slab_w), lambda b, be, nv: (b, 0))
    w_specs, w_args = [], []
    for j in range(bps):
        w_specs += [
            pl.BlockSpec((None, D_MODEL, 2 * EXPERT_FF), lambda b, be, nv, j=j: (be[b * bps + j], 0, 0)),
            pl.BlockSpec((None, EXPERT_FF, D_MODEL), lambda b, be, nv, j=j: (be[b * bps + j], 0, 0))]
        w_args += [w_gu, w_down]
    grid_spec = pltpu.PrefetchScalarGridSpec(
        num_scalar_prefetch=2,
        grid=(n_blocks // bps,),
        in_specs=[slab_spec] * Y_SLABS + w_specs,
        out_specs=[slab_spec] * Y_SLABS,
    )
    return pl.pallas_call(
        _expert_kernel,
        grid_spec=grid_spec,
        out_shape=[jax.ShapeDtypeStruct((P, slab_w), jnp.int32)] * Y_SLABS,
        compiler_params=pltpu.CompilerParams(dimension_semantics=("arbitrary",),
                                             vmem_limit_bytes=VMEM_LIMIT),
        name="experts",
    )(block_e, block_valid, *xs_slabs, *w_args)


def _gather_rows(table, indices):
    n, width = indices.shape[0], table.shape[1]
    mesh = plsc.VectorSubcoreMesh(core_axis_name="core", subcore_axis_name="subcore")

    @pl.kernel(out_type=jax.ShapeDtypeStruct((n, width), table.dtype), mesh=mesh)
    def gather_kernel(table_hbm, idx_hbm, out_hbm):
        def body(idx_vmem, out_vmem):
            pltpu.sync_copy(table_hbm.at[idx_vmem.at[0]], out_vmem)

        per_core = n // GATHER_WINDOW // SC_CORES
        pltpu.emit_pipeline(
            body,
            grid=(SC_CORES, per_core),
            in_specs=[pl.BlockSpec((1, GATHER_WINDOW), index_map=lambda c, i: (0, c * per_core + i))],
            out_specs=[pl.BlockSpec((GATHER_WINDOW, width), index_map=lambda c, i: (c * per_core + i, 0))],
            core_axis_name=("core", "subcore"),
            dimension_semantics=(pltpu.PARALLEL, pltpu.PARALLEL),
        )(idx_hbm, out_hbm)

    return gather_kernel(table, indices.reshape(1, n))


def _combine_kernel(x2_ref, route_ref, g_ref, *refs):
    y_refs, out_ref = refs[:-1], refs[-1]
    route = route_ref[...]

    def expert_rows(slabs):
        packed = jnp.concatenate([s[...] for s in slabs], axis=1)
        return _unpack_bf16_pairs(lax.bitcast_convert_type(packed, jnp.uint32))

    moe = expert_rows(y_refs[:Y_SLABS]) * route[:, 2:3] + expert_rows(y_refs[Y_SLABS:]) * route[:, 3:4]
    out_ref[...] = _rms(x2_ref[...] + moe, g_ref[...])


def _combine(x2, route, g_final, dest, ys_slabs):
    T = x2.shape[0]
    tm = MIX_TILE
    n_steps = T // tm
    rows = [_gather_rows(ys, dest.reshape(-1)) for ys in ys_slabs]
    slab_w = rows[0].shape[1]
    first = [pl.BlockSpec((tm, slab_w), lambda i: (i, 0))] * Y_SLABS
    second = [pl.BlockSpec((tm, slab_w), lambda i: (i + n_steps, 0))] * Y_SLABS
    return pl.pallas_call(
        _combine_kernel,
        grid=(n_steps,),
        in_specs=[pl.BlockSpec((tm, D_MODEL), lambda i: (i, 0)),
                  pl.BlockSpec((tm, ROUTE_W), lambda i: (i, 0)),
                  _const_spec((1, D_MODEL))] + first + second,
        out_specs=pl.BlockSpec((tm, D_MODEL), lambda i: (i, 0)),
        out_shape=jax.ShapeDtypeStruct((T, D_MODEL), F32),
        compiler_params=pltpu.CompilerParams(dimension_semantics=("arbitrary",)),
        name="combine",
    )(x2, route, g_final, *rows, *rows)


def _routing_tables(route, counts_rec, n_tokens):
    n_slots = 2 * n_tokens
    n_blocks = n_slots // MOE_BLOCK + N_EXPERTS
    n_rows = n_blocks * MOE_BLOCK
    counts = counts_rec[0, EXPERT_LANE0:EXPERT_LANE0 + N_EXPERTS].astype(jnp.int32)
    padded = (counts + MOE_BLOCK - 1) // MOE_BLOCK * MOE_BLOCK
    pends = jnp.cumsum(padded)
    pstarts = pends - padded
    expert = route[:, 0:2].astype(jnp.int32)
    rank = route[:, 4:6].astype(jnp.int32)
    lookup = lambda table, idx: jnp.sum(jnp.where(idx[..., None] == jnp.arange(N_EXPERTS), table, 0), axis=-1)
    count_le = lambda sorted_vals, q: jnp.sum(sorted_vals[None, :] <= q[:, None], axis=-1).astype(jnp.int32)
    dest = (lookup(pstarts, expert) + rank).T
    block_row0 = jnp.arange(n_blocks, dtype=jnp.int32) * MOE_BLOCK
    owner = count_le(pends, block_row0)
    block_e = jnp.minimum(owner, N_EXPERTS - 1)
    block_valid = jnp.clip(lookup(pstarts + counts, owner) - block_row0, 0, MOE_BLOCK)
    block_valid = jnp.where(owner < N_EXPERTS, block_valid, 0).astype(jnp.int32)
    return dest.astype(jnp.int32), block_e, block_valid, n_rows


def _encoder_group(x, mem, w):
    batch, seq, _ = x.shape
    T = batch * seq
    *qkv_groups, qkv_win = _proj(x, w["g_mix"], w["w_in"])

    slopes_a = _alibi_slopes(A_HEADS).reshape(A_GROUPS, A_HEADS_PER_GROUP)
    o_groups, lse_groups = [], []
    for g, (window, r) in enumerate(A_CONFIGS):
        o, lse = _banded_attention(qkv_groups[g], half_w=window // (2 * r), kv_width=GROUP_W,
                                   slopes=slopes_a[g] * np.float32(r))
        o_groups.append(o)
        lse_groups.append(lse)
    (o_b,) = _banded_attention(qkv_win.reshape(batch, 1, seq, WIN_W), half_w=B_HALF_WINDOW,
                               kv_width=KV2_W, slopes=_alibi_slopes(B_HEADS), sink=w["sink_b"],
                               want_lse=False)

    kvm = _memkv(mem.reshape(-1, D_MODEL), w["g_mem"], w["w_ckv"])
    mix_w = [w[k] for k in ("g_mix", "w_gate", "b_gate", "w_branch", "w_out", "g_xattn", "w_cq",
                            "w_co", "g_ffn", "w_route", "b_route")]
    x2, *h_slabs, route, counts_rec = _mix(x, o_groups, lse_groups, o_b, kvm, mix_w)

    dest, block_e, block_valid, n_rows = _routing_tables(route, counts_rec, T)
    xs_slabs = [_scatter_rows(h, dest, n_rows) for h in h_slabs]
    ys_slabs = _experts(xs_slabs, block_e, block_valid, w["w_gu"], w["w_down"])
    y = _combine(x2, route, w["g_final"], dest, ys_slabs)
    return y.reshape(batch, seq, D_MODEL)


def _prep_weights(g_mix, w_in, sink_b, w_gate, b_gate, w_branch, w_out, g_xattn, g_mem, w_cq, w_ckv,
                  w_co, g_ffn, w_rg, b_rg, w_re, b_re, w_gu, w_down, g_final):
    scale = HEAD_DIM ** -0.5
    aw = A_WIDTH
    qa, ka, va = w_in[:, :aw] * scale, w_in[:, aw:2 * aw], w_in[:, 2 * aw:3 * aw]
    qb = w_in[:, 3 * aw:3 * aw + B_Q] * scale
    kb = w_in[:, 3 * aw + B_Q:3 * aw + B_Q + B_KV]
    vb = w_in[:, 3 * aw + B_Q + B_KV:]
    twice = lambda t: jnp.repeat(t.reshape(D_MODEL, B_KV_HEADS, 1, HEAD_DIM), 2, axis=2).reshape(D_MODEL, KV2_W)
    group = lambda t, g: t[:, g * GROUP_W:(g + 1) * GROUP_W]
    cols = [group(t, g) for g in range(A_GROUPS) for t in (qa, ka, va)] + [qb, twice(kb), twice(vb)]
    w_in_x = jnp.concatenate(cols, axis=1).astype(BF16)
    w_route = jnp.zeros((D_MODEL, ROUTE_W), F32)
    w_route = w_route.at[:, :N_GROUPS].set(w_rg).at[:, EXPERT_LANE0:EXPERT_LANE0 + N_EXPERTS].set(w_re)
    b_route = jnp.zeros((1, ROUTE_W), F32)
    b_route = b_route.at[0, :N_GROUPS].set(b_rg).at[0, EXPERT_LANE0:EXPERT_LANE0 + N_EXPERTS].set(b_re)
    w_route_hi = w_route.astype(BF16)
    w_route = jnp.concatenate([w_route_hi, (w_route - w_route_hi.astype(F32)).astype(BF16)], axis=1)
    vec = lambda v: v.reshape(1, -1).astype(F32)
    return dict(
        g_mix=vec(g_mix), w_in=w_in_x, sink_b=sink_b.astype(F32),
        w_gate=w_gate.astype(BF16), b_gate=vec(b_gate), w_branch=w_branch.astype(BF16),
        w_out=w_out.astype(BF16), g_xattn=vec(g_xattn), g_mem=vec(g_mem), w_cq=w_cq.astype(BF16),
        w_ckv=w_ckv.astype(BF16), w_co=w_co.astype(BF16), g_ffn=vec(g_ffn),
        w_route=w_route, b_route=b_route, w_gu=w_gu.astype(BF16), w_down=w_down.astype(BF16),
        g_final=vec(g_final))


def kernel(x_prompt, x_sample, mem_prompt, mem_sample, g_mix, w_in, sink_b, w_gate, b_gate, w_branch,
           w_out, g_xattn, g_mem, w_cq, w_ckv, w_co, g_ffn, w_rg, b_rg, w_re, b_re, w_gu, w_down,
           g_final):
    assert g_mix.shape[0] == 1, "single-layer encoder"
    w = _prep_weights(g_mix[0], w_in[0], sink_b[0], w_gate[0], b_gate[0], w_branch[0], w_out[0],
                      g_xattn[0], g_mem[0], w_cq[0], w_ckv[0], w_co[0], g_ffn[0], w_rg[0], b_rg[0],
                      w_re[0], b_re[0], w_gu[0], w_down[0], g_final)
    y_sample = _encoder_group(x_sample, mem_sample, w)
    y_prompt = _encoder_group(x_prompt, mem_prompt, w)
    return (y_prompt, y_sample)
```

```python
import functools

import numpy as np
import jax
import jax.numpy as jnp
from jax import lax
from jax.experimental import pallas as pl
from jax.experimental.pallas import tpu as pltpu
from jax.experimental.pallas import tpu_sc as plsc

F32 = jnp.float32
BF16 = jnp.bfloat16

D_MODEL = 1024
HEAD_DIM = 64
A_CONFIGS = ((128, 1), (512, 4), (2048, 16))
A_GROUPS = 3
A_HEADS_PER_GROUP = 8
A_HEADS = A_GROUPS * A_HEADS_PER_GROUP
A_WIDTH = A_HEADS * HEAD_DIM
B_HEADS = 8
B_KV_HEADS = 2
B_HALF_WINDOW = 128
B_Q = B_HEADS * HEAD_DIM
B_KV = B_KV_HEADS * HEAD_DIM
X_HEADS = 4
X_HEAD_DIM = 128
X_WIDTH = X_HEADS * X_HEAD_DIM
N_GROUPS = 4
EXPERTS_PER_GROUP = 8
N_EXPERTS = N_GROUPS * EXPERTS_PER_GROUP
EXPERT_FF = 512
MOE_BLOCK = 256
EPS = 1e-6
NEG = -1e30

LANES = 128
GROUP_W = A_HEADS_PER_GROUP * HEAD_DIM
PAIR_W = 2 * HEAD_DIM
N_PAIRS = GROUP_W // PAIR_W
QKV_W = 3 * GROUP_W
KV2_W = 2 * B_KV
WIN_W = B_Q + 2 * KV2_W
PROJ_W = A_GROUPS * QKV_W + WIN_W
ROUTE_W = LANES
EXPERT_LANE0 = 32
VMEM_LIMIT = 52 * 1024 * 1024

ROW_TILE = 256
PROJ_TILE = 512
MIX_TILE = 512
MIX_SKEW = 3
MIX_SUB = 256
Q_STEPS = 8
Q_UNROLL = 4
GATHER_WINDOW = 128
SC_CORES = 2
BLOCKS_PER_STEP = 2
Y_SLABS = 2
def _rms(x, g):
    ms = jnp.mean(x * x, axis=-1, keepdims=True)
    return x * lax.rsqrt(ms + EPS) * g


def _alibi_slopes(n):
    return 2.0 ** (-8.0 * np.arange(1, n + 1, dtype=np.float32) / n)


def _const_spec(shape):
    return pl.BlockSpec(shape, lambda *_: (0,) * len(shape), pipeline_mode=pl.Buffered(1))


def _proj_kernel(x_ref, g_ref, w_ref, *refs):
    o_refs, win_ref, h_ref = refs[:A_GROUPS], refs[A_GROUPS], refs[A_GROUPS + 1]
    tm = x_ref.shape[0]
    h32 = _rms(x_ref[...], g_ref[...])
    h_nat = h32.astype(BF16)
    n_slabs = h_ref.shape[0]
    for s in range(n_slabs):
        h_ref[s] = h32[:, s * LANES:(s + 1) * LANES]
    for g, (_, r) in enumerate(A_CONFIGS):
        n = tm // r
        if r == 1:
            h = h_nat
        else:
            h = jnp.concatenate(
                [jnp.concatenate([h_ref[s, pl.ds(c, n, stride=r), :] for c in range(r)], axis=0)
                 for s in range(n_slabs)], axis=1).astype(BF16)
        for j in range(3):
            cols = slice(g * QKV_W + j * GROUP_W, g * QKV_W + (j + 1) * GROUP_W)
            res = jnp.dot(h, w_ref[:, cols], preferred_element_type=F32).astype(BF16)
            for c in range(r):
                o_refs[g][c, :, j * GROUP_W:(j + 1) * GROUP_W] = res[c * n:(c + 1) * n]
    for j in range(WIN_W // GROUP_W):
        cols = slice(A_GROUPS * QKV_W + j * GROUP_W, A_GROUPS * QKV_W + (j + 1) * GROUP_W)
        win_ref[:, j * GROUP_W:(j + 1) * GROUP_W] = jnp.dot(
            h_nat, w_ref[:, cols], preferred_element_type=F32).astype(BF16)


def _proj(x, g, w):
    batch, seq, _ = x.shape
    tm = PROJ_TILE
    out_specs = [pl.BlockSpec((None, r, tm // r, QKV_W), lambda b, i: (b, 0, i, 0)) for _, r in A_CONFIGS]
    out_shape = [jax.ShapeDtypeStruct((batch, r, seq // r, QKV_W), BF16) for _, r in A_CONFIGS]
    out_specs.append(pl.BlockSpec((None, tm, WIN_W), lambda b, i: (b, i, 0)))
    out_shape.append(jax.ShapeDtypeStruct((batch, seq, WIN_W), BF16))
    return pl.pallas_call(
        _proj_kernel,
        grid=(batch, seq // tm),
        in_specs=[pl.BlockSpec((None, tm, D_MODEL), lambda b, i: (b, i, 0)),
                  _const_spec((1, D_MODEL)),
                  _const_spec((D_MODEL, PROJ_W))],
        out_specs=out_specs,
        out_shape=out_shape,
        scratch_shapes=[pltpu.VMEM((D_MODEL // LANES, tm, LANES), F32)],
        compiler_params=pltpu.CompilerParams(dimension_semantics=("arbitrary",) * 2,
                                             vmem_limit_bytes=VMEM_LIMIT),
        name="proj",
    )(x, g, w)


def _attn_kernel(*refs, qb, kb, q_steps, m_len, half_w, offsets, kv_shared, has_sink, want_lse):
    refs = list(refs)
    sink_ref = refs.pop(0) if has_sink else None
    bias_ref, q_ref, k_ref, v_ref, o_ref = refs[:5]
    lse_ref = refs[5] if want_lse else None
    lo_q = lax.broadcasted_iota(jnp.int32, (qb, PAIR_W), 1) < HEAD_DIM
    first_head = lax.broadcasted_iota(jnp.int32, (2 * qb, 1), 0) < qb
    zeros_q = jnp.zeros((qb, PAIR_W), BF16)

    def q_block(it, carry):
        cc, qi = it // q_steps, it % q_steps
        gi = pl.program_id(2) * q_steps + qi
        ks = pl.multiple_of(jnp.clip(gi * qb - half_w, 0, m_len - kb), 16)
        off = gi * qb - ks
        var = sum(jnp.where(off == o, n, 0) for n, o in enumerate(offsets))
        rows = pl.ds(pl.multiple_of(qi * qb, qb), qb)
        for j in range(N_PAIRS):
            cols = slice(j * PAIR_W, (j + 1) * PAIR_W)
            jc = (j // 2) if kv_shared else j
            kcols = slice(jc * PAIR_W, (jc + 1) * PAIR_W)
            qp = q_ref[cc, rows, cols]
            q_st = jnp.concatenate([jnp.where(lo_q, qp, zeros_q), jnp.where(lo_q, zeros_q, qp)], axis=0)
            s = lax.dot_general(q_st, k_ref[cc, pl.ds(ks, kb), kcols], (((1,), (1,)), ((), ())),
                                preferred_element_type=F32) + bias_ref[var, j]
            m = jnp.max(s, axis=-1, keepdims=True)
            if has_sink:
                sk = jnp.where(first_head, sink_ref[2 * j], sink_ref[2 * j + 1])
                m = jnp.maximum(m, sk)
            e = jnp.exp(s - m)
            den = jnp.sum(e, axis=-1, keepdims=True)
            if has_sink:
                den = den + jnp.exp(sk - m)
            o2 = jnp.dot(e.astype(BF16), v_ref[cc, pl.ds(ks, kb), kcols], preferred_element_type=F32) / den
            o_ref[cc, rows, cols] = jnp.where(lo_q, o2[:qb], o2[qb:]).astype(BF16)
            if want_lse:
                l2 = m + jnp.log(den)
                lse_ref[cc, rows, cols] = jnp.where(lo_q, l2[:qb], l2[qb:])
        return carry

    n_iter = q_ref.shape[0] * q_steps
    lax.fori_loop(0, n_iter, q_block, 0, unroll=min(Q_UNROLL, n_iter))


def _bias_tables(offsets, slopes, qb, kb, half_w):
    rel = jnp.arange(qb, dtype=jnp.int32)[:, None] - jnp.arange(kb, dtype=jnp.int32)[None, :]
    dist = jnp.abs(rel[None] + jnp.asarray(offsets, jnp.int32)[:, None, None])
    bias = -jnp.asarray(slopes, F32)[None, :, None, None] * dist.astype(F32)[:, None]
    bias = jnp.where((dist <= half_w)[:, None], bias, NEG)
    return bias.reshape(len(offsets), N_PAIRS, 2 * qb, kb)


def _banded_attention(qkv, *, half_w, kv_width, slopes, sink=None, want_lse=True):
    batch, r, m_len, _ = qkv.shape
    qb = min(2 * half_w, m_len)
    kb = min(qb + 2 * half_w, m_len)
    nq = m_len // qb
    q_steps = min(Q_STEPS, nq)
    cb = min(r, Q_STEPS // q_steps)
    kv_shared = kv_width != GROUP_W
    has_sink = sink is not None
    k_block = GROUP_W // kv_width
    offsets = sorted({i * qb - min(max(i * qb - half_w, 0), m_len - kb) for i in range(nq)})
    bias = _bias_tables(offsets, slopes, qb, kb, half_w)

    in_specs = []
    args = []
    if has_sink:
        in_specs.append(pl.BlockSpec(memory_space=pltpu.SMEM))
        args.append(sink)
    in_specs += [
        _const_spec(bias.shape),
        pl.BlockSpec((None, cb, q_steps * qb, GROUP_W), lambda b, c, i: (b, c, i, 0)),
        pl.BlockSpec((None, cb, m_len, kv_width), lambda b, c, i: (b, c, 0, k_block)),
        pl.BlockSpec((None, cb, m_len, kv_width), lambda b, c, i: (b, c, 0, k_block + 1)),
    ]
    args += [bias, qkv, qkv, qkv]
    out_spec = pl.BlockSpec((None, cb, q_steps * qb, GROUP_W), lambda b, c, i: (b, c, i, 0))
    out_specs = [out_spec]
    out_shape = [jax.ShapeDtypeStruct((batch, r, m_len, GROUP_W), BF16)]
    if want_lse:
        out_specs.append(out_spec)
        out_shape.append(jax.ShapeDtypeStruct((batch, r, m_len, GROUP_W), F32))
    kern = functools.partial(_attn_kernel, qb=qb, kb=kb, q_steps=q_steps, m_len=m_len, half_w=half_w,
                             offsets=tuple(offsets), kv_shared=kv_shared, has_sink=has_sink,
                             want_lse=want_lse)
    return pl.pallas_call(
        kern,
        grid=(batch, r // cb, nq // q_steps),
        in_specs=in_specs,
        out_specs=out_specs,
        out_shape=out_shape,
        compiler_params=pltpu.CompilerParams(dimension_semantics=("arbitrary",) * 3,
                                             vmem_limit_bytes=VMEM_LIMIT),
        name=f"attn_r{r}_w{half_w}",
    )(*args)


def _memkv_kernel(m_ref, g_ref, w_ref, o_ref):
    h = _rms(m_ref[...], g_ref[...]).astype(BF16)
    o_ref[...] = jnp.dot(h, w_ref[...], preferred_element_type=F32).astype(BF16)


def _memkv(mem2d, g, w):
    R = mem2d.shape[0]
    return pl.pallas_call(
        _memkv_kernel,
        grid=(R // ROW_TILE,),
        in_specs=[pl.BlockSpec((ROW_TILE, D_MODEL), lambda i: (i, 0)),
                  _const_spec((1, D_MODEL)),
                  _const_spec((D_MODEL, 2 * X_WIDTH))],
        out_specs=pl.BlockSpec((ROW_TILE, 2 * X_WIDTH), lambda i: (i, 0)),
        out_shape=jax.ShapeDtypeStruct((R, 2 * X_WIDTH), BF16),
        compiler_params=pltpu.CompilerParams(dimension_semantics=("arbitrary",)),
        name="memkv",
    )(mem2d, g, w)


def _token_order(src_ref, dst_ref, m0, n):
    r = src_ref.shape[0]
    if r == 1:
        return src_ref[0, m0:m0 + n].astype(F32)
    n_slabs = dst_ref.shape[0]
    for c in range(r):
        rows = src_ref[c, m0:m0 + n].astype(F32)
        for s in range(n_slabs):
            dst_ref[s, pl.ds(m0 * r + c, n, stride=r), :] = rows[:, s * LANES:(s + 1) * LANES]
    return jnp.concatenate([dst_ref[s, m0 * r:(m0 + n) * r] for s in range(n_slabs)], axis=1)


def _mix_rows(t0, tm, x_ref, o_refs, l_refs, ob_ref, kv_ref,
              g_mix_ref, w_gate_ref, b_gate_ref, w_br_ref, w_out_ref,
              g_x_ref, w_cq_ref, w_co_ref, g_ffn_ref, w_rt_ref, b_rt_ref,
              x2_ref, hp_refs, route_ref, tri_ref, carry_ref, order_refs):
    rows = slice(t0, t0 + tm)
    x = x_ref[rows]
    h1 = _rms(x, g_mix_ref[...]).astype(BF16)
    yield

    by_class = lambda ref, scratch: _token_order(ref, scratch, t0 // ref.shape[0], tm // ref.shape[0])
    l0, l1, l2 = (by_class(l, s) for l, s in zip(l_refs, order_refs[:3]))
    lm = jnp.maximum(jnp.maximum(l0, l1), l2)
    e0, e1, e2 = jnp.exp(l0 - lm), jnp.exp(l1 - lm), jnp.exp(l2 - lm)
    den = e0 + e1 + e2
    o0, o1, o2 = (by_class(o, s) for o, s in zip(o_refs, order_refs[3:]))
    oa = (e0 / den) * o0 + (e1 / den) * o1 + (e2 / den) * o2
    yield
    br_a = jnp.dot(oa.astype(BF16), w_br_ref[:GROUP_W, :], preferred_element_type=F32)
    br_b = jnp.dot(ob_ref[rows], w_br_ref[GROUP_W:, :], preferred_element_type=F32)
    ga = jax.nn.sigmoid(jnp.dot(h1, w_gate_ref[:, :D_MODEL], preferred_element_type=F32)
                        + b_gate_ref[:, :D_MODEL])
    merged = ga * br_a
    gb = jax.nn.sigmoid(jnp.dot(h1, w_gate_ref[:, D_MODEL:], preferred_element_type=F32)
                        + b_gate_ref[:, D_MODEL:])
    merged = merged + gb * br_b
    yield
    x1 = x + jnp.dot(merged.astype(BF16), w_out_ref[...], preferred_element_type=F32)
    yield

    h2 = _rms(x1, g_x_ref[...]).astype(BF16)
    q = jnp.dot(h2, w_cq_ref[...], preferred_element_type=F32).astype(BF16)
    heads = []
    for h in range(X_HEADS):
        cols = slice(h * X_HEAD_DIM, (h + 1) * X_HEAD_DIM)
        kh = kv_ref[:, cols]
        vh = kv_ref[:, X_WIDTH + h * X_HEAD_DIM:X_WIDTH + (h + 1) * X_HEAD_DIM]
        s = lax.dot_general(q[:, cols], kh, (((1,), (1,)), ((), ())),
                            preferred_element_type=F32) * (X_HEAD_DIM ** -0.5)
        m = jnp.max(s, axis=-1, keepdims=True)
        e = jnp.exp(s - m)
        p = e / jnp.sum(e, axis=-1, keepdims=True)
        heads.append(jnp.dot(p.astype(BF16), vh, preferred_element_type=F32))
    o = jnp.concatenate(heads, axis=1).astype(BF16)
    yield
    x2 = x1 + jnp.dot(o, w_co_ref[...], preferred_element_type=F32)
    x2_ref[rows] = x2
    yield

    h3 = _rms(x2, g_ffn_ref[...])
    h3_hi = h3.astype(BF16)
    packed = lax.bitcast_convert_type(_pack_bf16_pairs(h3_hi.astype(F32)), jnp.int32)
    slab_w = packed.shape[1] // len(hp_refs)
    for n, hp_ref in enumerate(hp_refs):
        hp_ref[rows] = packed[:, n * slab_w:(n + 1) * slab_w]
    h3_lo = (h3 - h3_hi.astype(F32)).astype(BF16)
    by_hi = jnp.dot(h3_hi, w_rt_ref[...], preferred_element_type=F32)
    logits = (by_hi[:, :ROUTE_W] + by_hi[:, ROUTE_W:]
              + jnp.dot(h3_lo, w_rt_ref[:, :ROUTE_W], preferred_element_type=F32)) + b_rt_ref[...]
    lane = lax.broadcasted_iota(jnp.int32, (tm, ROUTE_W), 1)
    gmask = lane < N_GROUPS
    gl = jnp.where(gmask, logits, -jnp.inf)
    gmax = jnp.max(gl, axis=-1, keepdims=True)
    gidx = jnp.min(jnp.where(gl == gmax, lane, ROUTE_W), axis=-1, keepdims=True)
    pg_sel = 1.0 / jnp.sum(jnp.where(gmask, jnp.exp(logits - gmax), 0.0), axis=-1, keepdims=True)
    e_lo = EXPERT_LANE0 + gidx * EXPERTS_PER_GROUP
    emask = (lane >= e_lo) & (lane < e_lo + EXPERTS_PER_GROUP)
    el = jnp.where(emask, logits, -jnp.inf)
    emax1 = jnp.max(el, axis=-1, keepdims=True)
    i1 = jnp.min(jnp.where(el == emax1, lane, ROUTE_W), axis=-1, keepdims=True)
    el2 = jnp.where(lane == i1, -jnp.inf, el)
    emax2 = jnp.max(el2, axis=-1, keepdims=True)
    i2 = jnp.min(jnp.where(el2 == emax2, lane, ROUTE_W), axis=-1, keepdims=True)
    t2 = jnp.exp(emax2 - emax1)
    w1 = pg_sel / (1.0 + t2)
    w2 = pg_sel * t2 / (1.0 + t2)
    yield

    oh1 = lane == i1
    oh2 = lane == i2
    ohs = jnp.where(oh1 | oh2, 1.0, 0.0)
    before = jnp.dot(tri_ref[...], ohs.astype(BF16), preferred_element_type=F32) + carry_ref[...]
    rank1 = jnp.sum(jnp.where(oh1, before, 0.0), axis=-1, keepdims=True)
    rank2 = jnp.sum(jnp.where(oh2, before, 0.0), axis=-1, keepdims=True)
    carry_ref[...] = carry_ref[...] + jnp.sum(ohs, axis=0, keepdims=True)

    rec = jnp.where(lane == 0, (i1 - EXPERT_LANE0).astype(F32), 0.0)
    rec = jnp.where(lane == 1, (i2 - EXPERT_LANE0).astype(F32), rec)
    rec = jnp.where(lane == 2, w1, rec)
    rec = jnp.where(lane == 3, w2, rec)
    rec = jnp.where(lane == 4, rank1, rec)
    rec = jnp.where(lane == 5, rank2, rec)
    route_ref[rows] = rec


def _mix_kernel(x_ref, o0_ref, o1_ref, o2_ref, l0_ref, l1_ref, l2_ref, ob_ref, *refs):
    n_w = 12
    w_refs, refs = refs[:n_w], refs[n_w:]
    x2_ref, hp_refs, refs = refs[0], refs[1:1 + Y_SLABS], refs[1 + Y_SLABS:]
    route_ref, counts_ref, tri_ref, carry_ref = refs[:4]
    order_refs = refs[4:]
    sub = tri_ref.shape[0]

    @pl.when((pl.program_id(0) == 0) & (pl.program_id(1) == 0))
    def _():
        row = lax.broadcasted_iota(jnp.int32, (sub, sub), 0)
        col = lax.broadcasted_iota(jnp.int32, (sub, sub), 1)
        tri_ref[...] = jnp.where(row > col, 1.0, 0.0).astype(BF16)
        carry_ref[...] = jnp.zeros_like(carry_ref)

    chains = [_mix_rows(t0, sub, x_ref, (o0_ref, o1_ref, o2_ref), (l0_ref, l1_ref, l2_ref), ob_ref, *w_refs,
                        x2_ref, hp_refs, route_ref, tri_ref, carry_ref, order_refs)
              for t0 in range(0, x_ref.shape[0], sub)]
    finished = [False] * len(chains)
    rounds = 0
    while not all(finished):
        for n, chain in enumerate(chains):
            if MIX_SKEW * n <= rounds and not finished[n]:
                finished[n] = next(chain, "end") == "end"
        rounds += 1
    counts_ref[...] = jnp.broadcast_to(carry_ref[...], counts_ref.shape)


def _mix(x, o_groups, lse_groups, o_b, kvm, wts):
    batch, seq, _ = x.shape
    T = batch * seq
    tm = MIX_TILE
    steps = seq // tm
    mem_tokens = kvm.shape[0] // batch
    row = lambda w: pl.BlockSpec((tm, w), lambda b, i: (b * steps + i, 0))
    by_class = [pl.BlockSpec((None, r, tm // r, GROUP_W), lambda b, i: (b, 0, i, 0)) for _, r in A_CONFIGS]
    in_specs = ([pl.BlockSpec((None, tm, D_MODEL), lambda b, i: (b, i, 0))] + by_class + by_class
                + [pl.BlockSpec((None, None, tm, GROUP_W), lambda b, i: (b, 0, i, 0)),
                   pl.BlockSpec((mem_tokens, 2 * X_WIDTH), lambda b, i: (b, 0))]
                + [_const_spec(w.shape) for w in wts])
    return pl.pallas_call(
        _mix_kernel,
        grid=(batch, steps),
        in_specs=in_specs,
        out_specs=[row(D_MODEL)] + [row(D_MODEL // 2 // Y_SLABS)] * Y_SLABS
        + [row(ROUTE_W), pl.BlockSpec((8, ROUTE_W), lambda b, i: (0, 0))],
        out_shape=[jax.ShapeDtypeStruct((T, D_MODEL), F32)]
        + [jax.ShapeDtypeStruct((T, D_MODEL // 2 // Y_SLABS), jnp.int32)] * Y_SLABS
        + [jax.ShapeDtypeStruct((T, ROUTE_W), F32),
                   jax.ShapeDtypeStruct((8, ROUTE_W), F32)],
        scratch_shapes=[pltpu.VMEM((MIX_SUB, MIX_SUB), BF16), pltpu.VMEM((1, ROUTE_W), F32)]
        + [pltpu.VMEM((GROUP_W // LANES, tm, LANES), F32)] * (2 * A_GROUPS),
        compiler_params=pltpu.CompilerParams(dimension_semantics=("arbitrary",) * 2,
                                             vmem_limit_bytes=VMEM_LIMIT),
        name="mix",
    )(x, *o_groups, *lse_groups, o_b, kvm, *wts)


def _pack_bf16_pairs(x):
    w = x.shape[1] // 2
    lo = lax.bitcast_convert_type(x[:, :w], jnp.uint32) >> 16
    hi = lax.bitcast_convert_type(x[:, w:], jnp.uint32) & jnp.uint32(0xFFFF0000)
    return lo | hi


def _unpack_bf16_pairs(u):
    lo = lax.bitcast_convert_type(u << 16, F32)
    hi = lax.bitcast_convert_type(u & jnp.uint32(0xFFFF0000), F32)
    return jnp.concatenate([lo, hi], axis=1)


def _scatter_rows(rows, dest, n_rows):
    T, width = rows.shape
    mesh = plsc.VectorSubcoreMesh(core_axis_name="core", subcore_axis_name="subcore")

    @pl.kernel(out_type=jax.ShapeDtypeStruct((n_rows, width), rows.dtype), mesh=mesh, scratch_types=[])
    def scatter_kernel(rows_hbm, idx0_hbm, idx1_hbm, out_hbm):
        def body(rows_vmem, idx0_vmem, idx1_vmem):
            pltpu.sync_copy(rows_vmem, out_hbm.at[idx0_vmem.at[0]])
            pltpu.sync_copy(rows_vmem, out_hbm.at[idx1_vmem.at[0]])

        per_core = T // GATHER_WINDOW // SC_CORES
        idx_spec = pl.BlockSpec((1, GATHER_WINDOW), index_map=lambda c, i: (0, c * per_core + i))
        pltpu.emit_pipeline(
            body,
            grid=(SC_CORES, per_core),
            in_specs=[pl.BlockSpec((GATHER_WINDOW, width), index_map=lambda c, i: (c * per_core + i, 0)),
                      idx_spec, idx_spec],
            out_specs=[],
            core_axis_name=("core", "subcore"),
            dimension_semantics=(pltpu.PARALLEL, pltpu.PARALLEL),
        )(rows_hbm, idx0_hbm, idx1_hbm)

    return scatter_kernel(rows, dest[0].reshape(1, T), dest[1].reshape(1, T))


def _expert_block(rows, n_valid, xs_refs, wgu_ref, wdn_ref, ys_refs):
    packed_in = jnp.concatenate([r[rows] for r in xs_refs], axis=1)
    row = lax.broadcasted_iota(jnp.int32, packed_in.shape, 0)
    packed_in = jnp.where(row < n_valid, packed_in, 0)
    xb = _unpack_bf16_pairs(lax.bitcast_convert_type(packed_in, jnp.uint32)).astype(BF16)
    gu = jnp.dot(xb, wgu_ref[...], preferred_element_type=F32)
    gate, up = gu[:, :EXPERT_FF], gu[:, EXPERT_FF:]
    act = (gate * jax.nn.sigmoid(gate) * up).astype(BF16)
    y = jnp.dot(act, wdn_ref[...], preferred_element_type=F32)
    packed = lax.bitcast_convert_type(_pack_bf16_pairs(y.astype(BF16).astype(F32)), jnp.int32)
    w = packed.shape[1] // len(ys_refs)
    for n, ys_ref in enumerate(ys_refs):
        ys_ref[rows] = packed[:, n * w:(n + 1) * w]


def _expert_kernel(be_ref, nv_ref, *refs):
    xs_refs, w_refs, ys_refs = refs[:Y_SLABS], refs[Y_SLABS:-Y_SLABS], refs[-Y_SLABS:]
    for j in range(BLOCKS_PER_STEP):
        _expert_block(slice(j * MOE_BLOCK, (j + 1) * MOE_BLOCK),
                      nv_ref[pl.program_id(0) * BLOCKS_PER_STEP + j],
                      xs_refs, w_refs[2 * j], w_refs[2 * j + 1], ys_refs)


def _experts(xs_slabs, block_e, block_valid, w_gu, w_down):
    P, slab_w = xs_slabs[0].shape
    n_blocks = P // MOE_BLOCK
    bps = BLOCKS_PER_STEP
    slab_spec = pl.BlockSpec((bps * MOE_BLOCK, slab_w), lambda b, be, nv: (b, 0))
    w_specs, w_args = [], []
    for j in range(bps):
        w_specs += [
            pl.BlockSpec((None, D_MODEL, 2 * EXPERT_FF), lambda b, be, nv, j=j: (be[b * bps + j], 0, 0)),
            pl.BlockSpec((None, EXPERT_FF, D_MODEL), lambda b, be, nv, j=j: (be[b * bps + j], 0, 0))]
        w_args += [w_gu, w_down]
    grid_spec = pltpu.PrefetchScalarGridSpec(
        num_scalar_prefetch=2,
        grid=(n_blocks // bps,),
        in_specs=[slab_spec] * Y_SLABS + w_specs,
        out_specs=[slab_spec] * Y_SLABS,
    )
    return pl.pallas_call(
        _expert_kernel,
        grid_spec=grid_spec,
        out_shape=[jax.ShapeDtypeStruct((P, slab_w), jnp.int32)] * Y_SLABS,
        compiler_params=pltpu.CompilerParams(dimension_semantics=("arbitrary",),
                                             vmem_limit_bytes=VMEM_LIMIT),
        name="experts",
    )(block_e, block_valid, *xs_slabs, *w_args)


def _gather_rows(table, indices):
    n, width = indices.shape[0], table.shape[1]
    mesh = plsc.VectorSubcoreMesh(core_axis_name="core", subcore_axis_name="subcore")

    @pl.kernel(out_type=jax.ShapeDtypeStruct((n, width), table.dtype), mesh=mesh)
    def gather_kernel(table_hbm, idx_hbm, out_hbm):
        def body(idx_vmem, out_vmem):
            pltpu.sync_copy(table_hbm.at[idx_vmem.at[0]], out_vmem)

        per_core = n // GATHER_WINDOW // SC_CORES
        pltpu.emit_pipeline(
            body,
            grid=(SC_CORES, per_core),
            in_specs=[pl.BlockSpec((1, GATHER_WINDOW), index_map=lambda c, i: (0, c * per_core + i))],
            out_specs=[pl.BlockSpec((GATHER_WINDOW, width), index_map=lambda c, i: (c * per_core + i, 0))],
            core_axis_name=("core", "subcore"),
            dimension_semantics=(pltpu.PARALLEL, pltpu.PARALLEL),
        )(idx_hbm, out_hbm)

    return gather_kernel(table, indices.reshape(1, n))


def _combine_kernel(x2_ref, route_ref, g_ref, *refs):
    y_refs, out_ref = refs[:-1], refs[-1]
    route = route_ref[...]

    def expert_rows(slabs):
        packed = jnp.concatenate([s[...] for s in slabs], axis=1)
        return _unpack_bf16_pairs(lax.bitcast_convert_type(packed, jnp.uint32))

    moe = expert_rows(y_refs[:Y_SLABS]) * route[:, 2:3] + expert_rows(y_refs[Y_SLABS:]) * route[:, 3:4]
    out_ref[...] = _rms(x2_ref[...] + moe, g_ref[...])


def _combine(x2, route, g_final, dest, ys_slabs):
    T = x2.shape[0]
    tm = MIX_TILE
    n_steps = T // tm
    rows = [_gather_rows(ys, dest.reshape(-1)) for ys in ys_slabs]
    slab_w = rows[0].shape[1]
    first = [pl.BlockSpec((tm, slab_w), lambda i: (i, 0))] * Y_SLABS
    second = [pl.BlockSpec((tm, slab_w), lambda i: (i + n_steps, 0))] * Y_SLABS
    return pl.pallas_call(
        _combine_kernel,
        grid=(n_steps,),
        in_specs=[pl.BlockSpec((tm, D_MODEL), lambda i: (i, 0)),
                  pl.BlockSpec((tm, ROUTE_W), lambda i: (i, 0)),
                  _const_spec((1, D_MODEL))] + first + second,
        out_specs=pl.BlockSpec((tm, D_MODEL), lambda i: (i, 0)),
        out_shape=jax.ShapeDtypeStruct((T, D_MODEL), F32),
        compiler_params=pltpu.CompilerParams(dimension_semantics=("arbitrary",)),
        name="combine",
    )(x2, route, g_final, *rows, *rows)


def _routing_tables(route, counts_rec, n_tokens):
    n_slots = 2 * n_tokens
    n_blocks = n_slots // MOE_BLOCK + N_EXPERTS
    n_rows = n_blocks * MOE_BLOCK
    counts = counts_rec[0, EXPERT_LANE0:EXPERT_LANE0 + N_EXPERTS].astype(jnp.int32)
    padded = (counts + MOE_BLOCK - 1) // MOE_BLOCK * MOE_BLOCK
    pends = jnp.cumsum(padded)
    pstarts = pends - padded
    expert = route[:, 0:2].astype(jnp.int32)
    rank = route[:, 4:6].astype(jnp.int32)
    lookup = lambda table, idx: jnp.sum(jnp.where(idx[..., None] == jnp.arange(N_EXPERTS), table, 0), axis=-1)
    count_le = lambda sorted_vals, q: jnp.sum(sorted_vals[None, :] <= q[:, None], axis=-1).astype(jnp.int32)
    dest = (lookup(pstarts, expert) + rank).T
    block_row0 = jnp.arange(n_blocks, dtype=jnp.int32) * MOE_BLOCK
    owner = count_le(pends, block_row0)
    block_e = jnp.minimum(owner, N_EXPERTS - 1)
    block_valid = jnp.clip(lookup(pstarts + counts, owner) - block_row0, 0, MOE_BLOCK)
    block_valid = jnp.where(owner < N_EXPERTS, block_valid, 0).astype(jnp.int32)
    return dest.astype(jnp.int32), block_e, block_valid, n_rows


def _encoder_group(x, mem, w):
    batch, seq, _ = x.shape
    T = batch * seq
    *qkv_groups, qkv_win = _proj(x, w["g_mix"], w["w_in"])

    slopes_a = _alibi_slopes(A_HEADS).reshape(A_GROUPS, A_HEADS_PER_GROUP)
    o_groups, lse_groups = [], []
    for g, (window, r) in enumerate(A_CONFIGS):
        o, lse = _banded_attention(qkv_groups[g], half_w=window // (2 * r), kv_width=GROUP_W,
                                   slopes=slopes_a[g] * np.float32(r))
        o_groups.append(o)
        lse_groups.append(lse)
    (o_b,) = _banded_attention(qkv_win.reshape(batch, 1, seq, WIN_W), half_w=B_HALF_WINDOW,
                               kv_width=KV2_W, slopes=_alibi_slopes(B_HEADS), sink=w["sink_b"],
                               want_lse=False)

    kvm = _memkv(mem.reshape(-1, D_MODEL), w["g_mem"], w["w_ckv"])
    mix_w = [w[k] for k in ("g_mix", "w_gate", "b_gate", "w_branch", "w_out", "g_xattn", "w_cq",
                            "w_co", "g_ffn", "w_route", "b_route")]
    x2, *h_slabs, route, counts_rec = _mix(x, o_groups, lse_groups, o_b, kvm, mix_w)

    dest, block_e, block_valid, n_rows = _routing_tables(route, counts_rec, T)
    xs_slabs = [_scatter_rows(h, dest, n_rows) for h in h_slabs]
    ys_slabs = _experts(xs_slabs, block_e, block_valid, w["w_gu"], w["w_down"])
    y = _combine(x2, route, w["g_final"], dest, ys_slabs)
    return y.reshape(batch, seq, D_MODEL)


def _prep_weights(g_mix, w_in, sink_b, w_gate, b_gate, w_branch, w_out, g_xattn, g_mem, w_cq, w_ckv,
                  w_co, g_ffn, w_rg, b_rg, w_re, b_re, w_gu, w_down, g_final):
    scale = HEAD_DIM ** -0.5
    aw = A_WIDTH
    qa, ka, va = w_in[:, :aw] * scale, w_in[:, aw:2 * aw], w_in[:, 2 * aw:3 * aw]
    qb = w_in[:, 3 * aw:3 * aw + B_Q] * scale
    kb = w_in[:, 3 * aw + B_Q:3 * aw + B_Q + B_KV]
    vb = w_in[:, 3 * aw + B_Q + B_KV:]
    twice = lambda t: jnp.repeat(t.reshape(D_MODEL, B_KV_HEADS, 1, HEAD_DIM), 2, axis=2).reshape(D_MODEL, KV2_W)
    group = lambda t, g: t[:, g * GROUP_W:(g + 1) * GROUP_W]
    cols = [group(t, g) for g in range(A_GROUPS) for t in (qa, ka, va)] + [qb, twice(kb), twice(vb)]
    w_in_x = jnp.concatenate(cols, axis=1).astype(BF16)
    w_route = jnp.zeros((D_MODEL, ROUTE_W), F32)
    w_route = w_route.at[:, :N_GROUPS].set(w_rg).at[:, EXPERT_LANE0:EXPERT_LANE0 + N_EXPERTS].set(w_re)
    b_route = jnp.zeros((1, ROUTE_W), F32)
    b_route = b_route.at[0, :N_GROUPS].set(b_rg).at[0, EXPERT_LANE0:EXPERT_LANE0 + N_EXPERTS].set(b_re)
    w_route_hi = w_route.astype(BF16)
    w_route = jnp.concatenate([w_route_hi, (w_route - w_route_hi.astype(F32)).astype(BF16)], axis=1)
    vec = lambda v: v.reshape(1, -1).astype(F32)
    return dict(
        g_mix=vec(g_mix), w_in=w_in_x, sink_b=sink_b.astype(F32),
        w_gate=w_gate.astype(BF16), b_gate=vec(b_gate), w_branch=w_branch.astype(BF16),
        w_out=w_out.astype(BF16), g_xattn=vec(g_xattn), g_mem=vec(g_mem), w_cq=w_cq.astype(BF16),
        w_ckv=w_ckv.astype(BF16), w_co=w_co.astype(BF16), g_ffn=vec(g_ffn),
        w_route=w_route, b_route=b_route, w_gu=w_gu.astype(BF16), w_down=w_down.astype(BF16),
        g_final=vec(g_final))


def kernel(x_prompt, x_sample, mem_prompt, mem_sample, g_mix, w_in, sink_b, w_gate, b_gate, w_branch,
           w_out, g_xattn, g_mem, w_cq, w_ckv, w_co, g_ffn, w_rg, b_rg, w_re, b_re, w_gu, w_down,
           g_final):
    assert g_mix.shape[0] == 1, "single-layer encoder"
    w = _prep_weights(g_mix[0], w_in[0], sink_b[0], w_gate[0], b_gate[0], w_branch[0], w_out[0],
                      g_xattn[0], g_mem[0], w_cq[0], w_ckv[0], w_co[0], g_ffn[0], w_rg[0], b_rg[0],
                      w_re[0], b_re[0], w_gu[0], w_down[0], g_final)
    y_sample = _encoder_group(x_sample, mem_sample, w)
    y_prompt = _encoder_group(x_prompt, mem_prompt, w)
    return (y_prompt, y_sample)
```

```python
import functools

import numpy as np
import jax
import jax.numpy as jnp
from jax import lax
from jax.experimental import pallas as pl
from jax.experimental.pallas import tpu as pltpu
from jax.experimental.pallas import tpu_sc as plsc

F32 = jnp.float32
BF16 = jnp.bfloat16

D_MODEL = 1024
HEAD_DIM = 64
A_CONFIGS = ((128, 1), (512, 4), (2048, 16))
A_GROUPS = 3
A_HEADS_PER_GROUP = 8
A_HEADS = A_GROUPS * A_HEADS_PER_GROUP
A_WIDTH = A_HEADS * HEAD_DIM
B_HEADS = 8
B_KV_HEADS = 2
B_HALF_WINDOW = 128
B_Q = B_HEADS * HEAD_DIM
B_KV = B_KV_HEADS * HEAD_DIM
X_HEADS = 4
X_HEAD_DIM = 128
X_WIDTH = X_HEADS * X_HEAD_DIM
N_GROUPS = 4
EXPERTS_PER_GROUP = 8
N_EXPERTS = N_GROUPS * EXPERTS_PER_GROUP
EXPERT_FF = 512
MOE_BLOCK = 256
EPS = 1e-6
NEG = -1e30

LANES = 128
GROUP_W = A_HEADS_PER_GROUP * HEAD_DIM
PAIR_W = 2 * HEAD_DIM
N_PAIRS = GROUP_W // PAIR_W
QKV_W = 3 * GROUP_W
KV2_W = 2 * B_KV
WIN_W = B_Q + 2 * KV2_W
PROJ_W = A_GROUPS * QKV_W + WIN_W
ROUTE_W = LANES
EXPERT_LANE0 = 32
GROUP_ROWS = 8
VMEM_LIMIT = 52 * 1024 * 1024

ROW_TILE = 256
PROJ_TILE = 512
MIX_TILE = 512
MIX_SKEW = 3
MIX_SUB = 256
Q_STEPS = 8
Q_UNROLL = 4
GATHER_WINDOW = 128
SC_CORES = 2
BLOCKS_PER_STEP = 2
Y_SLABS = 2
def _rms(x, g):
    ms = jnp.mean(x * x, axis=-1, keepdims=True)
    return x * lax.rsqrt(ms + EPS) * g


def _alibi_slopes(n):
    return 2.0 ** (-8.0 * np.arange(1, n + 1, dtype=np.float32) / n)


def _const_spec(shape):
    return pl.BlockSpec(shape, lambda *_: (0,) * len(shape), pipeline_mode=pl.Buffered(1))


def _proj_kernel(x_ref, g_ref, w_ref, *refs):
    o_refs, win_ref, h_ref = refs[:A_GROUPS], refs[A_GROUPS], refs[A_GROUPS + 1]
    tm = x_ref.shape[0]
    h32 = _rms(x_ref[...], g_ref[...])
    h_nat = h32.astype(BF16)
    n_slabs = h_ref.shape[0]
    for s in range(n_slabs):
        h_ref[s] = h32[:, s * LANES:(s + 1) * LANES]
    for g, (_, r) in enumerate(A_CONFIGS):
        n = tm // r
        if r == 1:
            h = h_nat
        else:
            h = jnp.concatenate(
                [jnp.concatenate([h_ref[s, pl.ds(c, n, stride=r), :] for c in range(r)], axis=0)
                 for s in range(n_slabs)], axis=1).astype(BF16)
        for j in range(3):
            cols = slice(g * QKV_W + j * GROUP_W, g * QKV_W + (j + 1) * GROUP_W)
            res = jnp.dot(h, w_ref[:, cols], preferred_element_type=F32).astype(BF16)
            for c in range(r):
                o_refs[g][c, :, j * GROUP_W:(j + 1) * GROUP_W] = res[c * n:(c + 1) * n]
    for j in range(WIN_W // GROUP_W):
        cols = slice(A_GROUPS * QKV_W + j * GROUP_W, A_GROUPS * QKV_W + (j + 1) * GROUP_W)
        win_ref[:, j * GROUP_W:(j + 1) * GROUP_W] = jnp.dot(
            h_nat, w_ref[:, cols], preferred_element_type=F32).astype(BF16)


def _proj(x, g, w):
    batch, seq, _ = x.shape
    tm = PROJ_TILE
    out_specs = [pl.BlockSpec((None, r, tm // r, QKV_W), lambda b, i: (b, 0, i, 0)) for _, r in A_CONFIGS]
    out_shape = [jax.ShapeDtypeStruct((batch, r, seq // r, QKV_W), BF16) for _, r in A_CONFIGS]
    out_specs.append(pl.BlockSpec((None, tm, WIN_W), lambda b, i: (b, i, 0)))
    out_shape.append(jax.ShapeDtypeStruct((batch, seq, WIN_W), BF16))
    return pl.pallas_call(
        _proj_kernel,
        grid=(batch, seq // tm),
        in_specs=[pl.BlockSpec((None, tm, D_MODEL), lambda b, i: (b, i, 0)),
                  _const_spec((1, D_MODEL)),
                  _const_spec((D_MODEL, PROJ_W))],
        out_specs=out_specs,
        out_shape=out_shape,
        scratch_shapes=[pltpu.VMEM((D_MODEL // LANES, tm, LANES), F32)],
        compiler_params=pltpu.CompilerParams(dimension_semantics=("arbitrary",) * 2,
                                             vmem_limit_bytes=VMEM_LIMIT),
        name="proj",
    )(x, g, w)


def _attn_kernel(*refs, qb, kb, q_steps, m_len, half_w, offsets, kv_shared, has_sink, want_lse):
    refs = list(refs)
    sink_ref = refs.pop(0) if has_sink else None
    bias_ref, q_ref, k_ref, v_ref, o_ref = refs[:5]
    lse_ref = refs[5] if want_lse else None
    lo_q = lax.broadcasted_iota(jnp.int32, (qb, PAIR_W), 1) < HEAD_DIM
    first_head = lax.broadcasted_iota(jnp.int32, (2 * qb, 1), 0) < qb
    zeros_q = jnp.zeros((qb, PAIR_W), BF16)

    def q_block(it, carry):
        cc, qi = it // q_steps, it % q_steps
        gi = pl.program_id(2) * q_steps + qi
        ks = pl.multiple_of(jnp.clip(gi * qb - half_w, 0, m_len - kb), 16)
        off = gi * qb - ks
        var = sum(jnp.where(off == o, n, 0) for n, o in enumerate(offsets))
        rows = pl.ds(pl.multiple_of(qi * qb, qb), qb)
        for j in range(N_PAIRS):
            cols = slice(j * PAIR_W, (j + 1) * PAIR_W)
            jc = (j // 2) if kv_shared else j
            kcols = slice(jc * PAIR_W, (jc + 1) * PAIR_W)
            qp = q_ref[cc, rows, cols]
            q_st = jnp.concatenate([jnp.where(lo_q, qp, zeros_q), jnp.where(lo_q, zeros_q, qp)], axis=0)
            s = lax.dot_general(q_st, k_ref[cc, pl.ds(ks, kb), kcols], (((1,), (1,)), ((), ())),
                                preferred_element_type=F32) + bias_ref[var, j]
            m = jnp.max(s, axis=-1, keepdims=True)
            if has_sink:
                sk = jnp.where(first_head, sink_ref[2 * j], sink_ref[2 * j + 1])
                m = jnp.maximum(m, sk)
            e = jnp.exp(s - m)
            den = jnp.sum(e, axis=-1, keepdims=True)
            if has_sink:
                den = den + jnp.exp(sk - m)
            o2 = jnp.dot(e.astype(BF16), v_ref[cc, pl.ds(ks, kb), kcols], preferred_element_type=F32) / den
            o_ref[cc, rows, cols] = jnp.where(lo_q, o2[:qb], o2[qb:]).astype(BF16)
            if want_lse:
                l2 = m + jnp.log(den)
                lse_ref[cc, rows, cols] = jnp.where(lo_q, l2[:qb], l2[qb:])
        return carry

    n_iter = q_ref.shape[0] * q_steps
    lax.fori_loop(0, n_iter, q_block, 0, unroll=min(Q_UNROLL, n_iter))


def _bias_tables(offsets, slopes, qb, kb, half_w):
    rel = jnp.arange(qb, dtype=jnp.int32)[:, None] - jnp.arange(kb, dtype=jnp.int32)[None, :]
    dist = jnp.abs(rel[None] + jnp.asarray(offsets, jnp.int32)[:, None, None])
    bias = -jnp.asarray(slopes, F32)[None, :, None, None] * dist.astype(F32)[:, None]
    bias = jnp.where((dist <= half_w)[:, None], bias, NEG)
    return bias.reshape(len(offsets), N_PAIRS, 2 * qb, kb)


def _banded_attention(qkv, *, half_w, kv_width, slopes, sink=None, want_lse=True):
    batch, r, m_len, _ = qkv.shape
    qb = min(2 * half_w, m_len)
    kb = min(qb + 2 * half_w, m_len)
    nq = m_len // qb
    q_steps = min(Q_STEPS, nq)
    cb = min(r, Q_STEPS // q_steps)
    kv_shared = kv_width != GROUP_W
    has_sink = sink is not None
    k_block = GROUP_W // kv_width
    offsets = sorted({i * qb - min(max(i * qb - half_w, 0), m_len - kb) for i in range(nq)})
    bias = _bias_tables(offsets, slopes, qb, kb, half_w)

    in_specs = []
    args = []
    if has_sink:
        in_specs.append(pl.BlockSpec(memory_space=pltpu.SMEM))
        args.append(sink)
    in_specs += [
        _const_spec(bias.shape),
        pl.BlockSpec((None, cb, q_steps * qb, GROUP_W), lambda b, c, i: (b, c, i, 0)),
        pl.BlockSpec((None, cb, m_len, kv_width), lambda b, c, i: (b, c, 0, k_block)),
        pl.BlockSpec((None, cb, m_len, kv_width), lambda b, c, i: (b, c, 0, k_block + 1)),
    ]
    args += [bias, qkv, qkv, qkv]
    out_spec = pl.BlockSpec((None, cb, q_steps * qb, GROUP_W), lambda b, c, i: (b, c, i, 0))
    out_specs = [out_spec]
    out_shape = [jax.ShapeDtypeStruct((batch, r, m_len, GROUP_W), BF16)]
    if want_lse:
        out_specs.append(out_spec)
        out_shape.append(jax.ShapeDtypeStruct((batch, r, m_len, GROUP_W), F32))
    kern = functools.partial(_attn_kernel, qb=qb, kb=kb, q_steps=q_steps, m_len=m_len, half_w=half_w,
                             offsets=tuple(offsets), kv_shared=kv_shared, has_sink=has_sink,
                             want_lse=want_lse)
    return pl.pallas_call(
        kern,
        grid=(batch, r // cb, nq // q_steps),
        in_specs=in_specs,
        out_specs=out_specs,
        out_shape=out_shape,
        compiler_params=pltpu.CompilerParams(dimension_semantics=("arbitrary",) * 3,
                                             vmem_limit_bytes=VMEM_LIMIT),
        name=f"attn_r{r}_w{half_w}",
    )(*args)


def _memkv_kernel(m_ref, g_ref, w_ref, o_ref):
    h = _rms(m_ref[...], g_ref[...]).astype(BF16)
    o_ref[...] = jnp.dot(h, w_ref[...], preferred_element_type=F32).astype(BF16)


def _memkv(mem2d, g, w):
    R = mem2d.shape[0]
    return pl.pallas_call(
        _memkv_kernel,
        grid=(R // ROW_TILE,),
        in_specs=[pl.BlockSpec((ROW_TILE, D_MODEL), lambda i: (i, 0)),
                  _const_spec((1, D_MODEL)),
                  _const_spec((D_MODEL, 2 * X_WIDTH))],
        out_specs=pl.BlockSpec((ROW_TILE, 2 * X_WIDTH), lambda i: (i, 0)),
        out_shape=jax.ShapeDtypeStruct((R, 2 * X_WIDTH), BF16),
        compiler_params=pltpu.CompilerParams(dimension_semantics=("arbitrary",)),
        name="memkv",
    )(mem2d, g, w)


def _token_order(src_ref, dst_ref, m0, n):
    r = src_ref.shape[0]
    if r == 1:
        return src_ref[0, m0:m0 + n].astype(F32)
    n_slabs = dst_ref.shape[0]
    for c in range(r):
        rows = src_ref[c, m0:m0 + n].astype(F32)
        for s in range(n_slabs):
            dst_ref[s, pl.ds(m0 * r + c, n, stride=r), :] = rows[:, s * LANES:(s + 1) * LANES]
    return jnp.concatenate([dst_ref[s, m0 * r:(m0 + n) * r] for s in range(n_slabs)], axis=1)


def _mix_rows(t0, tm, x_ref, o_refs, l_refs, ob_ref, kv_ref,
              g_mix_ref, w_gate_ref, b_gate_ref, w_br_ref, w_out_ref,
              g_x_ref, w_cq_ref, w_co_ref, g_ffn_ref, w_rt_ref, b_rt_ref,
              x2_ref, hp_refs, route_ref, tri_ref, carry_ref, order_refs):
    rows = slice(t0, t0 + tm)
    x = x_ref[rows]
    h1 = _rms(x, g_mix_ref[...]).astype(BF16)
    yield

    by_class = lambda ref, scratch: _token_order(ref, scratch, t0 // ref.shape[0], tm // ref.shape[0])
    l0, l1, l2 = (by_class(l, s) for l, s in zip(l_refs, order_refs[:3]))
    lm = jnp.maximum(jnp.maximum(l0, l1), l2)
    e0, e1, e2 = jnp.exp(l0 - lm), jnp.exp(l1 - lm), jnp.exp(l2 - lm)
    den = e0 + e1 + e2
    o0, o1, o2 = (by_class(o, s) for o, s in zip(o_refs, order_refs[3:]))
    oa = (e0 / den) * o0 + (e1 / den) * o1 + (e2 / den) * o2
    yield
    br_a = jnp.dot(oa.astype(BF16), w_br_ref[:GROUP_W, :], preferred_element_type=F32)
    br_b = jnp.dot(ob_ref[rows], w_br_ref[GROUP_W:, :], preferred_element_type=F32)
    ga = jax.nn.sigmoid(jnp.dot(h1, w_gate_ref[:, :D_MODEL], preferred_element_type=F32)
                        + b_gate_ref[:, :D_MODEL])
    merged = ga * br_a
    gb = jax.nn.sigmoid(jnp.dot(h1, w_gate_ref[:, D_MODEL:], preferred_element_type=F32)
                        + b_gate_ref[:, D_MODEL:])
    merged = merged + gb * br_b
    yield
    x1 = x + jnp.dot(merged.astype(BF16), w_out_ref[...], preferred_element_type=F32)
    yield

    h2 = _rms(x1, g_x_ref[...]).astype(BF16)
    q = jnp.dot(h2, w_cq_ref[...], preferred_element_type=F32).astype(BF16)
    heads = []
    for h in range(X_HEADS):
        cols = slice(h * X_HEAD_DIM, (h + 1) * X_HEAD_DIM)
        kh = kv_ref[:, cols]
        vh = kv_ref[:, X_WIDTH + h * X_HEAD_DIM:X_WIDTH + (h + 1) * X_HEAD_DIM]
        s = lax.dot_general(q[:, cols], kh, (((1,), (1,)), ((), ())),
                            preferred_element_type=F32) * (X_HEAD_DIM ** -0.5)
        m = jnp.max(s, axis=-1, keepdims=True)
        e = jnp.exp(s - m)
        p = e / jnp.sum(e, axis=-1, keepdims=True)
        heads.append(jnp.dot(p.astype(BF16), vh, preferred_element_type=F32))
    o = jnp.concatenate(heads, axis=1).astype(BF16)
    yield
    x2 = x1 + jnp.dot(o, w_co_ref[...], preferred_element_type=F32)
    x2_ref[rows] = x2
    yield

    h3 = _rms(x2, g_ffn_ref[...])
    h3_hi = h3.astype(BF16)
    packed = lax.bitcast_convert_type(_pack_bf16_pairs(h3_hi.astype(F32)), jnp.int32)
    slab_w = packed.shape[1] // len(hp_refs)
    for n, hp_ref in enumerate(hp_refs):
        hp_ref[rows] = packed[:, n * slab_w:(n + 1) * slab_w]
    h3_lo = (h3 - h3_hi.astype(F32)).astype(BF16)
    by_hi = jnp.dot(h3_hi, w_rt_ref[...], preferred_element_type=F32)
    logits = (by_hi[:, :ROUTE_W] + by_hi[:, ROUTE_W:]
              + jnp.dot(h3_lo, w_rt_ref[:, :ROUTE_W], preferred_element_type=F32)) + b_rt_ref[...]
    lt = logits.T
    grow = lax.broadcasted_iota(jnp.int32, (GROUP_ROWS, tm), 0)
    gl = jnp.where(grow < N_GROUPS, lt[:GROUP_ROWS], -jnp.inf)
    gmax = jnp.max(gl, axis=0, keepdims=True)
    gidx = jnp.min(jnp.where(gl == gmax, grow, GROUP_ROWS), axis=0, keepdims=True)
    pg_sel = 1.0 / jnp.sum(jnp.exp(gl - gmax), axis=0, keepdims=True)
    erow = lax.broadcasted_iota(jnp.int32, (N_EXPERTS, tm), 0)
    in_group = (erow >= gidx * EXPERTS_PER_GROUP) & (erow < (gidx + 1) * EXPERTS_PER_GROUP)
    el = jnp.where(in_group, lt[EXPERT_LANE0:EXPERT_LANE0 + N_EXPERTS], -jnp.inf)
    emax1 = jnp.max(el, axis=0, keepdims=True)
    i1 = jnp.min(jnp.where(el == emax1, erow, N_EXPERTS), axis=0, keepdims=True)
    el2 = jnp.where(erow == i1, -jnp.inf, el)
    emax2 = jnp.max(el2, axis=0, keepdims=True)
    i2 = jnp.min(jnp.where(el2 == emax2, erow, N_EXPERTS), axis=0, keepdims=True)
    t2 = jnp.exp(emax2 - emax1)
    w1 = pg_sel / (1.0 + t2)
    w2 = pg_sel * t2 / (1.0 + t2)
    yield

    oh1 = erow == i1
    oh2 = erow == i2
    ohs = jnp.where(oh1 | oh2, 1.0, 0.0)
    before = lax.dot_general(ohs.astype(BF16), tri_ref[...], (((1,), (1,)), ((), ())),
                             preferred_element_type=F32) + carry_ref[...]
    rank1 = jnp.sum(jnp.where(oh1, before, 0.0), axis=0, keepdims=True)
    rank2 = jnp.sum(jnp.where(oh2, before, 0.0), axis=0, keepdims=True)
    carry_ref[...] = carry_ref[...] + jnp.sum(ohs, axis=1, keepdims=True)

    rec_t = jnp.concatenate([i1.astype(F32), i2.astype(F32), w1, w2, rank1, rank2,
                             jnp.zeros((ROUTE_W - 6, tm), F32)], axis=0)
    route_ref[rows] = rec_t.T


def _mix_kernel(x_ref, o0_ref, o1_ref, o2_ref, l0_ref, l1_ref, l2_ref, ob_ref, *refs):
    n_w = 12
    w_refs, refs = refs[:n_w], refs[n_w:]
    x2_ref, hp_refs, refs = refs[0], refs[1:1 + Y_SLABS], refs[1 + Y_SLABS:]
    route_ref, counts_ref, tri_ref, carry_ref = refs[:4]
    order_refs = refs[4:]
    sub = tri_ref.shape[0]

    @pl.when((pl.program_id(0) == 0) & (pl.program_id(1) == 0))
    def _():
        row = lax.broadcasted_iota(jnp.int32, (sub, sub), 0)
        col = lax.broadcasted_iota(jnp.int32, (sub, sub), 1)
        tri_ref[...] = jnp.where(row > col, 1.0, 0.0).astype(BF16)
        carry_ref[...] = jnp.zeros_like(carry_ref)

    chains = [_mix_rows(t0, sub, x_ref, (o0_ref, o1_ref, o2_ref), (l0_ref, l1_ref, l2_ref), ob_ref, *w_refs,
                        x2_ref, hp_refs, route_ref, tri_ref, carry_ref, order_refs)
              for t0 in range(0, x_ref.shape[0], sub)]
    finished = [False] * len(chains)
    rounds = 0
    while not all(finished):
        for n, chain in enumerate(chains):
            if MIX_SKEW * n <= rounds and not finished[n]:
                finished[n] = next(chain, "end") == "end"
        rounds += 1
    counts_ref[...] = jnp.broadcast_to(carry_ref[...], counts_ref.shape)


def _mix(x, o_groups, lse_groups, o_b, kvm, wts):
    batch, seq, _ = x.shape
    T = batch * seq
    tm = MIX_TILE
    steps = seq // tm
    mem_tokens = kvm.shape[0] // batch
    row = lambda w: pl.BlockSpec((tm, w), lambda b, i: (b * steps + i, 0))
    by_class = [pl.BlockSpec((None, r, tm // r, GROUP_W), lambda b, i: (b, 0, i, 0)) for _, r in A_CONFIGS]
    in_specs = ([pl.BlockSpec((None, tm, D_MODEL), lambda b, i: (b, i, 0))] + by_class + by_class
                + [pl.BlockSpec((None, None, tm, GROUP_W), lambda b, i: (b, 0, i, 0)),
                   pl.BlockSpec((mem_tokens, 2 * X_WIDTH), lambda b, i: (b, 0))]
                + [_const_spec(w.shape) for w in wts])
    return pl.pallas_call(
        _mix_kernel,
        grid=(batch, steps),
        in_specs=in_specs,
        out_specs=[row(D_MODEL)] + [row(D_MODEL // 2 // Y_SLABS)] * Y_SLABS
        + [row(ROUTE_W), pl.BlockSpec((N_EXPERTS, ROUTE_W), lambda b, i: (0, 0))],
        out_shape=[jax.ShapeDtypeStruct((T, D_MODEL), F32)]
        + [jax.ShapeDtypeStruct((T, D_MODEL // 2 // Y_SLABS), jnp.int32)] * Y_SLABS
        + [jax.ShapeDtypeStruct((T, ROUTE_W), F32),
           jax.ShapeDtypeStruct((N_EXPERTS, ROUTE_W), F32)],
        scratch_shapes=[pltpu.VMEM((MIX_SUB, MIX_SUB), BF16), pltpu.VMEM((N_EXPERTS, 1), F32)]
        + [pltpu.VMEM((GROUP_W // LANES, tm, LANES), F32)] * (2 * A_GROUPS),
        compiler_params=pltpu.CompilerParams(dimension_semantics=("arbitrary",) * 2,
                                             vmem_limit_bytes=VMEM_LIMIT),
        name="mix",
    )(x, *o_groups, *lse_groups, o_b, kvm, *wts)


def _pack_bf16_pairs(x):
    w = x.shape[1] // 2
    lo = lax.bitcast_convert_type(x[:, :w], jnp.uint32) >> 16
    hi = lax.bitcast_convert_type(x[:, w:], jnp.uint32) & jnp.uint32(0xFFFF0000)
    return lo | hi


def _unpack_bf16_pairs(u):
    lo = lax.bitcast_convert_type(u << 16, F32)
    hi = lax.bitcast_convert_type(u & jnp.uint32(0xFFFF0000), F32)
    return jnp.concatenate([lo, hi], axis=1)


def _scatter_rows(rows, dest, n_rows):
    T, width = rows.shape
    mesh = plsc.VectorSubcoreMesh(core_axis_name="core", subcore_axis_name="subcore")

    @pl.kernel(out_type=jax.ShapeDtypeStruct((n_rows, width), rows.dtype), mesh=mesh, scratch_types=[])
    def scatter_kernel(rows_hbm, idx0_hbm, idx1_hbm, out_hbm):
        def body(rows_vmem, idx0_vmem, idx1_vmem):
            pltpu.sync_copy(rows_vmem, out_hbm.at[idx0_vmem.at[0]])
            pltpu.sync_copy(rows_vmem, out_hbm.at[idx1_vmem.at[0]])

        per_core = T // GATHER_WINDOW // SC_CORES
        idx_spec = pl.BlockSpec((1, GATHER_WINDOW), index_map=lambda c, i: (0, c * per_core + i))
        pltpu.emit_pipeline(
            body,
            grid=(SC_CORES, per_core),
            in_specs=[pl.BlockSpec((GATHER_WINDOW, width), index_map=lambda c, i: (c * per_core + i, 0)),
                      idx_spec, idx_spec],
            out_specs=[],
            core_axis_name=("core", "subcore"),
            dimension_semantics=(pltpu.PARALLEL, pltpu.PARALLEL),
        )(rows_hbm, idx0_hbm, idx1_hbm)

    return scatter_kernel(rows, dest[0].reshape(1, T), dest[1].reshape(1, T))


def _expert_block(rows, n_valid, xs_refs, wgu_ref, wdn_ref, ys_refs):
    packed_in = jnp.concatenate([r[rows] for r in xs_refs], axis=1)
    row = lax.broadcasted_iota(jnp.int32, packed_in.shape, 0)
    packed_in = jnp.where(row < n_valid, packed_in, 0)
    xb = _unpack_bf16_pairs(lax.bitcast_convert_type(packed_in, jnp.uint32)).astype(BF16)
    gu = jnp.dot(xb, wgu_ref[...], preferred_element_type=F32)
    gate, up = gu[:, :EXPERT_FF], gu[:, EXPERT_FF:]
    act = (gate * jax.nn.sigmoid(gate) * up).astype(BF16)
    y = jnp.dot(act, wdn_ref[...], preferred_element_type=F32)
    packed = lax.bitcast_convert_type(_pack_bf16_pairs(y.astype(BF16).astype(F32)), jnp.int32)
    w = packed.shape[1] // len(ys_refs)
    for n, ys_ref in enumerate(ys_refs):
        ys_ref[rows] = packed[:, n * w:(n + 1) * w]


def _expert_kernel(be_ref, nv_ref, *refs):
    xs_refs, w_refs, ys_refs = refs[:Y_SLABS], refs[Y_SLABS:-Y_SLABS], refs[-Y_SLABS:]
    for j in range(BLOCKS_PER_STEP):
        _expert_block(slice(j * MOE_BLOCK, (j + 1) * MOE_BLOCK),
                      nv_ref[pl.program_id(0) * BLOCKS_PER_STEP + j],
                      xs_refs, w_refs[2 * j], w_refs[2 * j + 1], ys_refs)


def _experts(xs_slabs, block_e, block_valid, w_gu, w_down):
    P, slab_w = xs_slabs[0].shape
    n_blocks = P // MOE_BLOCK
    bps = BLOCKS_PER_STEP
    slab_spec = pl.BlockSpec((bps * MOE_BLOCK, slab_w), lambda b, be, nv: (b, 0))
    w_specs, w_args = [], []
    for j in range(bps):
        w_specs += [
            pl.BlockSpec((None, D_MODEL, 2 * EXPERT_FF), lambda b, be, nv, j=j: (be[b * bps + j], 0, 0)),
            pl.BlockSpec((None, EXPERT_FF, D_MODEL), lambda b, be, nv, j=j: (be[b * bps + j], 0, 0))]
        w_args += [w_gu, w_down]
    grid_spec = pltpu.PrefetchScalarGridSpec(
        num_scalar_prefetch=2,
        grid=(n_blocks // bps,),
        in_specs=[slab_spec] * Y_SLABS + w_specs,
        out_specs=[slab_spec] * Y_SLABS,
    )
    return pl.pallas_call(
        _expert_kernel,
        grid_spec=grid_spec,
        out_shape=[jax.ShapeDtypeStruct((P, slab_w), jnp.int32)] * Y_SLABS,
        compiler_params=pltpu.CompilerParams(dimension_semantics=("arbitrary",),
                                             vmem_limit_bytes=VMEM_LIMIT),
        name="experts",
    )(block_e, block_valid, *xs_slabs, *w_args)


def _gather_rows(table, indices):
    n, width = indices.shape[0], table.shape[1]
    mesh = plsc.VectorSubcoreMesh(core_axis_name="core", subcore_axis_name="subcore")

    @pl.kernel(out_type=jax.ShapeDtypeStruct((n, width), table.dtype), mesh=mesh)
    def gather_kernel(table_hbm, idx_hbm, out_hbm):
        def body(idx_vmem, out_vmem):
            pltpu.sync_copy(table_hbm.at[idx_vmem.at[0]], out_vmem)

        per_core = n // GATHER_WINDOW // SC_CORES
        pltpu.emit_pipeline(
            body,
            grid=(SC_CORES, per_core),
            in_specs=[pl.BlockSpec((1, GATHER_WINDOW), index_map=lambda c, i: (0, c * per_core + i))],
            out_specs=[pl.BlockSpec((GATHER_WINDOW, width), index_map=lambda c, i: (c * per_core + i, 0))],
            core_axis_name=("core", "subcore"),
            dimension_semantics=(pltpu.PARALLEL, pltpu.PARALLEL),
        )(idx_hbm, out_hbm)

    return gather_kernel(table, indices.reshape(1, n))


def _combine_kernel(x2_ref, route_ref, g_ref, *refs):
    y_refs, out_ref = refs[:-1], refs[-1]
    route = route_ref[...]

    def expert_rows(slabs):
        packed = jnp.concatenate([s[...] for s in slabs], axis=1)
        return _unpack_bf16_pairs(lax.bitcast_convert_type(packed, jnp.uint32))

    moe = expert_rows(y_refs[:Y_SLABS]) * route[:, 2:3] + expert_rows(y_refs[Y_SLABS:]) * route[:, 3:4]
    out_ref[...] = _rms(x2_ref[...] + moe, g_ref[...])


def _combine(x2, route, g_final, dest, ys_slabs):
    T = x2.shape[0]
    tm = MIX_TILE
    n_steps = T // tm
    rows = [_gather_rows(ys, dest.reshape(-1)) for ys in ys_slabs]
    slab_w = rows[0].shape[1]
    first = [pl.BlockSpec((tm, slab_w), lambda i: (i, 0))] * Y_SLABS
    second = [pl.BlockSpec((tm, slab_w), lambda i: (i + n_steps, 0))] * Y_SLABS
    return pl.pallas_call(
        _combine_kernel,
        grid=(n_steps,),
        in_specs=[pl.BlockSpec((tm, D_MODEL), lambda i: (i, 0)),
                  pl.BlockSpec((tm, ROUTE_W), lambda i: (i, 0)),
                  _const_spec((1, D_MODEL))] + first + second,
        out_specs=pl.BlockSpec((tm, D_MODEL), lambda i: (i, 0)),
        out_shape=jax.ShapeDtypeStruct((T, D_MODEL), F32),
        compiler_params=pltpu.CompilerParams(dimension_semantics=("arbitrary",)),
        name="combine",
    )(x2, route, g_final, *rows, *rows)


def _routing_tables(route, counts_rec, n_tokens):
    n_slots = 2 * n_tokens
    n_blocks = n_slots // MOE_BLOCK + N_EXPERTS
    n_rows = n_blocks * MOE_BLOCK
    counts = counts_rec[:, 0].astype(jnp.int32)
    padded = (counts + MOE_BLOCK - 1) // MOE_BLOCK * MOE_BLOCK
    pends = jnp.cumsum(padded)
    pstarts = pends - padded
    expert = route[:, 0:2].astype(jnp.int32)
    rank = route[:, 4:6].astype(jnp.int32)
    lookup = lambda table, idx: jnp.sum(jnp.where(idx[..., None] == jnp.arange(N_EXPERTS), table, 0), axis=-1)
    count_le = lambda sorted_vals, q: jnp.sum(sorted_vals[None, :] <= q[:, None], axis=-1).astype(jnp.int32)
    dest = (lookup(pstarts, expert) + rank).T
    block_row0 = jnp.arange(n_blocks, dtype=jnp.int32) * MOE_BLOCK
    owner = count_le(pends, block_row0)
    block_e = jnp.minimum(owner, N_EXPERTS - 1)
    block_valid = jnp.clip(lookup(pstarts + counts, owner) - block_row0, 0, MOE_BLOCK)
    block_valid = jnp.where(owner < N_EXPERTS, block_valid, 0).astype(jnp.int32)
    return dest.astype(jnp.int32), block_e, block_valid, n_rows


def _encoder_group(x, mem, w):
    batch, seq, _ = x.shape
    T = batch * seq
    *qkv_groups, qkv_win = _proj(x, w["g_mix"], w["w_in"])

    slopes_a = _alibi_slopes(A_HEADS).reshape(A_GROUPS, A_HEADS_PER_GROUP)
    o_groups, lse_groups = [], []
    for g, (window, r) in enumerate(A_CONFIGS):
        o, lse = _banded_attention(qkv_groups[g], half_w=window // (2 * r), kv_width=GROUP_W,
                                   slopes=slopes_a[g] * np.float32(r))
        o_groups.append(o)
        lse_groups.append(lse)
    (o_b,) = _banded_attention(qkv_win.reshape(batch, 1, seq, WIN_W), half_w=B_HALF_WINDOW,
                               kv_width=KV2_W, slopes=_alibi_slopes(B_HEADS), sink=w["sink_b"],
                               want_lse=False)

    kvm = _memkv(mem.reshape(-1, D_MODEL), w["g_mem"], w["w_ckv"])
    mix_w = [w[k] for k in ("g_mix", "w_gate", "b_gate", "w_branch", "w_out", "g_xattn", "w_cq",
                            "w_co", "g_ffn", "w_route", "b_route")]
    x2, *h_slabs, route, counts_rec = _mix(x, o_groups, lse_groups, o_b, kvm, mix_w)

    dest, block_e, block_valid, n_rows = _routing_tables(route, counts_rec, T)
    xs_slabs = [_scatter_rows(h, dest, n_rows) for h in h_slabs]
    ys_slabs = _experts(xs_slabs, block_e, block_valid, w["w_gu"], w["w_down"])
    y = _combine(x2, route, w["g_final"], dest, ys_slabs)
    return y.reshape(batch, seq, D_MODEL)


def _prep_weights(g_mix, w_in, sink_b, w_gate, b_gate, w_branch, w_out, g_xattn, g_mem, w_cq, w_ckv,
                  w_co, g_ffn, w_rg, b_rg, w_re, b_re, w_gu, w_down, g_final):
    scale = HEAD_DIM ** -0.5
    aw = A_WIDTH
    qa, ka, va = w_in[:, :aw] * scale, w_in[:, aw:2 * aw], w_in[:, 2 * aw:3 * aw]
    qb = w_in[:, 3 * aw:3 * aw + B_Q] * scale
    kb = w_in[:, 3 * aw + B_Q:3 * aw + B_Q + B_KV]
    vb = w_in[:, 3 * aw + B_Q + B_KV:]
    twice = lambda t: jnp.repeat(t.reshape(D_MODEL, B_KV_HEADS, 1, HEAD_DIM), 2, axis=2).reshape(D_MODEL, KV2_W)
    group = lambda t, g: t[:, g * GROUP_W:(g + 1) * GROUP_W]
    cols = [group(t, g) for g in range(A_GROUPS) for t in (qa, ka, va)] + [qb, twice(kb), twice(vb)]
    w_in_x = jnp.concatenate(cols, axis=1).astype(BF16)
    w_route = jnp.zeros((D_MODEL, ROUTE_W), F32)
    w_route = w_route.at[:, :N_GROUPS].set(w_rg).at[:, EXPERT_LANE0:EXPERT_LANE0 + N_EXPERTS].set(w_re)
    b_route = jnp.zeros((1, ROUTE_W), F32)
    b_route = b_route.at[0, :N_GROUPS].set(b_rg).at[0, EXPERT_LANE0:EXPERT_LANE0 + N_EXPERTS].set(b_re)
    w_route_hi = w_route.astype(BF16)
    w_route = jnp.concatenate([w_route_hi, (w_route - w_route_hi.astype(F32)).astype(BF16)], axis=1)
    vec = lambda v: v.reshape(1, -1).astype(F32)
    return dict(
        g_mix=vec(g_mix), w_in=w_in_x, sink_b=sink_b.astype(F32),
        w_gate=w_gate.astype(BF16), b_gate=vec(b_gate), w_branch=w_branch.astype(BF16),
        w_out=w_out.astype(BF16), g_xattn=vec(g_xattn), g_mem=vec(g_mem), w_cq=w_cq.astype(BF16),
        w_ckv=w_ckv.astype(BF16), w_co=w_co.astype(BF16), g_ffn=vec(g_ffn),
        w_route=w_route, b_route=b_route, w_gu=w_gu.astype(BF16), w_down=w_down.astype(BF16),
        g_final=vec(g_final))


def kernel(x_prompt, x_sample, mem_prompt, mem_sample, g_mix, w_in, sink_b, w_gate, b_gate, w_branch,
           w_out, g_xattn, g_mem, w_cq, w_ckv, w_co, g_ffn, w_rg, b_rg, w_re, b_re, w_gu, w_down,
           g_final):
    assert g_mix.shape[0] == 1, "single-layer encoder"
    w = _prep_weights(g_mix[0], w_in[0], sink_b[0], w_gate[0], b_gate[0], w_branch[0], w_out[0],
                      g_xattn[0], g_mem[0], w_cq[0], w_ckv[0], w_co[0], g_ffn[0], w_rg[0], b_rg[0],
                      w_re[0], b_re[0], w_gu[0], w_down[0], g_final)
    y_sample = _encoder_group(x_sample, mem_sample, w)
    y_prompt = _encoder_group(x_prompt, mem_prompt, w)
    return (y_prompt, y_sample)
```

```python
import functools

import numpy as np
import jax
import jax.numpy as jnp
from jax import lax
from jax.experimental import pallas as pl
from jax.experimental.pallas import tpu as pltpu
from jax.experimental.pallas import tpu_sc as plsc

F32 = jnp.float32
BF16 = jnp.bfloat16

D_MODEL = 1024
HEAD_DIM = 64
A_CONFIGS = ((128, 1), (512, 4), (2048, 16))
A_GROUPS = 3
A_HEADS_PER_GROUP = 8
A_HEADS = A_GROUPS * A_HEADS_PER_GROUP
A_WIDTH = A_HEADS * HEAD_DIM
B_HEADS = 8
B_KV_HEADS = 2
B_HALF_WINDOW = 128
B_Q = B_HEADS * HEAD_DIM
B_KV = B_KV_HEADS * HEAD_DIM
X_HEADS = 4
X_HEAD_DIM = 128
X_WIDTH = X_HEADS * X_HEAD_DIM
N_GROUPS = 4
EXPERTS_PER_GROUP = 8
N_EXPERTS = N_GROUPS * EXPERTS_PER_GROUP
EXPERT_FF = 512
MOE_BLOCK = 256
EPS = 1e-6
NEG = -1e30

LANES = 128
GROUP_W = A_HEADS_PER_GROUP * HEAD_DIM
PAIR_W = 2 * HEAD_DIM
N_PAIRS = GROUP_W // PAIR_W
QKV_W = 3 * GROUP_W
KV2_W = 2 * B_KV
WIN_W = B_Q + 2 * KV2_W
PROJ_W = A_GROUPS * QKV_W + WIN_W
ROUTE_W = LANES
EXPERT_LANE0 = 32
GROUP_ROWS = 8
VMEM_LIMIT = 52 * 1024 * 1024

ROW_TILE = 256
PROJ_TILE = 512
MIX_TILE = 512
MIX_SKEW = 3
MIX_SUB = 256
Q_STEPS = 8
Q_UNROLL = 4
GATHER_WINDOW = 128
SC_CORES = 2
BLOCKS_PER_STEP = 4
Y_SLABS = 2
def _rms(x, g):
    ms = jnp.mean(x * x, axis=-1, keepdims=True)
    return x * lax.rsqrt(ms + EPS) * g


def _alibi_slopes(n):
    return 2.0 ** (-8.0 * np.arange(1, n + 1, dtype=np.float32) / n)


def _const_spec(shape):
    return pl.BlockSpec(shape, lambda *_: (0,) * len(shape), pipeline_mode=pl.Buffered(1))


def _proj_kernel(x_ref, g_ref, w_ref, *refs):
    o_refs, win_ref, h_ref = refs[:A_GROUPS], refs[A_GROUPS], refs[A_GROUPS + 1]
    tm = x_ref.shape[0]
    h32 = _rms(x_ref[...], g_ref[...])
    h_nat = h32.astype(BF16)
    n_slabs = h_ref.shape[0]
    for s in range(n_slabs):
        h_ref[s] = h32[:, s * LANES:(s + 1) * LANES]
    for g, (_, r) in enumerate(A_CONFIGS):
        n = tm // r
        if r == 1:
            h = h_nat
        else:
            h = jnp.concatenate(
                [jnp.concatenate([h_ref[s, pl.ds(c, n, stride=r), :] for c in range(r)], axis=0)
                 for s in range(n_slabs)], axis=1).astype(BF16)
        for j in range(3):
            cols = slice(g * QKV_W + j * GROUP_W, g * QKV_W + (j + 1) * GROUP_W)
            res = jnp.dot(h, w_ref[:, cols], preferred_element_type=F32).astype(BF16)
            for c in range(r):
                o_refs[g][c, :, j * GROUP_W:(j + 1) * GROUP_W] = res[c * n:(c + 1) * n]
    for j in range(WIN_W // GROUP_W):
        cols = slice(A_GROUPS * QKV_W + j * GROUP_W, A_GROUPS * QKV_W + (j + 1) * GROUP_W)
        win_ref[:, j * GROUP_W:(j + 1) * GROUP_W] = jnp.dot(
            h_nat, w_ref[:, cols], preferred_element_type=F32).astype(BF16)


def _proj(x, g, w):
    batch, seq, _ = x.shape
    tm = PROJ_TILE
    out_specs = [pl.BlockSpec((None, r, tm // r, QKV_W), lambda b, i: (b, 0, i, 0)) for _, r in A_CONFIGS]
    out_shape = [jax.ShapeDtypeStruct((batch, r, seq // r, QKV_W), BF16) for _, r in A_CONFIGS]
    out_specs.append(pl.BlockSpec((None, tm, WIN_W), lambda b, i: (b, i, 0)))
    out_shape.append(jax.ShapeDtypeStruct((batch, seq, WIN_W), BF16))
    return pl.pallas_call(
        _proj_kernel,
        grid=(batch, seq // tm),
        in_specs=[pl.BlockSpec((None, tm, D_MODEL), lambda b, i: (b, i, 0)),
                  _const_spec((1, D_MODEL)),
                  _const_spec((D_MODEL, PROJ_W))],
        out_specs=out_specs,
        out_shape=out_shape,
        scratch_shapes=[pltpu.VMEM((D_MODEL // LANES, tm, LANES), F32)],
        compiler_params=pltpu.CompilerParams(dimension_semantics=("arbitrary",) * 2,
                                             vmem_limit_bytes=VMEM_LIMIT),
        name="proj",
    )(x, g, w)


def _attn_kernel(*refs, qb, kb, q_steps, m_len, half_w, offsets, kv_shared, has_sink, want_lse):
    refs = list(refs)
    sink_ref = refs.pop(0) if has_sink else None
    bias_ref, q_ref, k_ref, v_ref, o_ref = refs[:5]
    lse_ref = refs[5] if want_lse else None
    lo_q = lax.broadcasted_iota(jnp.int32, (qb, PAIR_W), 1) < HEAD_DIM
    first_head = lax.broadcasted_iota(jnp.int32, (2 * qb, 1), 0) < qb
    zeros_q = jnp.zeros((qb, PAIR_W), BF16)

    def q_block(it, carry):
        cc, qi = it // q_steps, it % q_steps
        gi = pl.program_id(2) * q_steps + qi
        ks = pl.multiple_of(jnp.clip(gi * qb - half_w, 0, m_len - kb), 16)
        off = gi * qb - ks
        var = sum(jnp.where(off == o, n, 0) for n, o in enumerate(offsets))
        rows = pl.ds(pl.multiple_of(qi * qb, qb), qb)
        for j in range(N_PAIRS):
            cols = slice(j * PAIR_W, (j + 1) * PAIR_W)
            jc = (j // 2) if kv_shared else j
            kcols = slice(jc * PAIR_W, (jc + 1) * PAIR_W)
            qp = q_ref[cc, rows, cols]
            q_st = jnp.concatenate([jnp.where(lo_q, qp, zeros_q), jnp.where(lo_q, zeros_q, qp)], axis=0)
            s = lax.dot_general(q_st, k_ref[cc, pl.ds(ks, kb), kcols], (((1,), (1,)), ((), ())),
                                preferred_element_type=F32) + bias_ref[var, j]
            m = jnp.max(s, axis=-1, keepdims=True)
            if has_sink:
                sk = jnp.where(first_head, sink_ref[2 * j], sink_ref[2 * j + 1])
                m = jnp.maximum(m, sk)
            e = jnp.exp(s - m)
            den = jnp.sum(e, axis=-1, keepdims=True)
            if has_sink:
                den = den + jnp.exp(sk - m)
            o2 = jnp.dot(e.astype(BF16), v_ref[cc, pl.ds(ks, kb), kcols], preferred_element_type=F32)
            den2 = jnp.where(lo_q, den[:qb], den[qb:])
            o_ref[cc, rows, cols] = (jnp.where(lo_q, o2[:qb], o2[qb:]) / den2).astype(BF16)
            if want_lse:
                lse_ref[cc, rows, cols] = jnp.where(lo_q, m[:qb], m[qb:]) + jnp.log(den2)
        return carry

    n_iter = q_ref.shape[0] * q_steps
    lax.fori_loop(0, n_iter, q_block, 0, unroll=min(Q_UNROLL, n_iter))


def _bias_tables(offsets, slopes, qb, kb, half_w):
    rel = jnp.arange(qb, dtype=jnp.int32)[:, None] - jnp.arange(kb, dtype=jnp.int32)[None, :]
    dist = jnp.abs(rel[None] + jnp.asarray(offsets, jnp.int32)[:, None, None])
    bias = -jnp.asarray(slopes, F32)[None, :, None, None] * dist.astype(F32)[:, None]
    bias = jnp.where((dist <= half_w)[:, None], bias, NEG)
    return bias.reshape(len(offsets), N_PAIRS, 2 * qb, kb)


def _banded_attention(qkv, *, half_w, kv_width, slopes, sink=None, want_lse=True):
    batch, r, m_len, _ = qkv.shape
    qb = min(2 * half_w, m_len)
    kb = min(qb + 2 * half_w, m_len)
    nq = m_len // qb
    q_steps = min(Q_STEPS, nq)
    cb = min(r, Q_STEPS // q_steps)
    kv_shared = kv_width != GROUP_W
    has_sink = sink is not None
    k_block = GROUP_W // kv_width
    offsets = sorted({i * qb - min(max(i * qb - half_w, 0), m_len - kb) for i in range(nq)})
    bias = _bias_tables(offsets, slopes, qb, kb, half_w)

    in_specs = []
    args = []
    if has_sink:
        in_specs.append(pl.BlockSpec(memory_space=pltpu.SMEM))
        args.append(sink)
    in_specs += [
        _const_spec(bias.shape),
        pl.BlockSpec((None, cb, q_steps * qb, GROUP_W), lambda b, c, i: (b, c, i, 0)),
        pl.BlockSpec((None, cb, m_len, kv_width), lambda b, c, i: (b, c, 0, k_block)),
        pl.BlockSpec((None, cb, m_len, kv_width), lambda b, c, i: (b, c, 0, k_block + 1)),
    ]
    args += [bias, qkv, qkv, qkv]
    out_spec = pl.BlockSpec((None, cb, q_steps * qb, GROUP_W), lambda b, c, i: (b, c, i, 0))
    out_specs = [out_spec]
    out_shape = [jax.ShapeDtypeStruct((batch, r, m_len, GROUP_W), BF16)]
    if want_lse:
        out_specs.append(out_spec)
        out_shape.append(jax.ShapeDtypeStruct((batch, r, m_len, GROUP_W), F32))
    kern = functools.partial(_attn_kernel, qb=qb, kb=kb, q_steps=q_steps, m_len=m_len, half_w=half_w,
                             offsets=tuple(offsets), kv_shared=kv_shared, has_sink=has_sink,
                             want_lse=want_lse)
    return pl.pallas_call(
        kern,
        grid=(batch, r // cb, nq // q_steps),
        in_specs=in_specs,
        out_specs=out_specs,
        out_shape=out_shape,
        compiler_params=pltpu.CompilerParams(dimension_semantics=("arbitrary",) * 3,
                                             vmem_limit_bytes=VMEM_LIMIT),
        name=f"attn_r{r}_w{half_w}",
    )(*args)


def _memkv_kernel(m_ref, g_ref, w_ref, o_ref):
    h = _rms(m_ref[...], g_ref[...]).astype(BF16)
    o_ref[...] = jnp.dot(h, w_ref[...], preferred_element_type=F32).astype(BF16)


def _memkv(mem2d, g, w):
    R = mem2d.shape[0]
    return pl.pallas_call(
        _memkv_kernel,
        grid=(R // ROW_TILE,),
        in_specs=[pl.BlockSpec((ROW_TILE, D_MODEL), lambda i: (i, 0)),
                  _const_spec((1, D_MODEL)),
                  _const_spec((D_MODEL, 2 * X_WIDTH))],
        out_specs=pl.BlockSpec((ROW_TILE, 2 * X_WIDTH), lambda i: (i, 0)),
        out_shape=jax.ShapeDtypeStruct((R, 2 * X_WIDTH), BF16),
        compiler_params=pltpu.CompilerParams(dimension_semantics=("arbitrary",)),
        name="memkv",
    )(mem2d, g, w)


def _token_order(src_ref, dst_ref, m0, n):
    r = src_ref.shape[0]
    if r == 1:
        return src_ref[0, m0:m0 + n].astype(F32)
    n_slabs = dst_ref.shape[0]
    for c in range(r):
        rows = src_ref[c, m0:m0 + n].astype(F32)
        for s in range(n_slabs):
            dst_ref[s, pl.ds(m0 * r + c, n, stride=r), :] = rows[:, s * LANES:(s + 1) * LANES]
    return jnp.concatenate([dst_ref[s, m0 * r:(m0 + n) * r] for s in range(n_slabs)], axis=1)


def _mix_rows(t0, tm, x_ref, o_refs, l_refs, ob_ref, kv_ref,
              g_mix_ref, w_gate_ref, b_gate_ref, w_br_ref, w_out_ref,
              g_x_ref, w_cq_ref, w_co_ref, g_ffn_ref, w_rt_ref, b_rt_ref,
              x2_ref, hp_refs, route_ref, tri_ref, carry_ref, order_refs):
    rows = slice(t0, t0 + tm)
    x = x_ref[rows]
    h1 = _rms(x, g_mix_ref[...]).astype(BF16)
    yield

    by_class = lambda ref, scratch: _token_order(ref, scratch, t0 // ref.shape[0], tm // ref.shape[0])
    l0, l1, l2 = (by_class(l, s) for l, s in zip(l_refs, order_refs[:3]))
    lm = jnp.maximum(jnp.maximum(l0, l1), l2)
    e0, e1, e2 = jnp.exp(l0 - lm), jnp.exp(l1 - lm), jnp.exp(l2 - lm)
    den = e0 + e1 + e2
    o0, o1, o2 = (by_class(o, s) for o, s in zip(o_refs, order_refs[3:]))
    oa = (e0 / den) * o0 + (e1 / den) * o1 + (e2 / den) * o2
    yield
    br_a = jnp.dot(oa.astype(BF16), w_br_ref[:GROUP_W, :], preferred_element_type=F32)
    br_b = jnp.dot(ob_ref[rows], w_br_ref[GROUP_W:, :], preferred_element_type=F32)
    ga = jax.nn.sigmoid(jnp.dot(h1, w_gate_ref[:, :D_MODEL], preferred_element_type=F32)
                        + b_gate_ref[:, :D_MODEL])
    merged = ga * br_a
    gb = jax.nn.sigmoid(jnp.dot(h1, w_gate_ref[:, D_MODEL:], preferred_element_type=F32)
                        + b_gate_ref[:, D_MODEL:])
    merged = merged + gb * br_b
    yield
    x1 = x + jnp.dot(merged.astype(BF16), w_out_ref[...], preferred_element_type=F32)
    yield

    h2 = _rms(x1, g_x_ref[...]).astype(BF16)
    q = jnp.dot(h2, w_cq_ref[...], preferred_element_type=F32).astype(BF16)
    heads = []
    for h in range(X_HEADS):
        cols = slice(h * X_HEAD_DIM, (h + 1) * X_HEAD_DIM)
        kh = kv_ref[:, cols]
        vh = kv_ref[:, X_WIDTH + h * X_HEAD_DIM:X_WIDTH + (h + 1) * X_HEAD_DIM]
        s = lax.dot_general(q[:, cols], kh, (((1,), (1,)), ((), ())),
                            preferred_element_type=F32) * (X_HEAD_DIM ** -0.5)
        m = jnp.max(s, axis=-1, keepdims=True)
        e = jnp.exp(s - m)
        p = e / jnp.sum(e, axis=-1, keepdims=True)
        heads.append(jnp.dot(p.astype(BF16), vh, preferred_element_type=F32))
    o = jnp.concatenate(heads, axis=1).astype(BF16)
    yield
    x2 = x1 + jnp.dot(o, w_co_ref[...], preferred_element_type=F32)
    x2_ref[rows] = x2
    yield

    h3 = _rms(x2, g_ffn_ref[...])
    h3_hi = h3.astype(BF16)
    packed = lax.bitcast_convert_type(_pack_bf16_pairs(h3_hi.astype(F32)), jnp.int32)
    slab_w = packed.shape[1] // len(hp_refs)
    for n, hp_ref in enumerate(hp_refs):
        hp_ref[rows] = packed[:, n * slab_w:(n + 1) * slab_w]
    h3_lo = (h3 - h3_hi.astype(F32)).astype(BF16)
    by_hi = jnp.dot(h3_hi, w_rt_ref[...], preferred_element_type=F32)
    logits = (by_hi[:, :ROUTE_W] + by_hi[:, ROUTE_W:]
              + jnp.dot(h3_lo, w_rt_ref[:, :ROUTE_W], preferred_element_type=F32)) + b_rt_ref[...]
    lt = logits.T
    grow = lax.broadcasted_iota(jnp.int32, (GROUP_ROWS, tm), 0)
    gl = jnp.where(grow < N_GROUPS, lt[:GROUP_ROWS], -jnp.inf)
    gmax = jnp.max(gl, axis=0, keepdims=True)
    gidx = jnp.min(jnp.where(gl == gmax, grow, GROUP_ROWS), axis=0, keepdims=True)
    pg_sel = 1.0 / jnp.sum(jnp.exp(gl - gmax), axis=0, keepdims=True)
    erow = lax.broadcasted_iota(jnp.int32, (N_EXPERTS, tm), 0)
    in_group = (erow >= gidx * EXPERTS_PER_GROUP) & (erow < (gidx + 1) * EXPERTS_PER_GROUP)
    el = jnp.where(in_group, lt[EXPERT_LANE0:EXPERT_LANE0 + N_EXPERTS], -jnp.inf)
    emax1 = jnp.max(el, axis=0, keepdims=True)
    i1 = jnp.min(jnp.where(el == emax1, erow, N_EXPERTS), axis=0, keepdims=True)
    el2 = jnp.where(erow == i1, -jnp.inf, el)
    emax2 = jnp.max(el2, axis=0, keepdims=True)
    i2 = jnp.min(jnp.where(el2 == emax2, erow, N_EXPERTS), axis=0, keepdims=True)
    t2 = jnp.exp(emax2 - emax1)
    w1 = pg_sel / (1.0 + t2)
    w2 = pg_sel * t2 / (1.0 + t2)
    yield

    oh1 = erow == i1
    oh2 = erow == i2
    ohs = jnp.where(oh1 | oh2, 1.0, 0.0)
    before = lax.dot_general(ohs.astype(BF16), tri_ref[...], (((1,), (1,)), ((), ())),
                             preferred_element_type=F32) + carry_ref[...]
    rank1 = jnp.sum(jnp.where(oh1, before, 0.0), axis=0, keepdims=True)
    rank2 = jnp.sum(jnp.where(oh2, before, 0.0), axis=0, keepdims=True)
    carry_ref[...] = carry_ref[...] + jnp.sum(ohs, axis=1, keepdims=True)

    rec_t = jnp.concatenate([i1.astype(F32), i2.astype(F32), w1, w2, rank1, rank2,
                             jnp.zeros((ROUTE_W - 6, tm), F32)], axis=0)
    route_ref[rows] = rec_t.T


def _mix_kernel(x_ref, o0_ref, o1_ref, o2_ref, l0_ref, l1_ref, l2_ref, ob_ref, *refs):
    n_w = 12
    w_refs, refs = refs[:n_w], refs[n_w:]
    x2_ref, hp_refs, refs = refs[0], refs[1:1 + Y_SLABS], refs[1 + Y_SLABS:]
    route_ref, counts_ref, tri_ref, carry_ref = refs[:4]
    order_refs = refs[4:]
    sub = tri_ref.shape[0]

    @pl.when((pl.program_id(0) == 0) & (pl.program_id(1) == 0))
    def _():
        row = lax.broadcasted_iota(jnp.int32, (sub, sub), 0)
        col = lax.broadcasted_iota(jnp.int32, (sub, sub), 1)
        tri_ref[...] = jnp.where(row > col, 1.0, 0.0).astype(BF16)
        carry_ref[...] = jnp.zeros_like(carry_ref)

    chains = [_mix_rows(t0, sub, x_ref, (o0_ref, o1_ref, o2_ref), (l0_ref, l1_ref, l2_ref), ob_ref, *w_refs,
                        x2_ref, hp_refs, route_ref, tri_ref, carry_ref, order_refs)
              for t0 in range(0, x_ref.shape[0], sub)]
    finished = [False] * len(chains)
    rounds = 0
    while not all(finished):
        for n, chain in enumerate(chains):
            if MIX_SKEW * n <= rounds and not finished[n]:
                finished[n] = next(chain, "end") == "end"
        rounds += 1
    counts_ref[...] = jnp.broadcast_to(carry_ref[...], counts_ref.shape)


def _mix(x, o_groups, lse_groups, o_b, kvm, wts):
    batch, seq, _ = x.shape
    T = batch * seq
    tm = MIX_TILE
    steps = seq // tm
    mem_tokens = kvm.shape[0] // batch
    row = lambda w: pl.BlockSpec((tm, w), lambda b, i: (b * steps + i, 0))
    by_class = [pl.BlockSpec((None, r, tm // r, GROUP_W), lambda b, i: (b, 0, i, 0)) for _, r in A_CONFIGS]
    in_specs = ([pl.BlockSpec((None, tm, D_MODEL), lambda b, i: (b, i, 0))] + by_class + by_class
                + [pl.BlockSpec((None, None, tm, GROUP_W), lambda b, i: (b, 0, i, 0)),
                   pl.BlockSpec((mem_tokens, 2 * X_WIDTH), lambda b, i: (b, 0))]
                + [_const_spec(w.shape) for w in wts])
    return pl.pallas_call(
        _mix_kernel,
        grid=(batch, steps),
        in_specs=in_specs,
        out_specs=[row(D_MODEL)] + [row(D_MODEL // 2 // Y_SLABS)] * Y_SLABS
        + [row(ROUTE_W), pl.BlockSpec((N_EXPERTS, ROUTE_W), lambda b, i: (0, 0))],
        out_shape=[jax.ShapeDtypeStruct((T, D_MODEL), F32)]
        + [jax.ShapeDtypeStruct((T, D_MODEL // 2 // Y_SLABS), jnp.int32)] * Y_SLABS
        + [jax.ShapeDtypeStruct((T, ROUTE_W), F32),
           jax.ShapeDtypeStruct((N_EXPERTS, ROUTE_W), F32)],
        scratch_shapes=[pltpu.VMEM((MIX_SUB, MIX_SUB), BF16), pltpu.VMEM((N_EXPERTS, 1), F32)]
        + [pltpu.VMEM((GROUP_W // LANES, tm, LANES), F32)] * (2 * A_GROUPS),
        compiler_params=pltpu.CompilerParams(dimension_semantics=("arbitrary",) * 2,
                                             vmem_limit_bytes=VMEM_LIMIT),
        name="mix",
    )(x, *o_groups, *lse_groups, o_b, kvm, *wts)


def _pack_bf16_pairs(x):
    w = x.shape[1] // 2
    lo = lax.bitcast_convert_type(x[:, :w], jnp.uint32) >> 16
    hi = lax.bitcast_convert_type(x[:, w:], jnp.uint32) & jnp.uint32(0xFFFF0000)
    return lo | hi


def _unpack_bf16_pairs(u):
    lo = lax.bitcast_convert_type(u << 16, F32)
    hi = lax.bitcast_convert_type(u & jnp.uint32(0xFFFF0000), F32)
    return jnp.concatenate([lo, hi], axis=1)


def _scatter_rows(rows, dest, n_rows):
    T, width = rows.shape
    mesh = plsc.VectorSubcoreMesh(core_axis_name="core", subcore_axis_name="subcore")

    @pl.kernel(out_type=jax.ShapeDtypeStruct((n_rows, width), rows.dtype), mesh=mesh, scratch_types=[])
    def scatter_kernel(rows_hbm, idx0_hbm, idx1_hbm, out_hbm):
        def body(rows_vmem, idx0_vmem, idx1_vmem):
            pltpu.sync_copy(rows_vmem, out_hbm.at[idx0_vmem.at[0]])
            pltpu.sync_copy(rows_vmem, out_hbm.at[idx1_vmem.at[0]])

        per_core = T // GATHER_WINDOW // SC_CORES
        idx_spec = pl.BlockSpec((1, GATHER_WINDOW), index_map=lambda c, i: (0, c * per_core + i))
        pltpu.emit_pipeline(
            body,
            grid=(SC_CORES, per_core),
            in_specs=[pl.BlockSpec((GATHER_WINDOW, width), index_map=lambda c, i: (c * per_core + i, 0)),
                      idx_spec, idx_spec],
            out_specs=[],
            core_axis_name=("core", "subcore"),
            dimension_semantics=(pltpu.PARALLEL, pltpu.PARALLEL),
        )(rows_hbm, idx0_hbm, idx1_hbm)

    return scatter_kernel(rows, dest[0].reshape(1, T), dest[1].reshape(1, T))


def _expert_block(rows, n_valid, xs_refs, wgu_ref, wdn_ref, ys_refs):
    packed_in = jnp.concatenate([r[rows] for r in xs_refs], axis=1)
    row = lax.broadcasted_iota(jnp.int32, packed_in.shape, 0)
    packed_in = jnp.where(row < n_valid, packed_in, 0)
    xb = _unpack_bf16_pairs(lax.bitcast_convert_type(packed_in, jnp.uint32)).astype(BF16)
    gu = jnp.dot(xb, wgu_ref[...], preferred_element_type=F32)
    gate, up = gu[:, :EXPERT_FF], gu[:, EXPERT_FF:]
    act = (gate * jax.nn.sigmoid(gate) * up).astype(BF16)
    y = jnp.dot(act, wdn_ref[...], preferred_element_type=F32)
    packed = lax.bitcast_convert_type(_pack_bf16_pairs(y.astype(BF16).astype(F32)), jnp.int32)
    w = packed.shape[1] // len(ys_refs)
    for n, ys_ref in enumerate(ys_refs):
        ys_ref[rows] = packed[:, n * w:(n + 1) * w]


def _expert_kernel(be_ref, nv_ref, *refs):
    xs_refs, w_refs, ys_refs = refs[:Y_SLABS], refs[Y_SLABS:-Y_SLABS], refs[-Y_SLABS:]
    for j in range(BLOCKS_PER_STEP):
        _expert_block(slice(j * MOE_BLOCK, (j + 1) * MOE_BLOCK),
                      nv_ref[pl.program_id(0) * BLOCKS_PER_STEP + j],
                      xs_refs, w_refs[2 * j], w_refs[2 * j + 1], ys_refs)


def _experts(xs_slabs, block_e, block_valid, w_gu, w_down):
    P, slab_w = xs_slabs[0].shape
    n_blocks = P // MOE_BLOCK
    bps = BLOCKS_PER_STEP
    slab_spec = pl.BlockSpec((bps * MOE_BLOCK, slab_w), lambda b, be, nv: (b, 0))
    w_specs, w_args = [], []
    for j in range(bps):
        w_specs += [
            pl.BlockSpec((None, D_MODEL, 2 * EXPERT_FF), lambda b, be, nv, j=j: (be[b * bps + j], 0, 0)),
            pl.BlockSpec((None, EXPERT_FF, D_MODEL), lambda b, be, nv, j=j: (be[b * bps + j], 0, 0))]
        w_args += [w_gu, w_down]
    grid_spec = pltpu.PrefetchScalarGridSpec(
        num_scalar_prefetch=2,
        grid=(n_blocks // bps,),
        in_specs=[slab_spec] * Y_SLABS + w_specs,
        out_specs=[slab_spec] * Y_SLABS,
    )
    return pl.pallas_call(
        _expert_kernel,
        grid_spec=grid_spec,
        out_shape=[jax.ShapeDtypeStruct((P, slab_w), jnp.int32)] * Y_SLABS,
        compiler_params=pltpu.CompilerParams(dimension_semantics=("arbitrary",),
                                             vmem_limit_bytes=VMEM_LIMIT),
        name="experts",
    )(block_e, block_valid, *xs_slabs, *w_args)


def _gather_rows(table, indices):
    n, width = indices.shape[0], table.shape[1]
    mesh = plsc.VectorSubcoreMesh(core_axis_name="core", subcore_axis_name="subcore")

    @pl.kernel(out_type=jax.ShapeDtypeStruct((n, width), table.dtype), mesh=mesh)
    def gather_kernel(table_hbm, idx_hbm, out_hbm):
        def body(idx_vmem, out_vmem):
            pltpu.sync_copy(table_hbm.at[idx_vmem.at[0]], out_vmem)

        per_core = n // GATHER_WINDOW // SC_CORES
        pltpu.emit_pipeline(
            body,
            grid=(SC_CORES, per_core),
            in_specs=[pl.BlockSpec((1, GATHER_WINDOW), index_map=lambda c, i: (0, c * per_core + i))],
            out_specs=[pl.BlockSpec((GATHER_WINDOW, width), index_map=lambda c, i: (c * per_core + i, 0))],
            core_axis_name=("core", "subcore"),
            dimension_semantics=(pltpu.PARALLEL, pltpu.PARALLEL),
        )(idx_hbm, out_hbm)

    return gather_kernel(table, indices.reshape(1, n))


def _combine_kernel(x2_ref, route_ref, g_ref, *refs):
    y_refs, out_ref = refs[:-1], refs[-1]
    route = route_ref[...]

    def expert_rows(slabs):
        packed = jnp.concatenate([s[...] for s in slabs], axis=1)
        return _unpack_bf16_pairs(lax.bitcast_convert_type(packed, jnp.uint32))

    moe = expert_rows(y_refs[:Y_SLABS]) * route[:, 2:3] + expert_rows(y_refs[Y_SLABS:]) * route[:, 3:4]
    out_ref[...] = _rms(x2_ref[...] + moe, g_ref[...])


def _combine(x2, route, g_final, dest, ys_slabs):
    T = x2.shape[0]
    tm = MIX_TILE
    n_steps = T // tm
    rows = [_gather_rows(ys, dest.reshape(-1)) for ys in ys_slabs]
    slab_w = rows[0].shape[1]
    first = [pl.BlockSpec((tm, slab_w), lambda i: (i, 0))] * Y_SLABS
    second = [pl.BlockSpec((tm, slab_w), lambda i: (i + n_steps, 0))] * Y_SLABS
    return pl.pallas_call(
        _combine_kernel,
        grid=(n_steps,),
        in_specs=[pl.BlockSpec((tm, D_MODEL), lambda i: (i, 0)),
                  pl.BlockSpec((tm, ROUTE_W), lambda i: (i, 0)),
                  _const_spec((1, D_MODEL))] + first + second,
        out_specs=pl.BlockSpec((tm, D_MODEL), lambda i: (i, 0)),
        out_shape=jax.ShapeDtypeStruct((T, D_MODEL), F32),
        compiler_params=pltpu.CompilerParams(dimension_semantics=("arbitrary",)),
        name="combine",
    )(x2, route, g_final, *rows, *rows)


def _routing_tables(route, counts_rec, n_tokens):
    n_slots = 2 * n_tokens
    n_blocks = n_slots // MOE_BLOCK + N_EXPERTS
    n_rows = n_blocks * MOE_BLOCK
    counts = counts_rec[:, 0].astype(jnp.int32)
    padded = (counts + MOE_BLOCK - 1) // MOE_BLOCK * MOE_BLOCK
    pends = jnp.cumsum(padded)
    pstarts = pends - padded
    expert = route[:, 0:2].astype(jnp.int32)
    rank = route[:, 4:6].astype(jnp.int32)
    lookup = lambda table, idx: jnp.sum(jnp.where(idx[..., None] == jnp.arange(N_EXPERTS), table, 0), axis=-1)
    count_le = lambda sorted_vals, q: jnp.sum(sorted_vals[None, :] <= q[:, None], axis=-1).astype(jnp.int32)
    dest = (lookup(pstarts, expert) + rank).T
    block_row0 = jnp.arange(n_blocks, dtype=jnp.int32) * MOE_BLOCK
    owner = count_le(pends, block_row0)
    block_e = jnp.minimum(owner, N_EXPERTS - 1)
    block_valid = jnp.clip(lookup(pstarts + counts, owner) - block_row0, 0, MOE_BLOCK)
    block_valid = jnp.where(owner < N_EXPERTS, block_valid, 0).astype(jnp.int32)
    return dest.astype(jnp.int32), block_e, block_valid, n_rows


def _encoder_group(x, mem, w):
    batch, seq, _ = x.shape
    T = batch * seq
    *qkv_groups, qkv_win = _proj(x, w["g_mix"], w["w_in"])

    slopes_a = _alibi_slopes(A_HEADS).reshape(A_GROUPS, A_HEADS_PER_GROUP)
    o_groups, lse_groups = [], []
    for g, (window, r) in enumerate(A_CONFIGS):
        o, lse = _banded_attention(qkv_groups[g], half_w=window // (2 * r), kv_width=GROUP_W,
                                   slopes=slopes_a[g] * np.float32(r))
        o_groups.append(o)
        lse_groups.append(lse)
    (o_b,) = _banded_attention(qkv_win.reshape(batch, 1, seq, WIN_W), half_w=B_HALF_WINDOW,
                               kv_width=KV2_W, slopes=_alibi_slopes(B_HEADS), sink=w["sink_b"],
                               want_lse=False)

    kvm = _memkv(mem.reshape(-1, D_MODEL), w["g_mem"], w["w_ckv"])
    mix_w = [w[k] for k in ("g_mix", "w_gate", "b_gate", "w_branch", "w_out", "g_xattn", "w_cq",
                            "w_co", "g_ffn", "w_route", "b_route")]
    x2, *h_slabs, route, counts_rec = _mix(x, o_groups, lse_groups, o_b, kvm, mix_w)

    dest, block_e, block_valid, n_rows = _routing_tables(route, counts_rec, T)
    xs_slabs = [_scatter_rows(h, dest, n_rows) for h in h_slabs]
    ys_slabs = _experts(xs_slabs, block_e, block_valid, w["w_gu"], w["w_down"])
    y = _combine(x2, route, w["g_final"], dest, ys_slabs)
    return y.reshape(batch, seq, D_MODEL)


def _prep_weights(g_mix, w_in, sink_b, w_gate, b_gate, w_branch, w_out, g_xattn, g_mem, w_cq, w_ckv,
                  w_co, g_ffn, w_rg, b_rg, w_re, b_re, w_gu, w_down, g_final):
    scale = HEAD_DIM ** -0.5
    aw = A_WIDTH
    qa, ka, va = w_in[:, :aw] * scale, w_in[:, aw:2 * aw], w_in[:, 2 * aw:3 * aw]
    qb = w_in[:, 3 * aw:3 * aw + B_Q] * scale
    kb = w_in[:, 3 * aw + B_Q:3 * aw + B_Q + B_KV]
    vb = w_in[:, 3 * aw + B_Q + B_KV:]
    twice = lambda t: jnp.repeat(t.reshape(D_MODEL, B_KV_HEADS, 1, HEAD_DIM), 2, axis=2).reshape(D_MODEL, KV2_W)
    group = lambda t, g: t[:, g * GROUP_W:(g + 1) * GROUP_W]
    cols = [group(t, g) for g in range(A_GROUPS) for t in (qa, ka, va)] + [qb, twice(kb), twice(vb)]
    w_in_x = jnp.concatenate(cols, axis=1).astype(BF16)
    w_route = jnp.zeros((D_MODEL, ROUTE_W), F32)
    w_route = w_route.at[:, :N_GROUPS].set(w_rg).at[:, EXPERT_LANE0:EXPERT_LANE0 + N_EXPERTS].set(w_re)
    b_route = jnp.zeros((1, ROUTE_W), F32)
    b_route = b_route.at[0, :N_GROUPS].set(b_rg).at[0, EXPERT_LANE0:EXPERT_LANE0 + N_EXPERTS].set(b_re)
    w_route_hi = w_route.astype(BF16)
    w_route = jnp.concatenate([w_route_hi, (w_route - w_route_hi.astype(F32)).astype(BF16)], axis=1)
    vec = lambda v: v.reshape(1, -1).astype(F32)
    return dict(
        g_mix=vec(g_mix), w_in=w_in_x, sink_b=sink_b.astype(F32),
        w_gate=w_gate.astype(BF16), b_gate=vec(b_gate), w_branch=w_branch.astype(BF16),
        w_out=w_out.astype(BF16), g_xattn=vec(g_xattn), g_mem=vec(g_mem), w_cq=w_cq.astype(BF16),
        w_ckv=w_ckv.astype(BF16), w_co=w_co.astype(BF16), g_ffn=vec(g_ffn),
        w_route=w_route, b_route=b_route, w_gu=w_gu.astype(BF16), w_down=w_down.astype(BF16),
        g_final=vec(g_final))


def kernel(x_prompt, x_sample, mem_prompt, mem_sample, g_mix, w_in, sink_b, w_gate, b_gate, w_branch,
           w_out, g_xattn, g_mem, w_cq, w_ckv, w_co, g_ffn, w_rg, b_rg, w_re, b_re, w_gu, w_down,
           g_final):
    assert g_mix.shape[0] == 1, "single-layer encoder"
    w = _prep_weights(g_mix[0], w_in[0], sink_b[0], w_gate[0], b_gate[0], w_branch[0], w_out[0],
                      g_xattn[0], g_mem[0], w_cq[0], w_ckv[0], w_co[0], g_ffn[0], w_rg[0], b_rg[0],
                      w_re[0], b_re[0], w_gu[0], w_down[0], g_final)
    y_sample = _encoder_group(x_sample, mem_sample, w)
    y_prompt = _encoder_group(x_prompt, mem_prompt, w)
    return (y_prompt, y_sample)
```

```python
import functools

import numpy as np
import jax
import jax.numpy as jnp
from jax import lax
from jax.experimental import pallas as pl
from jax.experimental.pallas import tpu as pltpu
from jax.experimental.pallas import tpu_sc as plsc

F32 = jnp.float32
BF16 = jnp.bfloat16

D_MODEL = 1024
HEAD_DIM = 64
A_CONFIGS = ((128, 1), (512, 4), (2048, 16))
A_GROUPS = 3
A_HEADS_PER_GROUP = 8
A_HEADS = A_GROUPS * A_HEADS_PER_GROUP
A_WIDTH = A_HEADS * HEAD_DIM
B_HEADS = 8
B_KV_HEADS = 2
B_HALF_WINDOW = 128
B_Q = B_HEADS * HEAD_DIM
B_KV = B_KV_HEADS * HEAD_DIM
X_HEADS = 4
X_HEAD_DIM = 128
X_WIDTH = X_HEADS * X_HEAD_DIM
N_GROUPS = 4
EXPERTS_PER_GROUP = 8
N_EXPERTS = N_GROUPS * EXPERTS_PER_GROUP
EXPERT_FF = 512
MOE_BLOCK = 256
EPS = 1e-6
NEG = -1e30

LANES = 128
GROUP_W = A_HEADS_PER_GROUP * HEAD_DIM
PAIR_W = 2 * HEAD_DIM
N_PAIRS = GROUP_W // PAIR_W
QKV_W = 3 * GROUP_W
KV2_W = 2 * B_KV
WIN_W = B_Q + 2 * KV2_W
PROJ_W = A_GROUPS * QKV_W + WIN_W
ROUTE_W = LANES
EXPERT_LANE0 = 32
GROUP_ROWS = 8
VMEM_LIMIT = 52 * 1024 * 1024

ROW_TILE = 256
PROJ_TILE = 512
MIX_TILE = 512
MIX_SKEW = 2
MIX_SUB = 256
Q_STEPS = 8
Q_UNROLL = 4
GATHER_WINDOW = 128
SC_CORES = 2
BLOCKS_PER_STEP = 4
Y_SLABS = 2
def _rms(x, g):
    ms = jnp.mean(x * x, axis=-1, keepdims=True)
    return x * lax.rsqrt(ms + EPS) * g


def _alibi_slopes(n):
    return 2.0 ** (-8.0 * np.arange(1, n + 1, dtype=np.float32) / n)


def _const_spec(shape):
    return pl.BlockSpec(shape, lambda *_: (0,) * len(shape), pipeline_mode=pl.Buffered(1))


def _proj_kernel(x_ref, g_ref, w_ref, *refs):
    o_refs, win_ref, h_ref = refs[:A_GROUPS], refs[A_GROUPS], refs[A_GROUPS + 1]
    tm = x_ref.shape[0]
    h32 = _rms(x_ref[...], g_ref[...])
    h_nat = h32.astype(BF16)
    n_slabs = h_ref.shape[0]
    for s in range(n_slabs):
        h_ref[s] = h32[:, s * LANES:(s + 1) * LANES]
    for g, (_, r) in enumerate(A_CONFIGS):
        n = tm // r
        if r == 1:
            h = h_nat
        else:
            h = jnp.concatenate(
                [jnp.concatenate([h_ref[s, pl.ds(c, n, stride=r), :] for c in range(r)], axis=0)
                 for s in range(n_slabs)], axis=1).astype(BF16)
        for j in range(3):
            cols = slice(g * QKV_W + j * GROUP_W, g * QKV_W + (j + 1) * GROUP_W)
            res = jnp.dot(h, w_ref[:, cols], preferred_element_type=F32).astype(BF16)
            for c in range(r):
                o_refs[g][c, :, j * GROUP_W:(j + 1) * GROUP_W] = res[c * n:(c + 1) * n]
    for j in range(WIN_W // GROUP_W):
        cols = slice(A_GROUPS * QKV_W + j * GROUP_W, A_GROUPS * QKV_W + (j + 1) * GROUP_W)
        win_ref[:, j * GROUP_W:(j + 1) * GROUP_W] = jnp.dot(
            h_nat, w_ref[:, cols], preferred_element_type=F32).astype(BF16)


def _proj(x, g, w):
    batch, seq, _ = x.shape
    tm = PROJ_TILE
    out_specs = [pl.BlockSpec((None, r, tm // r, QKV_W), lambda b, i: (b, 0, i, 0)) for _, r in A_CONFIGS]
    out_shape = [jax.ShapeDtypeStruct((batch, r, seq // r, QKV_W), BF16) for _, r in A_CONFIGS]
    out_specs.append(pl.BlockSpec((None, tm, WIN_W), lambda b, i: (b, i, 0)))
    out_shape.append(jax.ShapeDtypeStruct((batch, seq, WIN_W), BF16))
    return pl.pallas_call(
        _proj_kernel,
        grid=(batch, seq // tm),
        in_specs=[pl.BlockSpec((None, tm, D_MODEL), lambda b, i: (b, i, 0)),
                  _const_spec((1, D_MODEL)),
                  _const_spec((D_MODEL, PROJ_W))],
        out_specs=out_specs,
        out_shape=out_shape,
        scratch_shapes=[pltpu.VMEM((D_MODEL // LANES, tm, LANES), F32)],
        compiler_params=pltpu.CompilerParams(dimension_semantics=("arbitrary",) * 2,
                                             vmem_limit_bytes=VMEM_LIMIT),
        name="proj",
    )(x, g, w)


def _attn_kernel(*refs, qb, kb, q_steps, m_len, half_w, offsets, kv_shared, has_sink, want_lse):
    refs = list(refs)
    sink_ref = refs.pop(0) if has_sink else None
    bias_ref, q_ref, k_ref, v_ref, o_ref = refs[:5]
    lse_ref = refs[5] if want_lse else None
    lo_q = lax.broadcasted_iota(jnp.int32, (qb, PAIR_W), 1) < HEAD_DIM
    first_head = lax.broadcasted_iota(jnp.int32, (2 * qb, 1), 0) < qb
    zeros_q = jnp.zeros((qb, PAIR_W), BF16)

    def q_block(it, carry):
        cc, qi = it // q_steps, it % q_steps
        gi = pl.program_id(2) * q_steps + qi
        ks = pl.multiple_of(jnp.clip(gi * qb - half_w, 0, m_len - kb), 16)
        off = gi * qb - ks
        var = sum(jnp.where(off == o, n, 0) for n, o in enumerate(offsets))
        rows = pl.ds(pl.multiple_of(qi * qb, qb), qb)
        for j in range(N_PAIRS):
            cols = slice(j * PAIR_W, (j + 1) * PAIR_W)
            jc = (j // 2) if kv_shared else j
            kcols = slice(jc * PAIR_W, (jc + 1) * PAIR_W)
            qp = q_ref[cc, rows, cols]
            q_st = jnp.concatenate([jnp.where(lo_q, qp, zeros_q), jnp.where(lo_q, zeros_q, qp)], axis=0)
            s = lax.dot_general(q_st, k_ref[cc, pl.ds(ks, kb), kcols], (((1,), (1,)), ((), ())),
                                preferred_element_type=F32) + bias_ref[var, j]
            m = jnp.max(s, axis=-1, keepdims=True)
            if has_sink:
                sk = jnp.where(first_head, sink_ref[2 * j], sink_ref[2 * j + 1])
                m = jnp.maximum(m, sk)
            e = jnp.exp(s - m)
            den = jnp.sum(e, axis=-1, keepdims=True)
            if has_sink:
                den = den + jnp.exp(sk - m)
            o2 = jnp.dot(e.astype(BF16), v_ref[cc, pl.ds(ks, kb), kcols], preferred_element_type=F32)
            den2 = jnp.where(lo_q, den[:qb], den[qb:])
            o_ref[cc, rows, cols] = (jnp.where(lo_q, o2[:qb], o2[qb:]) / den2).astype(BF16)
            if want_lse:
                lse_ref[cc, rows, cols] = jnp.where(lo_q, m[:qb], m[qb:]) + jnp.log(den2)
        return carry

    n_iter = q_ref.shape[0] * q_steps
    lax.fori_loop(0, n_iter, q_block, 0, unroll=min(Q_UNROLL, n_iter))


def _bias_tables(offsets, slopes, qb, kb, half_w):
    rel = jnp.arange(qb, dtype=jnp.int32)[:, None] - jnp.arange(kb, dtype=jnp.int32)[None, :]
    dist = jnp.abs(rel[None] + jnp.asarray(offsets, jnp.int32)[:, None, None])
    bias = -jnp.asarray(slopes, F32)[None, :, None, None] * dist.astype(F32)[:, None]
    bias = jnp.where((dist <= half_w)[:, None], bias, NEG)
    return bias.reshape(len(offsets), N_PAIRS, 2 * qb, kb)


def _banded_attention(qkv, *, half_w, kv_width, slopes, sink=None, want_lse=True):
    batch, r, m_len, _ = qkv.shape
    qb = min(2 * half_w, m_len)
    kb = min(qb + 2 * half_w, m_len)
    nq = m_len // qb
    q_steps = min(Q_STEPS, nq)
    cb = min(r, Q_STEPS // q_steps)
    kv_shared = kv_width != GROUP_W
    has_sink = sink is not None
    k_block = GROUP_W // kv_width
    offsets = sorted({i * qb - min(max(i * qb - half_w, 0), m_len - kb) for i in range(nq)})
    bias = _bias_tables(offsets, slopes, qb, kb, half_w)

    in_specs = []
    args = []
    if has_sink:
        in_specs.append(pl.BlockSpec(memory_space=pltpu.SMEM))
        args.append(sink)
    in_specs += [
        _const_spec(bias.shape),
        pl.BlockSpec((None, cb, q_steps * qb, GROUP_W), lambda b, c, i: (b, c, i, 0)),
        pl.BlockSpec((None, cb, m_len, kv_width), lambda b, c, i: (b, c, 0, k_block)),
        pl.BlockSpec((None, cb, m_len, kv_width), lambda b, c, i: (b, c, 0, k_block + 1)),
    ]
    args += [bias, qkv, qkv, qkv]
    out_spec = pl.BlockSpec((None, cb, q_steps * qb, GROUP_W), lambda b, c, i: (b, c, i, 0))
    out_specs = [out_spec]
    out_shape = [jax.ShapeDtypeStruct((batch, r, m_len, GROUP_W), BF16)]
    if want_lse:
        out_specs.append(out_spec)
        out_shape.append(jax.ShapeDtypeStruct((batch, r, m_len, GROUP_W), F32))
    kern = functools.partial(_attn_kernel, qb=qb, kb=kb, q_steps=q_steps, m_len=m_len, half_w=half_w,
                             offsets=tuple(offsets), kv_shared=kv_shared, has_sink=has_sink,
                             want_lse=want_lse)
    return pl.pallas_call(
        kern,
        grid=(batch, r // cb, nq // q_steps),
        in_specs=in_specs,
        out_specs=out_specs,
        out_shape=out_shape,
        compiler_params=pltpu.CompilerParams(dimension_semantics=("arbitrary",) * 3,
                                             vmem_limit_bytes=VMEM_LIMIT),
        name=f"attn_r{r}_w{half_w}",
    )(*args)


def _memkv_kernel(m_ref, g_ref, w_ref, o_ref):
    h = _rms(m_ref[...], g_ref[...]).astype(BF16)
    o_ref[...] = jnp.dot(h, w_ref[...], preferred_element_type=F32).astype(BF16)


def _memkv(mem2d, g, w):
    R = mem2d.shape[0]
    return pl.pallas_call(
        _memkv_kernel,
        grid=(R // ROW_TILE,),
        in_specs=[pl.BlockSpec((ROW_TILE, D_MODEL), lambda i: (i, 0)),
                  _const_spec((1, D_MODEL)),
                  _const_spec((D_MODEL, 2 * X_WIDTH))],
        out_specs=pl.BlockSpec((ROW_TILE, 2 * X_WIDTH), lambda i: (i, 0)),
        out_shape=jax.ShapeDtypeStruct((R, 2 * X_WIDTH), BF16),
        compiler_params=pltpu.CompilerParams(dimension_semantics=("arbitrary",)),
        name="memkv",
    )(mem2d, g, w)


def _token_order(src_ref, dst_ref, m0, n):
    r = src_ref.shape[0]
    if r == 1:
        return src_ref[0, m0:m0 + n].astype(F32)
    n_slabs = dst_ref.shape[0]
    for c in range(r):
        rows = src_ref[c, m0:m0 + n].astype(F32)
        for s in range(n_slabs):
            dst_ref[s, pl.ds(m0 * r + c, n, stride=r), :] = rows[:, s * LANES:(s + 1) * LANES]
    return jnp.concatenate([dst_ref[s, m0 * r:(m0 + n) * r] for s in range(n_slabs)], axis=1)


def _mix_rows(t0, tm, x_ref, o_refs, l_refs, ob_ref, kv_ref,
              g_mix_ref, w_gate_ref, b_gate_ref, w_br_ref, w_out_ref,
              g_x_ref, w_cq_ref, w_co_ref, g_ffn_ref, w_rt_ref, b_rt_ref,
              x2_ref, hp_refs, route_ref, tri_ref, carry_ref, order_refs):
    rows = slice(t0, t0 + tm)
    x = x_ref[rows]
    h1 = _rms(x, g_mix_ref[...]).astype(BF16)
    yield

    by_class = lambda ref, scratch: _token_order(ref, scratch, t0 // ref.shape[0], tm // ref.shape[0])
    l0, l1, l2 = (by_class(l, s) for l, s in zip(l_refs, order_refs[:3]))
    lm = jnp.maximum(jnp.maximum(l0, l1), l2)
    e0, e1, e2 = jnp.exp(l0 - lm), jnp.exp(l1 - lm), jnp.exp(l2 - lm)
    den = e0 + e1 + e2
    o0, o1, o2 = (by_class(o, s) for o, s in zip(o_refs, order_refs[3:]))
    oa = (e0 / den) * o0 + (e1 / den) * o1 + (e2 / den) * o2
    yield
    br_a = jnp.dot(oa.astype(BF16), w_br_ref[:GROUP_W, :], preferred_element_type=F32)
    br_b = jnp.dot(ob_ref[rows], w_br_ref[GROUP_W:, :], preferred_element_type=F32)
    ga = jax.nn.sigmoid(jnp.dot(h1, w_gate_ref[:, :D_MODEL], preferred_element_type=F32)
                        + b_gate_ref[:, :D_MODEL])
    merged = ga * br_a
    gb = jax.nn.sigmoid(jnp.dot(h1, w_gate_ref[:, D_MODEL:], preferred_element_type=F32)
                        + b_gate_ref[:, D_MODEL:])
    merged = merged + gb * br_b
    yield
    x1 = x + jnp.dot(merged.astype(BF16), w_out_ref[...], preferred_element_type=F32)
    yield

    h2 = _rms(x1, g_x_ref[...]).astype(BF16)
    q = jnp.dot(h2, w_cq_ref[...], preferred_element_type=F32).astype(BF16)
    heads = []
    for h in range(X_HEADS):
        cols = slice(h * X_HEAD_DIM, (h + 1) * X_HEAD_DIM)
        kh = kv_ref[:, cols]
        vh = kv_ref[:, X_WIDTH + h * X_HEAD_DIM:X_WIDTH + (h + 1) * X_HEAD_DIM]
        s = lax.dot_general(q[:, cols], kh, (((1,), (1,)), ((), ())),
                            preferred_element_type=F32) * (X_HEAD_DIM ** -0.5)
        m = jnp.max(s, axis=-1, keepdims=True)
        e = jnp.exp(s - m)
        p = e / jnp.sum(e, axis=-1, keepdims=True)
        heads.append(jnp.dot(p.astype(BF16), vh, preferred_element_type=F32))
    o = jnp.concatenate(heads, axis=1).astype(BF16)
    yield
    x2 = x1 + jnp.dot(o, w_co_ref[...], preferred_element_type=F32)
    x2_ref[rows] = x2
    yield

    h3 = _rms(x2, g_ffn_ref[...])
    h3_hi = h3.astype(BF16)
    packed = lax.bitcast_convert_type(_pack_bf16_pairs(h3_hi.astype(F32)), jnp.int32)
    slab_w = packed.shape[1] // len(hp_refs)
    for n, hp_ref in enumerate(hp_refs):
        hp_ref[rows] = packed[:, n * slab_w:(n + 1) * slab_w]
    h3_lo = (h3 - h3_hi.astype(F32)).astype(BF16)
    by_hi = jnp.dot(h3_hi, w_rt_ref[...], preferred_element_type=F32)
    logits = (by_hi[:, :ROUTE_W] + by_hi[:, ROUTE_W:]
              + jnp.dot(h3_lo, w_rt_ref[:, :ROUTE_W], preferred_element_type=F32)) + b_rt_ref[...]
    lt = logits.T
    grow = lax.broadcasted_iota(jnp.int32, (GROUP_ROWS, tm), 0)
    gl = jnp.where(grow < N_GROUPS, lt[:GROUP_ROWS], -jnp.inf)
    gmax = jnp.max(gl, axis=0, keepdims=True)
    gidx = jnp.min(jnp.where(gl == gmax, grow, GROUP_ROWS), axis=0, keepdims=True)
    pg_sel = 1.0 / jnp.sum(jnp.exp(gl - gmax), axis=0, keepdims=True)
    erow = lax.broadcasted_iota(jnp.int32, (N_EXPERTS, tm), 0)
    in_group = (erow >= gidx * EXPERTS_PER_GROUP) & (erow < (gidx + 1) * EXPERTS_PER_GROUP)
    el = jnp.where(in_group, lt[EXPERT_LANE0:EXPERT_LANE0 + N_EXPERTS], -jnp.inf)
    emax1 = jnp.max(el, axis=0, keepdims=True)
    i1 = jnp.min(jnp.where(el == emax1, erow, N_EXPERTS), axis=0, keepdims=True)
    el2 = jnp.where(erow == i1, -jnp.inf, el)
    emax2 = jnp.max(el2, axis=0, keepdims=True)
    i2 = jnp.min(jnp.where(el2 == emax2, erow, N_EXPERTS), axis=0, keepdims=True)
    t2 = jnp.exp(emax2 - emax1)
    w1 = pg_sel / (1.0 + t2)
    w2 = pg_sel * t2 / (1.0 + t2)
    yield

    oh1 = erow == i1
    oh2 = erow == i2
    ohs = jnp.where(oh1 | oh2, 1.0, 0.0)
    before = lax.dot_general(ohs.astype(BF16), tri_ref[...], (((1,), (1,)), ((), ())),
                             preferred_element_type=F32) + carry_ref[...]
    rank1 = jnp.sum(jnp.where(oh1, before, 0.0), axis=0, keepdims=True)
    rank2 = jnp.sum(jnp.where(oh2, before, 0.0), axis=0, keepdims=True)
    carry_ref[...] = carry_ref[...] + jnp.sum(ohs, axis=1, keepdims=True)

    rec_t = jnp.concatenate([i1.astype(F32), i2.astype(F32), w1, w2, rank1, rank2,
                             jnp.zeros((ROUTE_W - 6, tm), F32)], axis=0)
    route_ref[rows] = rec_t.T


def _mix_kernel(x_ref, o0_ref, o1_ref, o2_ref, l0_ref, l1_ref, l2_ref, ob_ref, *refs):
    n_w = 12
    w_refs, refs = refs[:n_w], refs[n_w:]
    x2_ref, hp_refs, refs = refs[0], refs[1:1 + Y_SLABS], refs[1 + Y_SLABS:]
    route_ref, counts_ref, tri_ref, carry_ref = refs[:4]
    order_refs = refs[4:]
    sub = tri_ref.shape[0]

    @pl.when((pl.program_id(0) == 0) & (pl.program_id(1) == 0))
    def _():
        row = lax.broadcasted_iota(jnp.int32, (sub, sub), 0)
        col = lax.broadcasted_iota(jnp.int32, (sub, sub), 1)
        tri_ref[...] = jnp.where(row > col, 1.0, 0.0).astype(BF16)
        carry_ref[...] = jnp.zeros_like(carry_ref)

    chains = [_mix_rows(t0, sub, x_ref, (o0_ref, o1_ref, o2_ref), (l0_ref, l1_ref, l2_ref), ob_ref, *w_refs,
                        x2_ref, hp_refs, route_ref, tri_ref, carry_ref, order_refs)
              for t0 in range(0, x_ref.shape[0], sub)]
    finished = [False] * len(chains)
    rounds = 0
    while not all(finished):
        for n, chain in enumerate(chains):
            if MIX_SKEW * n <= rounds and not finished[n]:
                finished[n] = next(chain, "end") == "end"
        rounds += 1
    counts_ref[...] = jnp.broadcast_to(carry_ref[...], counts_ref.shape)


def _mix(x, o_groups, lse_groups, o_b, kvm, wts):
    batch, seq, _ = x.shape
    T = batch * seq
    tm = MIX_TILE
    steps = seq // tm
    mem_tokens = kvm.shape[0] // batch
    row = lambda w: pl.BlockSpec((tm, w), lambda b, i: (b * steps + i, 0))
    by_class = [pl.BlockSpec((None, r, tm // r, GROUP_W), lambda b, i: (b, 0, i, 0)) for _, r in A_CONFIGS]
    in_specs = ([pl.BlockSpec((None, tm, D_MODEL), lambda b, i: (b, i, 0))] + by_class + by_class
                + [pl.BlockSpec((None, None, tm, GROUP_W), lambda b, i: (b, 0, i, 0)),
                   pl.BlockSpec((mem_tokens, 2 * X_WIDTH), lambda b, i: (b, 0))]
                + [_const_spec(w.shape) for w in wts])
    return pl.pallas_call(
        _mix_kernel,
        grid=(batch, steps),
        in_specs=in_specs,
        out_specs=[row(D_MODEL)] + [row(D_MODEL // 2 // Y_SLABS)] * Y_SLABS
        + [row(ROUTE_W), pl.BlockSpec((N_EXPERTS, ROUTE_W), lambda b, i: (0, 0))],
        out_shape=[jax.ShapeDtypeStruct((T, D_MODEL), F32)]
        + [jax.ShapeDtypeStruct((T, D_MODEL // 2 // Y_SLABS), jnp.int32)] * Y_SLABS
        + [jax.ShapeDtypeStruct((T, ROUTE_W), F32),
           jax.ShapeDtypeStruct((N_EXPERTS, ROUTE_W), F32)],
        scratch_shapes=[pltpu.VMEM((MIX_SUB, MIX_SUB), BF16), pltpu.VMEM((N_EXPERTS, 1), F32)]
        + [pltpu.VMEM((GROUP_W // LANES, tm, LANES), F32)] * (2 * A_GROUPS),
        compiler_params=pltpu.CompilerParams(dimension_semantics=("arbitrary",) * 2,
                                             vmem_limit_bytes=VMEM_LIMIT),
        name="mix",
    )(x, *o_groups, *lse_groups, o_b, kvm, *wts)


def _pack_bf16_pairs(x):
    w = x.shape[1] // 2
    lo = lax.bitcast_convert_type(x[:, :w], jnp.uint32) >> 16
    hi = lax.bitcast_convert_type(x[:, w:], jnp.uint32) & jnp.uint32(0xFFFF0000)
    return lo | hi


def _unpack_bf16_pairs(u):
    lo = lax.bitcast_convert_type(u << 16, F32)
    hi = lax.bitcast_convert_type(u & jnp.uint32(0xFFFF0000), F32)
    return jnp.concatenate([lo, hi], axis=1)


def _scatter_rows(rows, dest, n_rows):
    T, width = rows.shape
    mesh = plsc.VectorSubcoreMesh(core_axis_name="core", subcore_axis_name="subcore")

    @pl.kernel(out_type=jax.ShapeDtypeStruct((n_rows, width), rows.dtype), mesh=mesh, scratch_types=[])
    def scatter_kernel(rows_hbm, idx0_hbm, idx1_hbm, out_hbm):
        def body(rows_vmem, idx0_vmem, idx1_vmem):
            pltpu.sync_copy(rows_vmem, out_hbm.at[idx0_vmem.at[0]])
            pltpu.sync_copy(rows_vmem, out_hbm.at[idx1_vmem.at[0]])

        per_core = T // GATHER_WINDOW // SC_CORES
        idx_spec = pl.BlockSpec((1, GATHER_WINDOW), index_map=lambda c, i: (0, c * per_core + i))
        pltpu.emit_pipeline(
            body,
            grid=(SC_CORES, per_core),
            in_specs=[pl.BlockSpec((GATHER_WINDOW, width), index_map=lambda c, i: (c * per_core + i, 0)),
                      idx_spec, idx_spec],
            out_specs=[],
            core_axis_name=("core", "subcore"),
            dimension_semantics=(pltpu.PARALLEL, pltpu.PARALLEL),
        )(rows_hbm, idx0_hbm, idx1_hbm)

    return scatter_kernel(rows, dest[0].reshape(1, T), dest[1].reshape(1, T))


def _expert_block(rows, n_valid, xs_refs, wgu_ref, wdn_ref, ys_refs):
    packed_in = jnp.concatenate([r[rows] for r in xs_refs], axis=1)
    row = lax.broadcasted_iota(jnp.int32, packed_in.shape, 0)
    packed_in = jnp.where(row < n_valid, packed_in, 0)
    xb = _unpack_bf16_pairs(lax.bitcast_convert_type(packed_in, jnp.uint32)).astype(BF16)
    gu = jnp.dot(xb, wgu_ref[...], preferred_element_type=F32)
    gate, up = gu[:, :EXPERT_FF], gu[:, EXPERT_FF:]
    act = (gate * jax.nn.sigmoid(gate) * up).astype(BF16)
    y = jnp.dot(act, wdn_ref[...], preferred_element_type=F32)
    packed = lax.bitcast_convert_type(_pack_bf16_pairs(y.astype(BF16).astype(F32)), jnp.int32)
    w = packed.shape[1] // len(ys_refs)
    for n, ys_ref in enumerate(ys_refs):
        ys_ref[rows] = packed[:, n * w:(n + 1) * w]


def _expert_kernel(be_ref, nv_ref, *refs):
    xs_refs, w_refs, ys_refs = refs[:Y_SLABS], refs[Y_SLABS:-Y_SLABS], refs[-Y_SLABS:]
    for j in range(BLOCKS_PER_STEP):
        _expert_block(slice(j * MOE_BLOCK, (j + 1) * MOE_BLOCK),
                      nv_ref[pl.program_id(0) * BLOCKS_PER_STEP + j],
                      xs_refs, w_refs[2 * j], w_refs[2 * j + 1], ys_refs)


def _experts(xs_slabs, block_e, block_valid, w_gu, w_down):
    P, slab_w = xs_slabs[0].shape
    n_blocks = P // MOE_BLOCK
    bps = BLOCKS_PER_STEP
    slab_spec = pl.BlockSpec((bps * MOE_BLOCK, slab_w), lambda b, be, nv: (b, 0))
    w_specs, w_args = [], []
    for j in range(bps):
        w_specs += [
            pl.BlockSpec((None, D_MODEL, 2 * EXPERT_FF), lambda b, be, nv, j=j: (be[b * bps + j], 0, 0)),
            pl.BlockSpec((None, EXPERT_FF, D_MODEL), lambda b, be, nv, j=j: (be[b * bps + j], 0, 0))]
        w_args += [w_gu, w_down]
    grid_spec = pltpu.PrefetchScalarGridSpec(
        num_scalar_prefetch=2,
        grid=(n_blocks // bps,),
        in_specs=[slab_spec] * Y_SLABS + w_specs,
        out_specs=[slab_spec] * Y_SLABS,
    )
    return pl.pallas_call(
        _expert_kernel,
        grid_spec=grid_spec,
        out_shape=[jax.ShapeDtypeStruct((P, slab_w), jnp.int32)] * Y_SLABS,
        compiler_params=pltpu.CompilerParams(dimension_semantics=("arbitrary",),
                                             vmem_limit_bytes=VMEM_LIMIT),
        name="experts",
    )(block_e, block_valid, *xs_slabs, *w_args)


def _gather_rows(table, indices):
    n, width = indices.shape[0], table.shape[1]
    mesh = plsc.VectorSubcoreMesh(core_axis_name="core", subcore_axis_name="subcore")

    @pl.kernel(out_type=jax.ShapeDtypeStruct((n, width), table.dtype), mesh=mesh)
    def gather_kernel(table_hbm, idx_hbm, out_hbm):
        def body(idx_vmem, out_vmem):
            pltpu.sync_copy(table_hbm.at[idx_vmem.at[0]], out_vmem)

        per_core = n // GATHER_WINDOW // SC_CORES
        pltpu.emit_pipeline(
            body,
            grid=(SC_CORES, per_core),
            in_specs=[pl.BlockSpec((1, GATHER_WINDOW), index_map=lambda c, i: (0, c * per_core + i))],
            out_specs=[pl.BlockSpec((GATHER_WINDOW, width), index_map=lambda c, i: (c * per_core + i, 0))],
            core_axis_name=("core", "subcore"),
            dimension_semantics=(pltpu.PARALLEL, pltpu.PARALLEL),
        )(idx_hbm, out_hbm)

    return gather_kernel(table, indices.reshape(1, n))


def _combine_kernel(x2_ref, route_ref, g_ref, *refs):
    y_refs, out_ref = refs[:-1], refs[-1]
    route = route_ref[...]

    def expert_rows(slabs):
        packed = jnp.concatenate([s[...] for s in slabs], axis=1)
        return _unpack_bf16_pairs(lax.bitcast_convert_type(packed, jnp.uint32))

    moe = expert_rows(y_refs[:Y_SLABS]) * route[:, 2:3] + expert_rows(y_refs[Y_SLABS:]) * route[:, 3:4]
    out_ref[...] = _rms(x2_ref[...] + moe, g_ref[...])


def _combine(x2, route, g_final, dest, ys_slabs):
    T = x2.shape[0]
    tm = MIX_TILE
    n_steps = T // tm
    rows = [_gather_rows(ys, dest.reshape(-1)) for ys in ys_slabs]
    slab_w = rows[0].shape[1]
    first = [pl.BlockSpec((tm, slab_w), lambda i: (i, 0))] * Y_SLABS
    second = [pl.BlockSpec((tm, slab_w), lambda i: (i + n_steps, 0))] * Y_SLABS
    return pl.pallas_call(
        _combine_kernel,
        grid=(n_steps,),
        in_specs=[pl.BlockSpec((tm, D_MODEL), lambda i: (i, 0)),
                  pl.BlockSpec((tm, ROUTE_W), lambda i: (i, 0)),
                  _const_spec((1, D_MODEL))] + first + second,
        out_specs=pl.BlockSpec((tm, D_MODEL), lambda i: (i, 0)),
        out_shape=jax.ShapeDtypeStruct((T, D_MODEL), F32),
        compiler_params=pltpu.CompilerParams(dimension_semantics=("arbitrary",)),
        name="combine",
    )(x2, route, g_final, *rows, *rows)


def _routing_tables(route, counts_rec, n_tokens):
    n_slots = 2 * n_tokens
    n_blocks = n_slots // MOE_BLOCK + N_EXPERTS
    n_rows = n_blocks * MOE_BLOCK
    counts = counts_rec[:, 0].astype(jnp.int32)
    padded = (counts + MOE_BLOCK - 1) // MOE_BLOCK * MOE_BLOCK
    pends = jnp.cumsum(padded)
    pstarts = pends - padded
    expert = route[:, 0:2].astype(jnp.int32)
    rank = route[:, 4:6].astype(jnp.int32)
    lookup = lambda table, idx: jnp.sum(jnp.where(idx[..., None] == jnp.arange(N_EXPERTS), table, 0), axis=-1)
    count_le = lambda sorted_vals, q: jnp.sum(sorted_vals[None, :] <= q[:, None], axis=-1).astype(jnp.int32)
    dest = (lookup(pstarts, expert) + rank).T
    block_row0 = jnp.arange(n_blocks, dtype=jnp.int32) * MOE_BLOCK
    owner = count_le(pends, block_row0)
    block_e = jnp.minimum(owner, N_EXPERTS - 1)
    block_valid = jnp.clip(lookup(pstarts + counts, owner) - block_row0, 0, MOE_BLOCK)
    block_valid = jnp.where(owner < N_EXPERTS, block_valid, 0).astype(jnp.int32)
    return dest.astype(jnp.int32), block_e, block_valid, n_rows


def _encoder_group(x, mem, w):
    batch, seq, _ = x.shape
    T = batch * seq
    *qkv_groups, qkv_win = _proj(x, w["g_mix"], w["w_in"])

    slopes_a = _alibi_slopes(A_HEADS).reshape(A_GROUPS, A_HEADS_PER_GROUP)
    o_groups, lse_groups = [], []
    for g, (window, r) in enumerate(A_CONFIGS):
        o, lse = _banded_attention(qkv_groups[g], half_w=window // (2 * r), kv_width=GROUP_W,
                                   slopes=slopes_a[g] * np.float32(r))
        o_groups.append(o)
        lse_groups.append(lse)
    (o_b,) = _banded_attention(qkv_win.reshape(batch, 1, seq, WIN_W), half_w=B_HALF_WINDOW,
                               kv_width=KV2_W, slopes=_alibi_slopes(B_HEADS), sink=w["sink_b"],
                               want_lse=False)

    kvm = _memkv(mem.reshape(-1, D_MODEL), w["g_mem"], w["w_ckv"])
    mix_w = [w[k] for k in ("g_mix", "w_gate", "b_gate", "w_branch", "w_out", "g_xattn", "w_cq",
                            "w_co", "g_ffn", "w_route", "b_route")]
    x2, *h_slabs, route, counts_rec = _mix(x, o_groups, lse_groups, o_b, kvm, mix_w)

    dest, block_e, block_valid, n_rows = _routing_tables(route, counts_rec, T)
    xs_slabs = [_scatter_rows(h, dest, n_rows) for h in h_slabs]
    ys_slabs = _experts(xs_slabs, block_e, block_valid, w["w_gu"], w["w_down"])
    y = _combine(x2, route, w["g_final"], dest, ys_slabs)
    return y.reshape(batch, seq, D_MODEL)


def _prep_weights(g_mix, w_in, sink_b, w_gate, b_gate, w_branch, w_out, g_xattn, g_mem, w_cq, w_ckv,
                  w_co, g_ffn, w_rg, b_rg, w_re, b_re, w_gu, w_down, g_final):
    scale = HEAD_DIM ** -0.5
    aw = A_WIDTH
    qa, ka, va = w_in[:, :aw] * scale, w_in[:, aw:2 * aw], w_in[:, 2 * aw:3 * aw]
    qb = w_in[:, 3 * aw:3 * aw + B_Q] * scale
    kb = w_in[:, 3 * aw + B_Q:3 * aw + B_Q + B_KV]
    vb = w_in[:, 3 * aw + B_Q + B_KV:]
    twice = lambda t: jnp.repeat(t.reshape(D_MODEL, B_KV_HEADS, 1, HEAD_DIM), 2, axis=2).reshape(D_MODEL, KV2_W)
    group = lambda t, g: t[:, g * GROUP_W:(g + 1) * GROUP_W]
    cols = [group(t, g) for g in range(A_GROUPS) for t in (qa, ka, va)] + [qb, twice(kb), twice(vb)]
    w_in_x = jnp.concatenate(cols, axis=1).astype(BF16)
    w_route = jnp.zeros((D_MODEL, ROUTE_W), F32)
    w_route = w_route.at[:, :N_GROUPS].set(w_rg).at[:, EXPERT_LANE0:EXPERT_LANE0 + N_EXPERTS].set(w_re)
    b_route = jnp.zeros((1, ROUTE_W), F32)
    b_route = b_route.at[0, :N_GROUPS].set(b_rg).at[0, EXPERT_LANE0:EXPERT_LANE0 + N_EXPERTS].set(b_re)
    w_route_hi = w_route.astype(BF16)
    w_route = jnp.concatenate([w_route_hi, (w_route - w_route_hi.astype(F32)).astype(BF16)], axis=1)
    vec = lambda v: v.reshape(1, -1).astype(F32)
    return dict(
        g_mix=vec(g_mix), w_in=w_in_x, sink_b=sink_b.astype(F32),
        w_gate=w_gate.astype(BF16), b_gate=vec(b_gate), w_branch=w_branch.astype(BF16),
        w_out=w_out.astype(BF16), g_xattn=vec(g_xattn), g_mem=vec(g_mem), w_cq=w_cq.astype(BF16),
        w_ckv=w_ckv.astype(BF16), w_co=w_co.astype(BF16), g_ffn=vec(g_ffn),
        w_route=w_route, b_route=b_route, w_gu=w_gu.astype(BF16), w_down=w_down.astype(BF16),
        g_final=vec(g_final))


def kernel(x_prompt, x_sample, mem_prompt, mem_sample, g_mix, w_in, sink_b, w_gate, b_gate, w_branch,
           w_out, g_xattn, g_mem, w_cq, w_ckv, w_co, g_ffn, w_rg, b_rg, w_re, b_re, w_gu, w_down,
           g_final):
    assert g_mix.shape[0] == 1, "single-layer encoder"
    w = _prep_weights(g_mix[0], w_in[0], sink_b[0], w_gate[0], b_gate[0], w_branch[0], w_out[0],
                      g_xattn[0], g_mem[0], w_cq[0], w_ckv[0], w_co[0], g_ffn[0], w_rg[0], b_rg[0],
                      w_re[0], b_re[0], w_gu[0], w_down[0], g_final)
    y_sample = _encoder_group(x_sample, mem_sample, w)
    y_prompt = _encoder_group(x_prompt, mem_prompt, w)
    return (y_prompt, y_sample)
```
